```python
import math
import jax
import jax.numpy as jnp
from jax import lax
import numpy as np

D_MODEL = 1024
BATCH = 8
SEQ = 4096
DEPTH = 4

GRID_W = 64
CTX_LEN = 256
EPS = 1e-6

DA_HEADS = 8
DA_HEAD_DIM = D_MODEL // (2 * DA_HEADS)
DA_V_DIM = 2 * DA_HEAD_DIM
DA_QK_W = DA_HEADS * 2 * DA_HEAD_DIM
DA_V_W = DA_HEADS * DA_V_DIM
Q_BLOCK = 128
ROPE_BASE = 10000.0

LRU_W = D_MODEL
LRU_BLOCKS = 8
LRU_BW = LRU_W // LRU_BLOCKS
LRU_CONV = 4
LRU_CONV_LEFT = 2
LRU_C = 8.0

HY_W = D_MODEL
HY_ORDER = 2
HY_SHORT = 3
HY_BANDS = 16
HY_EMB = 1 + 2 * HY_BANDS
HY_HID = 64
HY_DECAY_MIN = -math.log(1e-2) / 1.5
HY_DECAY_MAX = -math.log(1e-2) / 0.3

N_BRANCH = 3
BRANCH_W = D_MODEL

C_K = 0
C_V = C_K + DA_QK_W
C_LX = C_V + DA_V_W
C_Q = C_LX + LRU_W
C_LY = C_Q + DA_QK_W
C_HY = C_LY + LRU_W
C_G = C_HY + 3 * HY_W
C_END = C_G + N_BRANCH * D_MODEL

D_FF = 256 * ((8 * D_MODEL // 3 + 255) // 256)
N_EXPERTS = 8
TOP_K = 2
MOE_BLOCK = 256
N_DENSE = (DEPTH + 1) // 2
N_MOE = DEPTH // 2

kernel_name = 'hybrid_diffusion_trunk'


def rmsnorm(x, g):
    xf = x.astype(jnp.float32)
    y = xf * lax.rsqrt(jnp.mean(xf * xf, axis=-1, keepdims=True) + EPS)
    return (y * g.astype(jnp.float32)).astype(x.dtype)


def modulate(h, shift, scale):
    return h * (1.0 + scale) + shift


def dwconv(x, w, b, left):
    k, ch = w.shape
    y = lax.conv_general_dilated(x, w.astype(x.dtype)[:, None, :], window_strides=(1,),
                                 padding=[(left, k - 1 - left)],
                                 dimension_numbers=('NWC', 'WIO', 'NWC'), feature_group_count=ch)
    return y + b.astype(x.dtype)


def axial_rope(length):
    rows = length // GRID_W
    row = jnp.repeat(jnp.arange(rows, dtype=jnp.float32), GRID_W)
    col = jnp.tile(jnp.arange(GRID_W, dtype=jnp.float32), rows)
    n_freq = DA_HEAD_DIM // 4
    inv = ROPE_BASE ** (-jnp.arange(n_freq, dtype=jnp.float32) / n_freq)
    ar = row[:, None] * inv
    ac = col[:, None] * inv
    ang = jnp.concatenate([ar, ar, ac, ac], axis=-1)
    return jnp.cos(ang), jnp.sin(ang)


def apply_axial_rope(t, cos, sin):
    ts = t.reshape(t.shape[:-1] + (2, 2, DA_HEAD_DIM // 4))
    rot = jnp.stack([-ts[..., 1, :], ts[..., 0, :]], axis=-2).reshape(t.shape)
    return t * cos[None, :, None, None, :] + rot * sin[None, :, None, None, :]


def split_qk(t):
    return t.reshape(t.shape[:2] + (DA_HEADS, 2, DA_HEAD_DIM))


def split_v(t):
    return t.reshape(t.shape[:2] + (DA_HEADS, DA_V_DIM))


def diff_attn_core(q, k, v, lam):
    s = jnp.einsum('bqhcd,bkhcd->bhcqk', q, k, preferred_element_type=jnp.float32) * (DA_HEAD_DIM ** -0.5)
    p = jax.nn.softmax(s, axis=-1)
    w = p[:, :, 0] - lam * p[:, :, 1]
    return jnp.einsum('bhqk,bkhe->bqhe', w.astype(v.dtype), v)


def head_out(o, g, lam_init):
    o = rmsnorm(o, g) * (1.0 - lam_init)
    return o.reshape(o.shape[:2] + (DA_V_W,))


def diff_attention(p_l, p_c, cos, sin, lam_vecs, subln_g, lam_init, ctx_out):
    bsz, length = p_l.shape[:2]
    dt = p_l.dtype
    lv = lam_vecs.astype(jnp.float32)
    lam = jnp.exp(jnp.sum(lv[0] * lv[1])) - jnp.exp(jnp.sum(lv[2] * lv[3])) + lam_init
    q_l = apply_axial_rope(split_qk(p_l[..., C_Q:C_LY]), cos, sin).astype(dt)
    k_l = apply_axial_rope(split_qk(p_l[..., C_K:C_V]), cos, sin).astype(dt)
    k_c = split_qk(p_c[..., C_K:C_V])
    v_c = split_v(p_c[..., C_V:C_LX])
    k_all = jnp.concatenate([k_l, k_c], axis=1)
    v_all = jnp.concatenate([split_v(p_l[..., C_V:C_LX]), v_c], axis=1)
    nb = length // Q_BLOCK
    q_blocks = jnp.swapaxes(q_l.reshape((bsz, nb, Q_BLOCK) + q_l.shape[2:]), 0, 1)
    o_l = lax.map(lambda qb: diff_attn_core(qb, k_all, v_all, lam), q_blocks)
    o_l = jnp.swapaxes(o_l, 0, 1).reshape(bsz, length, DA_HEADS, DA_V_DIM)
    out_l = head_out(o_l, subln_g, lam_init)
    out_c = None
    if ctx_out:
        q_c = split_qk(p_c[..., C_Q:C_LY])
        out_c = head_out(diff_attn_core(q_c, k_c, v_c, lam), subln_g, lam_init)
    return out_l, out_c


def block_diag(x, w, b):
    xb = x.reshape(x.shape[:-1] + (LRU_BLOCKS, LRU_BW))
    return jnp.einsum('blni,nio->blno', xb, w).reshape(x.shape) + b


def linear_scan(a, b, h0, reverse):
    if reverse:
        b = b.at[:, -1].add(a[:, -1] * h0)
    else:
        b = b.at[:, 0].add(a[:, 0] * h0)

    def combine(e1, e2):
        a1, b1 = e1
        a2, b2 = e2
        return a1 * a2, a2 * b1 + b2

    _, h = lax.associative_scan(combine, (a, b), axis=1, reverse=reverse)
    return h


def rglru_scan(xc, wa, ba, wi, bi, lam, h0, reverse):
    r = jax.nn.sigmoid(block_diag(xc, wa, ba).astype(jnp.float32))
    i = jax.nn.sigmoid(block_diag(xc, wi, bi).astype(jnp.float32))
    log_a = -LRU_C * r * jax.nn.softplus(-lam.astype(jnp.float32))
    a = jnp.exp(log_a)
    b = jnp.sqrt(-jnp.expm1(2.0 * log_a)) * (i * xc.astype(jnp.float32))
    return linear_scan(a, b, h0, reverse)


def rglru_block(p_l, p_c, conv_w, conv_b, wa, ba, wi, bi, lam, ctx_out):
    xl = dwconv(p_l[..., C_LX:C_Q], conv_w, conv_b, LRU_CONV_LEFT)
    xcx = dwconv(p_c[..., C_LX:C_Q], conv_w, conv_b, LRU_CONV_LEFT)
    h0 = jnp.zeros((p_l.shape[0], LRU_W), jnp.float32)
    hc_f = rglru_scan(xcx, wa[0], ba[0], wi[0], bi[0], lam[0], h0, False)
    hc_b = rglru_scan(xcx, wa[1], ba[1], wi[1], bi[1], lam[1], h0, True)
    hl_f = rglru_scan(xl, wa[0], ba[0], wi[0], bi[0], lam[0], hc_f[:, -1], False)
    hl_b = rglru_scan(xl, wa[1], ba[1], wi[1], bi[1], lam[1], hc_b[:, 0], True)
    out_l = jax.nn.gelu(p_l[..., C_LY:C_HY]) * (hl_f + hl_b).astype(p_l.dtype)
    out_c = None
    if ctx_out:
        out_c = jax.nn.gelu(p_c[..., C_LY:C_HY]) * (hc_f + hc_b).astype(p_c.dtype)
    return out_l, out_c


def hyena_filters(length, w1, b1, w2, b2, freq, w3, decay):
    f32 = jnp.float32
    t = jnp.linspace(0.0, 1.0, length, dtype=f32)[:, None]
    w = 2.0 * math.pi * jnp.arange(length, dtype=f32)[:, None] / length
    bands = jnp.linspace(1e-4, HY_BANDS - 1, HY_BANDS, dtype=f32)
    z = jnp.concatenate([t, jnp.cos(bands * w), -jnp.sin(bands * w)], axis=-1)
    fr = freq.astype(f32)
    hdn = jnp.sin(fr * (z @ w1.astype(f32) + b1.astype(f32)))
    hdn = jnp.sin(fr * (hdn @ w2.astype(f32) + b2.astype(f32)))
    filt = (hdn @ w3.astype(f32)).reshape(length, HY_ORDER, 2, HY_W)
    window = jnp.exp(-t[:, :, None] * jnp.abs(decay.astype(f32)))
    filt = filt * window[:, :, None, :]
    return filt / (jnp.sum(jnp.abs(filt), axis=(0, 2), keepdims=True) + EPS)


def bidir_long_conv(u, h_fwd, h_bwd, skip):
    length = u.shape[1]
    hc = jnp.concatenate([h_fwd[:1] + h_bwd[:1], h_fwd[1:], jnp.zeros_like(h_fwd[:1]), h_bwd[:0:-1]], axis=0)
    y = jnp.fft.irfft(jnp.fft.rfft(u, n=2 * length, axis=1) * jnp.fft.rfft(hc, axis=0)[None],
                      n=2 * length, axis=1)[:, :length]
    return y + skip * u


def hyena_seq(p, conv_w, conv_b, w1, b1, w2, b2, freq, w3, decay, skip):
    u = dwconv(p, conv_w, conv_b, 1).astype(jnp.float32)
    v, x1, x2 = jnp.split(u, 3, axis=-1)
    filt = hyena_filters(u.shape[1], w1, b1, w2, b2, freq, w3, decay)
    sk = skip.astype(jnp.float32)
    z = x1 * bidir_long_conv(v, filt[:, 0, 0], filt[:, 0, 1], sk[0])
    return x2 * bidir_long_conv(z, filt[:, 1, 0], filt[:, 1, 1], sk[1])


def merge_branches(p, branches, b_gate, w_br, w_out):
    gates = jax.nn.sigmoid(p[..., C_G:C_END] + b_gate)
    m = None
    for k in range(N_BRANCH):
        term = gates[..., k * D_MODEL:(k + 1) * D_MODEL] * (branches[k].astype(p.dtype) @ w_br[k])
        m = term if m is None else m + term
    return m @ w_out


def swiglu(x, w_gate, w_up, w_down):
    return (jax.nn.silu(x @ w_gate) * (x @ w_up)) @ w_down


def moe_swiglu(x, w_router, w_gate, w_up, w_down):
    shp = x.shape
    xt = x.reshape(-1, shp[-1])
    n_tok = xt.shape[0]
    n_asg = n_tok * TOP_K
    logits = jnp.einsum('td,de->te', xt, w_router, preferred_element_type=jnp.float32)
    top_v, top_i = lax.top_k(logits, TOP_K)
    top_w = jax.nn.softmax(top_v, axis=-1)
    e_flat = top_i.reshape(-1)
    tok_flat = jnp.repeat(jnp.arange(n_tok, dtype=jnp.int32), TOP_K)
    w_flat = top_w.reshape(-1)
    order = jnp.argsort(e_flat)
    e_sorted = e_flat[order]
    counts = jnp.bincount(e_flat, length=N_EXPERTS)
    padded = ((counts + MOE_BLOCK - 1) // MOE_BLOCK) * MOE_BLOCK
    pad_end = jnp.cumsum(padded)
    pad_start = pad_end - padded
    start = jnp.cumsum(counts) - counts
    dest = pad_start[e_sorted] + jnp.arange(n_asg, dtype=jnp.int32) - start[e_sorted]
    n_blocks = -(-n_asg // MOE_BLOCK) + N_EXPERTS
    n_slots = n_blocks * MOE_BLOCK
    slot_tok = jnp.zeros((n_slots,), jnp.int32).at[dest].set(tok_flat[order])
    slot_w = jnp.zeros((n_slots,), jnp.float32).at[dest].set(w_flat[order])
    block_e = jnp.minimum(jnp.searchsorted(pad_end, jnp.arange(n_blocks) * MOE_BLOCK, side='right'), N_EXPERTS - 1)
    xs = xt[slot_tok].reshape(n_blocks, MOE_BLOCK, shp[-1])

    def expert_block(args):
        xb, e = args
        return (jax.nn.silu(xb @ w_gate[e]) * (xb @ w_up[e])) @ w_down[e]

    ys = lax.map(expert_block, (xs, block_e)).reshape(n_slots, shp[-1])
    out = jnp.zeros_like(xt).at[slot_tok].add(ys * slot_w[:, None].astype(ys.dtype))
    return out.reshape(shp)


def setup_inputs(seed: int = 0) -> dict:
    key = jax.random.key(seed)
    keys = iter(jax.random.split(key, 48))
    f32 = jnp.float32
    D = D_MODEL

    def nrm(shape, std):
        return jax.random.normal(next(keys), shape, f32) * std

    def gain(shape):
        return 1.0 + nrm(shape, 0.05)

    u = jax.random.uniform(next(keys), (DEPTH, 2, LRU_W), f32, 0.9, 0.999)
    a_base = u ** (1.0 / LRU_C)
    lru_lambda = jnp.log(a_base) - jnp.log1p(-a_base)
    hy_decay = jnp.broadcast_to(jnp.linspace(HY_DECAY_MIN, HY_DECAY_MAX, HY_W, dtype=f32),
                                (DEPTH, HY_ORDER, HY_W)) + nrm((DEPTH, HY_ORDER, HY_W), 0.1)
    return {
        'x': nrm((BATCH, SEQ, D), 1.0),
        'c': nrm((BATCH, D), 1.0),
        'ctx': nrm((BATCH, CTX_LEN, D), 1.0),
        'c_ctx': nrm((D,), 1.0),
        'w_mod': nrm((DEPTH, D, 6 * D), 0.5 * D ** -0.5),
        'b_mod': nrm((DEPTH, 6 * D), 0.02),
        'g_mix': gain((DEPTH, D)),
        'g_ffn': gain((DEPTH, D)),
        'w_in': nrm((DEPTH, D, C_END), D ** -0.5),
        'b_gate': nrm((DEPTH, N_BRANCH * D), 0.02),
        'w_br': nrm((DEPTH, N_BRANCH, BRANCH_W, D), BRANCH_W ** -0.5),
        'w_out': nrm((DEPTH, D, D), D ** -0.5),
        'da_lambda': nrm((DEPTH, 4, DA_HEAD_DIM), 0.1),
        'da_subln_g': gain((DEPTH, DA_V_DIM)),
        'lru_conv_w': nrm((DEPTH, LRU_CONV, LRU_W), LRU_CONV ** -0.5),
        'lru_conv_b': nrm((DEPTH, LRU_W), 0.02),
        'lru_wa': nrm((DEPTH, 2, LRU_BLOCKS, LRU_BW, LRU_BW), LRU_BW ** -0.5),
        'lru_ba': nrm((DEPTH, 2, LRU_W), 0.02),
        'lru_wi': nrm((DEPTH, 2, LRU_BLOCKS, LRU_BW, LRU_BW), LRU_BW ** -0.5),
        'lru_bi': nrm((DEPTH, 2, LRU_W), 0.02),
        'lru_lambda': lru_lambda,
        'hy_conv_w': nrm((DEPTH, HY_SHORT, 3 * HY_W), HY_SHORT ** -0.5),
        'hy_conv_b': nrm((DEPTH, 3 * HY_W), 0.02),
        'hy_f_w1': nrm((DEPTH, HY_EMB, HY_HID), HY_EMB ** -0.5),
        'hy_f_b1': nrm((DEPTH, HY_HID), 0.02),
        'hy_f_w2': nrm((DEPTH, HY_HID, HY_HID), HY_HID ** -0.5),
        'hy_f_b2': nrm((DEPTH, HY_HID), 0.02),
        'hy_f_freq': gain((DEPTH, HY_HID)),
        'hy_f_w3': nrm((DEPTH, HY_HID, HY_ORDER * 2 * HY_W), HY_HID ** -0.5),
        'hy_decay': hy_decay,
        'hy_skip': nrm((DEPTH, HY_ORDER, HY_W), 1.0),
        'ffn_w_gate': nrm((N_DENSE, D, D_FF), D ** -0.5),
        'ffn_w_up': nrm((N_DENSE, D, D_FF), D ** -0.5),
        'ffn_w_down': nrm((N_DENSE, D_FF, D), D_FF ** -0.5),
        'moe_router': nrm((N_MOE, D, N_EXPERTS), D ** -0.5),
        'moe_w_gate': nrm((N_MOE, N_EXPERTS, D, D_FF), D ** -0.5),
        'moe_w_up': nrm((N_MOE, N_EXPERTS, D, D_FF), D ** -0.5),
        'moe_w_down': nrm((N_MOE, N_EXPERTS, D_FF, D), D_FF ** -0.5),
        'g_final': gain((D,)),
    }


def reference(x, c, ctx, c_ctx, w_mod, b_mod, g_mix, g_ffn, w_in, b_gate, w_br, w_out,
              da_lambda, da_subln_g, lru_conv_w, lru_conv_b, lru_wa, lru_ba, lru_wi, lru_bi,
              lru_lambda, hy_conv_w, hy_conv_b, hy_f_w1, hy_f_b1, hy_f_w2, hy_f_b2, hy_f_freq,
              hy_f_w3, hy_decay, hy_skip, ffn_w_gate, ffn_w_up, ffn_w_down, moe_router,
              moe_w_gate, moe_w_up, moe_w_down, g_final):
    bsz, length, dm = x.shape
    n_ctx = ctx.shape[1]
    cos, sin = axial_rope(length)
    silu_c = jax.nn.silu(c)
    silu_cc = jax.nn.silu(c_ctx)
    xc = ctx
    for li in range(DEPTH):
        last = li == DEPTH - 1
        ctx_out = not last
        n_mod_c = 2 if last else 6
        lam_init = 0.8 - 0.6 * math.exp(-0.3 * li)
        mod_l = jnp.split((silu_c @ w_mod[li] + b_mod[li])[:, None, :], 6, axis=-1)
        mod_c = jnp.split(silu_cc @ w_mod[li][:, :n_mod_c * dm] + b_mod[li][:n_mod_c * dm], n_mod_c)

        h_l = modulate(rmsnorm(x, g_mix[li]), mod_l[0], mod_l[1])
        h_c = modulate(rmsnorm(xc, g_mix[li]), mod_c[0], mod_c[1])
        p_l = h_l @ w_in[li]
        p_c = h_c @ w_in[li][:, :(C_END if ctx_out else C_Q)]

        a_l, a_c = diff_attention(p_l, p_c, cos, sin, da_lambda[li], da_subln_g[li], lam_init, ctx_out)
        r_l, r_c = rglru_block(p_l, p_c, lru_conv_w[li], lru_conv_b[li], lru_wa[li], lru_ba[li],
                               lru_wi[li], lru_bi[li], lru_lambda[li], ctx_out)
        hy_args = (hy_conv_w[li], hy_conv_b[li], hy_f_w1[li], hy_f_b1[li], hy_f_w2[li], hy_f_b2[li],
                   hy_f_freq[li], hy_f_w3[li], hy_decay[li], hy_skip[li])
        y_l = hyena_seq(p_l[..., C_HY:C_G], *hy_args)
        x = x + mod_l[2] * merge_branches(p_l, (a_l, r_l, y_l), b_gate[li], w_br[li], w_out[li])
        if ctx_out:
            y_c = hyena_seq(p_c[..., C_HY:C_G], *hy_args)
            xc = xc + mod_c[2] * merge_branches(p_c, (a_c, r_c, y_c), b_gate[li], w_br[li], w_out[li])

        f_l = modulate(rmsnorm(x, g_ffn[li]), mod_l[3], mod_l[4])
        if ctx_out:
            f_c = modulate(rmsnorm(xc, g_ffn[li]), mod_c[3], mod_c[4])
            tokens = jnp.concatenate([f_c, f_l], axis=1)
        else:
            tokens = f_l
        if li % 2 == 0:
            j = li // 2
            y = swiglu(tokens, ffn_w_gate[j], ffn_w_up[j], ffn_w_down[j])
        else:
            j = li // 2
            y = moe_swiglu(tokens, moe_router[j], moe_w_gate[j], moe_w_up[j], moe_w_down[j])
        x = x + mod_l[5] * y[:, -length:]
        if ctx_out:
            xc = xc + mod_c[5] * y[:, :n_ctx]
    return rmsnorm(x, g_final)
```

```python
import functools
import math

import jax
import jax.numpy as jnp
from jax import lax
from jax.experimental import pallas as pl
from jax.experimental.pallas import tpu as pltpu

F32 = jnp.float32
BF16 = jnp.bfloat16

EPS = 1e-6
ROW_G = 256
GRID_W = 64
ROPE_BASE = 10000.0
DA_HEADS = 8
DA_HEAD_DIM = 64
DA_V_DIM = 128
LRU_BLOCKS = 8
LRU_C = 8.0
HY_ORDER = 2
HY_BANDS = 16
N_EXPERTS = 8
TOP_K = 2
MOE_BLOCK = 256
HALO = 16
NEG_BIG = -1e30
VMEM_LIMIT = 56 * 1024 * 1024


def _cparams(n_axes):
    return pltpu.CompilerParams(dimension_semantics=("arbitrary",) * n_axes,
                                vmem_limit_bytes=VMEM_LIMIT)


def _pick(n, prefs):
    for p in prefs:
        if n % p == 0:
            return p
    raise ValueError(f"no tile in {prefs} divides {n}")


def _mod_row(tab_ref, k, group, groups_per_batch, n_batch):
    b = group // groups_per_batch
    row = jnp.where(group % groups_per_batch == 0, n_batch, b)
    return tab_ref[k, pl.ds(row, 1), :]


def _mm_f32_kernel(a_ref, w_ref, b_ref, o_ref):
    o_ref[...] = jnp.dot(a_ref[...], w_ref[...], preferred_element_type=F32) + b_ref[...]


def mm_f32_bias(a, w, b, tn):
    m, k = a.shape
    n = w.shape[1]
    return pl.pallas_call(
        _mm_f32_kernel, grid=(n // tn,),
        in_specs=[pl.BlockSpec((m, k), lambda j: (0, 0)),
                  pl.BlockSpec((k, tn), lambda j: (0, j)),
                  pl.BlockSpec((1, tn), lambda j: (0, j))],
        out_specs=pl.BlockSpec((m, tn), lambda j: (0, j)),
        out_shape=jax.ShapeDtypeStruct((m, n), F32),
        compiler_params=_cparams(1), name="mod_matmul")(a, w, b.reshape(1, n))


def _norm_mod_kernel(x_ref, g_ref, tab_ref, *rest, k_shift, k_scale, gpb, n_batch, router):
    if router:
        wr_ref, o_ref, lg_ref = rest
    else:
        (o_ref,) = rest
    nsub = x_ref.shape[0] // ROW_G
    for s in range(nsub):
        grp = pl.program_id(0) * nsub + s
        rows = pl.ds(s * ROW_G, ROW_G)
        xs = x_ref[rows, :]
        y = xs * lax.rsqrt(jnp.mean(xs * xs, axis=-1, keepdims=True) + EPS) * g_ref[...]
        shift = _mod_row(tab_ref, k_shift, grp, gpb, n_batch)
        scale = _mod_row(tab_ref, k_scale, grp, gpb, n_batch)
        h = y * (1.0 + scale) + shift
        o_ref[rows, :] = h.astype(o_ref.dtype)
        if router:
            lg_ref[rows, :] = jnp.dot(h, wr_ref[...], preferred_element_type=F32,
                                      precision=lax.Precision.HIGHEST)


def norm_mod(x, g, tab, k_shift, k_scale, gpb, n_batch, w_router=None):
    nt, d = x.shape
    tm = _pick(nt, (1024, 512, 256))
    router = w_router is not None
    kern = functools.partial(_norm_mod_kernel, k_shift=k_shift, k_scale=k_scale, gpb=gpb,
                             n_batch=n_batch, router=router)
    in_specs = [pl.BlockSpec((tm, d), lambda i: (i, 0)),
                pl.BlockSpec((1, d), lambda i: (0, 0)),
                pl.BlockSpec(tab.shape, lambda i: (0, 0, 0))]
    out_specs = [pl.BlockSpec((tm, d), lambda i: (i, 0))]
    out_shape = [jax.ShapeDtypeStruct((nt, d), BF16)]
    args = [x, g.reshape(1, d), tab]
    if router:
        in_specs.append(pl.BlockSpec(w_router.shape, lambda i: (0, 0)))
        out_specs.append(pl.BlockSpec((tm, w_router.shape[1]), lambda i: (i, 0)))
        out_shape.append(jax.ShapeDtypeStruct((nt, w_router.shape[1]), F32))
        args.append(w_router)
    out = pl.pallas_call(kern, grid=(nt // tm,), in_specs=in_specs, out_specs=out_specs,
                         out_shape=out_shape, compiler_params=_cparams(1), name="norm_mod")(*args)
    return out if router else out[0]


def _final_norm_kernel(x_ref, g_ref, o_ref):
    xs = x_ref[...]
    o_ref[...] = xs * lax.rsqrt(jnp.mean(xs * xs, axis=-1, keepdims=True) + EPS) * g_ref[...]


def final_norm(x3, g, n_ctx):
    b, tb, d = x3.shape
    length = tb - n_ctx
    tm = ROW_G
    off = n_ctx // tm
    return pl.pallas_call(
        _final_norm_kernel, grid=(b, length // tm),
        in_specs=[pl.BlockSpec((None, tm, d), lambda i, j: (i, j + off, 0)),
                  pl.BlockSpec((1, d), lambda i, j: (0, 0))],
        out_specs=pl.BlockSpec((None, tm, d), lambda i, j: (i, j, 0)),
        out_shape=jax.ShapeDtypeStruct((b, length, d), F32),
        compiler_params=_cparams(2), name="final_norm")(x3, g.reshape(1, d))


def _mm_kernel(a_ref, w_ref, o_ref):
    o_ref[...] = jnp.dot(a_ref[...], w_ref[...], preferred_element_type=F32).astype(o_ref.dtype)


def mm_bf16(a, w, out_dtype, tm, tn, name):
    m, k = a.shape
    n = w.shape[1]
    return pl.pallas_call(
        _mm_kernel, grid=(n // tn, m // tm),
        in_specs=[pl.BlockSpec((tm, k), lambda j, i: (i, 0)),
                  pl.BlockSpec((k, tn), lambda j, i: (0, j))],
        out_specs=pl.BlockSpec((tm, tn), lambda j, i: (i, j)),
        out_shape=jax.ShapeDtypeStruct((m, n), out_dtype),
        compiler_params=_cparams(2), name=name)(a, w)


def _rope_kernel(pq_ref, pk_ref, cos_ref, sa_ref, sb_ref, q_ref, k_ref, *, q_scale):
    width = pq_ref.shape[1]
    cos, sa, sb = cos_ref[...], sa_ref[...], sb_ref[...]

    def rot(t):
        return t * cos + pltpu.roll(t, width - 16, 1) * sa + pltpu.roll(t, 16, 1) * sb

    q_ref[...] = (rot(pq_ref[...].astype(F32)) * q_scale).astype(q_ref.dtype)
    k_ref[...] = rot(pk_ref[...].astype(F32)).astype(k_ref.dtype)


def rope_tables(length, n_ctx, width):
    rows = length // GRID_W
    row = jnp.repeat(jnp.arange(rows, dtype=F32), GRID_W)
    col = jnp.tile(jnp.arange(GRID_W, dtype=F32), rows)
    n_freq = DA_HEAD_DIM // 4
    inv = ROPE_BASE ** (-jnp.arange(n_freq, dtype=F32) / n_freq)
    ar = row[:, None] * inv
    ac = col[:, None] * inv
    ang = jnp.concatenate([ar, ar, ac, ac], axis=-1)
    ang = jnp.concatenate([jnp.zeros((n_ctx, DA_HEAD_DIM), F32), ang], axis=0)
    reps = width // DA_HEAD_DIM
    cos = jnp.tile(jnp.cos(ang), (1, reps))
    sin = jnp.tile(jnp.sin(ang), (1, reps))
    first = (jnp.arange(width) % (2 * n_freq)) < n_freq
    sa = jnp.where(first[None, :], -sin, 0.0)
    sb = jnp.where(first[None, :], 0.0, sin)
    return cos, sa, sb


def rope_qk(p, tables, n_batch, gpb, cb_q, cb_k, width):
    nt = p.shape[0]
    cos, sa, sb = tables
    kern = functools.partial(_rope_kernel, q_scale=DA_HEAD_DIM ** -0.5)
    tspec = pl.BlockSpec((ROW_G, width), lambda b, j: (j, 0))
    ospec = pl.BlockSpec((ROW_G, width), lambda b, j: (b * gpb + j, 0))
    return pl.pallas_call(
        kern, grid=(n_batch, gpb),
        in_specs=[pl.BlockSpec((ROW_G, width), lambda b, j: (b * gpb + j, cb_q)),
                  pl.BlockSpec((ROW_G, width), lambda b, j: (b * gpb + j, cb_k)),
                  tspec, tspec, tspec],
        out_specs=[ospec, ospec],
        out_shape=[jax.ShapeDtypeStruct((nt, width), BF16)] * 2,
        compiler_params=_cparams(2), name="rope")(p, p, cos, sa, sb)


def _attn_kernel(lamv_ref, g_ref, q_ref, k_ref, v_ref, o_ref, m_ref, l_ref, acc_ref,
                 *, lam_init, n_ctx, tk):
    j = pl.program_id(2)
    q = q_ref[...]
    lane = lax.broadcasted_iota(jnp.int32, q.shape, 1)
    zero = jnp.zeros_like(q)
    qq = jnp.concatenate([jnp.where(lane < DA_HEAD_DIM, q, zero),
                          jnp.where(lane >= DA_HEAD_DIM, q, zero)], axis=0)
    tq = q.shape[0]
    m_ref[...] = jnp.full(m_ref.shape, NEG_BIG, F32)
    l_ref[...] = jnp.zeros(l_ref.shape, F32)
    acc_ref[...] = jnp.zeros(acc_ref.shape, F32)

    def chunk(start, size):
        kc = k_ref[pl.ds(start, size), :]
        vc = v_ref[pl.ds(start, size), :]
        s = lax.dot_general(qq, kc, (((1,), (1,)), ((), ())), preferred_element_type=F32)
        m_prev = m_ref[...]
        m_new = jnp.maximum(m_prev, jnp.max(s, axis=-1, keepdims=True))
        alpha = jnp.exp(m_prev - m_new)
        p = jnp.exp(s - m_new)
        l_ref[...] = alpha * l_ref[...] + jnp.sum(p, axis=-1, keepdims=True)
        acc_ref[...] = alpha * acc_ref[...] + jnp.dot(p.astype(BF16), vc, preferred_element_type=F32)
        m_ref[...] = m_new

    chunk(0, n_ctx)
    n_lat = (k_ref.shape[0] - n_ctx) // tk

    @pl.when(j > 0)
    def _():
        def body(i, carry):
            chunk(pl.multiple_of(n_ctx + i * tk, tk), tk)
            return carry
        lax.fori_loop(0, n_lat, body, 0)

    lv = lamv_ref[...]
    lam = (jnp.exp(jnp.sum(lv[0:1] * lv[1:2], axis=-1, keepdims=True))
           - jnp.exp(jnp.sum(lv[2:3] * lv[3:4], axis=-1, keepdims=True)) + lam_init)
    o = acc_ref[...] / l_ref[...]
    o = o[:tq] - lam * o[tq:]
    o = o * lax.rsqrt(jnp.mean(o * o, axis=-1, keepdims=True) + EPS) * g_ref[...]
    o_ref[...] = (o * (1.0 - lam_init)).astype(o_ref.dtype)


def diff_attention(q, k, p, lam_vecs, subln_g, lam_init, n_batch, gpb, cb_v):
    nt, width = q.shape
    tb = gpb * ROW_G
    n_ctx = ROW_G
    tk = _pick(tb - n_ctx, (512, 256))
    kern = functools.partial(_attn_kernel, lam_init=lam_init, n_ctx=n_ctx, tk=tk)
    hb = DA_V_DIM // 128
    return pl.pallas_call(
        kern, grid=(n_batch, DA_HEADS, gpb),
        in_specs=[pl.BlockSpec(lam_vecs.shape, lambda b, h, j: (0, 0)),
                  pl.BlockSpec((1, DA_V_DIM), lambda b, h, j: (0, 0)),
                  pl.BlockSpec((ROW_G, DA_V_DIM), lambda b, h, j: (b * gpb + j, h)),
                  pl.BlockSpec((tb, DA_V_DIM), lambda b, h, j: (b, h)),
                  pl.BlockSpec((tb, DA_V_DIM), lambda b, h, j: (b, cb_v * (width // DA_V_DIM) + h * hb))],
        out_specs=pl.BlockSpec((ROW_G, DA_V_DIM), lambda b, h, j: (b * gpb + j, h)),
        out_shape=jax.ShapeDtypeStruct((nt, width), BF16),
        scratch_shapes=[pltpu.VMEM((2 * ROW_G, 1), F32), pltpu.VMEM((2 * ROW_G, 1), F32),
                        pltpu.VMEM((2 * ROW_G, DA_V_DIM), F32)],
        compiler_params=_cparams(3), name="diff_attn")(
            lam_vecs, subln_g.reshape(1, DA_V_DIM), q, k, p)


def _fill_padded(xpad_ref, main_ref, prev_ref, next_ref, has_prev, has_next):
    prev = prev_ref[...].astype(F32)[HALO - 8:, :]
    nxt = next_ref[...].astype(F32)[:8, :]
    xpad_ref[0:8, :] = jnp.where(has_prev, prev, 0.0)
    xpad_ref[8:8 + ROW_G, :] = main_ref[...].astype(F32)
    xpad_ref[8 + ROW_G:16 + ROW_G, :] = jnp.where(has_next, nxt, 0.0)


def _dwconv(xpad_ref, w_ref, b_ref, left):
    taps = w_ref.shape[0]
    acc = b_ref[...] + w_ref[0:1, :] * xpad_ref[pl.ds(8 - left, ROW_G), :]
    for t in range(1, taps):
        acc = acc + w_ref[t:t + 1, :] * xpad_ref[pl.ds(8 - left + t, ROW_G), :]
    return acc


def _halo_specs(width, cb, gpb, tile_of, n_groups):
    per = ROW_G // HALO

    def main(b, s, *_):
        return (b * gpb + tile_of(s), cb)

    def prev(b, s, *_):
        return (jnp.maximum((b * gpb + tile_of(s)) * per - 1, 0), cb)

    def nxt(b, s, *_):
        return (jnp.minimum((b * gpb + tile_of(s) + 1) * per, n_groups * per - 1), cb)

    return [pl.BlockSpec((ROW_G, width), main), pl.BlockSpec((HALO, width), prev),
            pl.BlockSpec((HALO, width), nxt)]


def _seq_flags(j, gpb):
    return j >= 2, jnp.logical_and(j >= 1, j <= gpb - 2)


def _scan_tile(a, b, reverse):
    n = a.shape[0]
    row = lax.broadcasted_iota(jnp.int32, a.shape, 0)
    s = 1
    while s < n:
        if reverse:
            keep = row < n - s
            a_sh = jnp.where(keep, pltpu.roll(a, n - s, 0), 1.0)
            b_sh = jnp.where(keep, pltpu.roll(b, n - s, 0), 0.0)
        else:
            keep = row >= s
            a_sh = jnp.where(keep, pltpu.roll(a, s, 0), 1.0)
            b_sh = jnp.where(keep, pltpu.roll(b, s, 0), 0.0)
        b = a * b_sh + b
        a = a * a_sh
        s *= 2
    return a, b


def _gelu_tanh(x):
    return 0.5 * x * (1.0 + jnp.tanh(math.sqrt(2.0 / math.pi) * (x + 0.044715 * (x * x * x))))


def _lru_kernel(main_ref, prev_ref, next_ref, cw_ref, cb_ref, w_ref, ba_ref, bi_ref, lam_ref, *rest,
                gpb, reverse):
    if reverse:
        hf_ref, ly_ref, o_ref, xpad_ref, carry_ref = rest
    else:
        o_ref, xpad_ref, carry_ref = rest
    s = pl.program_id(1)
    j = jnp.where(s == 0, 0, gpb - s) if reverse else s
    has_prev, has_next = _seq_flags(j, gpb)
    _fill_padded(xpad_ref, main_ref, prev_ref, next_ref, has_prev, has_next)
    xc = _dwconv(xpad_ref, cw_ref, cb_ref, 2)
    xcb = xc.astype(BF16)
    bw = w_ref.shape[1]
    r_parts, i_parts = [], []
    for n in range(w_ref.shape[0]):
        res = jnp.dot(xcb[:, n * bw:(n + 1) * bw], w_ref[n], preferred_element_type=F32)
        r_parts.append(res[:, :bw])
        i_parts.append(res[:, bw:])
    r = jax.nn.sigmoid(jnp.concatenate(r_parts, axis=1) + ba_ref[...])
    gate_i = jax.nn.sigmoid(jnp.concatenate(i_parts, axis=1) + bi_ref[...])
    nl = -lam_ref[...]
    softplus = jnp.maximum(nl, 0.0) + jnp.log(1.0 + jnp.exp(-jnp.abs(nl)))
    a = jnp.exp((-LRU_C) * r * softplus)
    bb = jnp.sqrt(1.0 - a * a) * (gate_i * xc)
    a_cum, h = _scan_tile(a, bb, reverse)

    @pl.when(s == 0)
    def _():
        carry_ref[...] = jnp.zeros(carry_ref.shape, F32)

    h = h + a_cum * carry_ref[...]
    carry_ref[...] = h[0:1, :] if reverse else h[ROW_G - 1:ROW_G, :]
    if reverse:
        ly = ly_ref[...].astype(F32)
        o_ref[...] = (_gelu_tanh(ly) * (hf_ref[...] + h)).astype(o_ref.dtype)
    else:
        o_ref[...] = h


def lru_pass(p, conv_w, conv_b, w_cat, ba, bi, lam, n_batch, gpb, cb_x, reverse, hf=None, cb_y=None):
    nt = p.shape[0]
    width = conv_w.shape[1]
    tile_of = (lambda s: jnp.where(s == 0, 0, gpb - s)) if reverse else (lambda s: s)
    kern = functools.partial(_lru_kernel, gpb=gpb, reverse=reverse)
    const2 = lambda b, s: (0, 0)
    in_specs = _halo_specs(width, cb_x, gpb, tile_of, nt // ROW_G) + [
        pl.BlockSpec(conv_w.shape, const2), pl.BlockSpec((1, width), const2),
        pl.BlockSpec(w_cat.shape, lambda b, s: (0, 0, 0)),
        pl.BlockSpec((1, width), const2), pl.BlockSpec((1, width), const2),
        pl.BlockSpec((1, width), const2)]
    args = [p, p, p, conv_w, conv_b.reshape(1, width), w_cat, ba.reshape(1, width),
            bi.reshape(1, width), lam.reshape(1, width)]
    row_spec = lambda cb: pl.BlockSpec((ROW_G, width), lambda b, s: (b * gpb + tile_of(s), cb))
    if reverse:
        in_specs += [row_spec(0), row_spec(cb_y)]
        args += [hf, p]
        out_dtype = BF16
    else:
        out_dtype = F32
    return pl.pallas_call(
        kern, grid=(n_batch, gpb), in_specs=in_specs, out_specs=row_spec(0),
        out_shape=jax.ShapeDtypeStruct((nt, width), out_dtype),
        scratch_shapes=[pltpu.VMEM((ROW_G + 16, width), F32), pltpu.VMEM((1, width), F32)],
        compiler_params=_cparams(2), name="lru_bwd" if reverse else "lru_fwd")(*args)


def _short_conv_kernel(main_ref, prev_ref, next_ref, cw_ref, cb_ref, o_ref, xpad_ref, *, gpb):
    j = pl.program_id(1)
    has_prev, has_next = _seq_flags(j, gpb)
    _fill_padded(xpad_ref, main_ref, prev_ref, next_ref, has_prev, has_next)
    o_ref[...] = _dwconv(xpad_ref, cw_ref, cb_ref, 1)


def hyena_short_conv(p, conv_w, conv_b, n_batch, gpb, cb0, width):
    nt = p.shape[0]
    total = conv_w.shape[1]
    ncb = total // width
    base = _halo_specs(width, 0, gpb, lambda s: s, nt // ROW_G)

    def shifted(spec):
        f = spec.index_map
        return pl.BlockSpec(spec.block_shape, lambda b, j, c: (f(b, j)[0], cb0 + c))

    kern = functools.partial(_short_conv_kernel, gpb=gpb)
    return pl.pallas_call(
        kern, grid=(n_batch, gpb, ncb),
        in_specs=[shifted(sp) for sp in base] + [
            pl.BlockSpec((conv_w.shape[0], width), lambda b, j, c: (0, c)),
            pl.BlockSpec((1, width), lambda b, j, c: (0, c))],
        out_specs=pl.BlockSpec((ROW_G, width), lambda b, j, c: (b * gpb + j, c)),
        out_shape=jax.ShapeDtypeStruct((nt, total), F32),
        scratch_shapes=[pltpu.VMEM((ROW_G + 16, width), F32)],
        compiler_params=_cparams(3), name="hyena_short_conv")(p, p, p, conv_w, conv_b.reshape(1, total))


def _merge_kernel(a_ref, r_ref, y_ref, g0_ref, g1_ref, g2_ref, bg_ref, wbr_ref, wout_ref, tab_ref, x_ref,
                  o_ref, *, k_gate, gpb, n_batch):
    width = a_ref.shape[1]
    m = None
    for k, (br_ref, gt_ref) in enumerate(((a_ref, g0_ref), (r_ref, g1_ref), (y_ref, g2_ref))):
        gate = jax.nn.sigmoid(gt_ref[...].astype(F32) + bg_ref[:, k * width:(k + 1) * width])
        term = gate * jnp.dot(br_ref[...], wbr_ref[k], preferred_element_type=F32)
        m = term if m is None else m + term
    out = jnp.dot(m.astype(BF16), wout_ref[...], preferred_element_type=F32)
    nsub = x_ref.shape[0] // ROW_G
    for s in range(nsub):
        rows = pl.ds(s * ROW_G, ROW_G)
        gmod = _mod_row(tab_ref, k_gate, pl.program_id(0) * nsub + s, gpb, n_batch)
        o_ref[rows, :] = x_ref[rows, :] + gmod * out[s * ROW_G:(s + 1) * ROW_G, :]


def merge_branches(x, a, r, y, p, b_gate, w_br, w_out, tab, k_gate, gpb, n_batch, cb_g):
    nt, d = x.shape
    tm = _pick(nt, (512, 256))
    kern = functools.partial(_merge_kernel, k_gate=k_gate, gpb=gpb, n_batch=n_batch)
    row = pl.BlockSpec((tm, d), lambda i: (i, 0))
    gspec = lambda k: pl.BlockSpec((tm, d), lambda i: (i, cb_g + k))
    return pl.pallas_call(
        kern, grid=(nt // tm,),
        in_specs=[row, row, row, gspec(0), gspec(1), gspec(2),
                  pl.BlockSpec((1, 3 * d), lambda i: (0, 0)),
                  pl.BlockSpec(w_br.shape, lambda i: (0, 0, 0)),
                  pl.BlockSpec(w_out.shape, lambda i: (0, 0)),
                  pl.BlockSpec(tab.shape, lambda i: (0, 0, 0)), row],
        out_specs=row, out_shape=jax.ShapeDtypeStruct((nt, d), F32),
        compiler_params=_cparams(1), name="merge")(a, r, y, p, p, p, b_gate.reshape(1, 3 * d), w_br, w_out, tab, x)


def _gate_up_kernel(*refs, expert):
    if expert:
        _, f_ref, wg_ref, wu_ref, o_ref = refs
    else:
        f_ref, wg_ref, wu_ref, o_ref = refs
    f = f_ref[...]
    g = jnp.dot(f, wg_ref[...], preferred_element_type=F32)
    u = jnp.dot(f, wu_ref[...], preferred_element_type=F32)
    o_ref[...] = (g * jax.nn.sigmoid(g) * u).astype(o_ref.dtype)


def gate_up(f, wg, wu, block_e=None):
    m, d = f.shape
    dff = wg.shape[-1]
    tn = _pick(dff, (1408, 1024, 512, 256, 128))
    if block_e is None:
        tm = _pick(m, (512, 256))
        grid_spec = pltpu.PrefetchScalarGridSpec(
            num_scalar_prefetch=0, grid=(dff // tn, m // tm),
            in_specs=[pl.BlockSpec((tm, d), lambda j, i: (i, 0)),
                      pl.BlockSpec((d, tn), lambda j, i: (0, j)),
                      pl.BlockSpec((d, tn), lambda j, i: (0, j))],
            out_specs=pl.BlockSpec((tm, tn), lambda j, i: (i, j)))
        args = (f, wg, wu)
    else:
        tm = MOE_BLOCK
        grid_spec = pltpu.PrefetchScalarGridSpec(
            num_scalar_prefetch=1, grid=(dff // tn, m // tm),
            in_specs=[pl.BlockSpec((tm, d), lambda j, i, be: (i, 0)),
                      pl.BlockSpec((None, d, tn), lambda j, i, be: (be[i], 0, j)),
                      pl.BlockSpec((None, d, tn), lambda j, i, be: (be[i], 0, j))],
            out_specs=pl.BlockSpec((tm, tn), lambda j, i, be: (i, j)))
        args = (block_e, f, wg, wu)
    return pl.pallas_call(
        functools.partial(_gate_up_kernel, expert=block_e is not None), grid_spec=grid_spec,
        out_shape=jax.ShapeDtypeStruct((m, dff), BF16),
        compiler_params=_cparams(2), name="gate_up")(*args)


def _down_res_kernel(h_ref, wd_ref, tab_ref, x_ref, o_ref, *, k_gate, gpb, n_batch):
    out = jnp.dot(h_ref[...], wd_ref[...], preferred_element_type=F32)
    nsub = x_ref.shape[0] // ROW_G
    for s in range(nsub):
        rows = pl.ds(s * ROW_G, ROW_G)
        gmod = _mod_row(tab_ref, k_gate, pl.program_id(0) * nsub + s, gpb, n_batch)
        o_ref[rows, :] = x_ref[rows, :] + gmod * out[s * ROW_G:(s + 1) * ROW_G, :]


def down_residual(h, wd, x, tab, k_gate, gpb, n_batch):
    nt, d = x.shape
    dff = h.shape[1]
    tm = _pick(nt, (512, 256))
    kern = functools.partial(_down_res_kernel, k_gate=k_gate, gpb=gpb, n_batch=n_batch)
    return pl.pallas_call(
        kern, grid=(nt // tm,),
        in_specs=[pl.BlockSpec((tm, dff), lambda i: (i, 0)),
                  pl.BlockSpec((dff, d), lambda i: (0, 0)),
                  pl.BlockSpec(tab.shape, lambda i: (0, 0, 0)),
                  pl.BlockSpec((tm, d), lambda i: (i, 0))],
        out_specs=pl.BlockSpec((tm, d), lambda i: (i, 0)),
        out_shape=jax.ShapeDtypeStruct((nt, d), F32),
        compiler_params=_cparams(1), name="down_residual")(h, wd, tab, x)


def _expert_down_kernel(be_ref, h_ref, wd_ref, o_ref):
    o_ref[...] = jnp.dot(h_ref[...], wd_ref[...], preferred_element_type=F32)


def expert_down(h, wd, block_e):
    m, dff = h.shape
    d = wd.shape[-1]
    grid_spec = pltpu.PrefetchScalarGridSpec(
        num_scalar_prefetch=1, grid=(m // MOE_BLOCK,),
        in_specs=[pl.BlockSpec((MOE_BLOCK, dff), lambda i, be: (i, 0)),
                  pl.BlockSpec((None, dff, d), lambda i, be: (be[i], 0, 0))],
        out_specs=pl.BlockSpec((MOE_BLOCK, d), lambda i, be: (i, 0)))
    return pl.pallas_call(_expert_down_kernel, grid_spec=grid_spec,
                          out_shape=jax.ShapeDtypeStruct((m, d), F32),
                          compiler_params=_cparams(1), name="expert_down")(block_e, h, wd)


def _combine_kernel(y0_ref, y1_ref, w0_ref, w1_ref, tab_ref, x_ref, o_ref, *, k_gate, gpb, n_batch):
    nsub = x_ref.shape[0] // ROW_G
    for s in range(nsub):
        rows = pl.ds(s * ROW_G, ROW_G)
        gmod = _mod_row(tab_ref, k_gate, pl.program_id(0) * nsub + s, gpb, n_batch)
        y = y0_ref[rows, :] * w0_ref[rows, :] + y1_ref[rows, :] * w1_ref[rows, :]
        o_ref[rows, :] = x_ref[rows, :] + gmod * y


def moe_combine(y0, y1, w0, w1, x, tab, k_gate, gpb, n_batch):
    nt, d = x.shape
    tm = _pick(nt, (512, 256))
    kern = functools.partial(_combine_kernel, k_gate=k_gate, gpb=gpb, n_batch=n_batch)
    row = pl.BlockSpec((tm, d), lambda i: (i, 0))
    wsp = pl.BlockSpec((tm, 1), lambda i: (i, 0))
    return pl.pallas_call(
        kern, grid=(nt // tm,),
        in_specs=[row, row, wsp, wsp, pl.BlockSpec(tab.shape, lambda i: (0, 0, 0)), row],
        out_specs=row, out_shape=jax.ShapeDtypeStruct((nt, d), F32),
        compiler_params=_cparams(1), name="moe_combine")(y0, y1, w0, w1, tab, x)


def moe_layer(f, logits, w_gate, w_up, w_down, x, tab, k_gate, gpb, n_batch):
    nt = f.shape[0]
    n_asg = nt * TOP_K
    top_v, top_i = lax.top_k(logits[:, :N_EXPERTS], TOP_K)
    top_w = jax.nn.softmax(top_v, axis=-1)
    e_flat = top_i.reshape(-1)
    onehot = (e_flat[:, None] == jnp.arange(N_EXPERTS, dtype=e_flat.dtype)[None, :]).astype(jnp.int32)
    csum = jnp.cumsum(onehot, axis=0)
    counts = csum[-1]
    rank = jnp.take_along_axis(csum, e_flat[:, None], axis=1)[:, 0] - 1
    padded = ((counts + MOE_BLOCK - 1) // MOE_BLOCK) * MOE_BLOCK
    pad_end = jnp.cumsum(padded)
    pad_start = pad_end - padded
    dest = (pad_start[e_flat] + rank).astype(jnp.int32)
    n_blocks = -(-n_asg // MOE_BLOCK) + N_EXPERTS
    n_slots = n_blocks * MOE_BLOCK
    tok_flat = jnp.repeat(jnp.arange(nt, dtype=jnp.int32), TOP_K)
    slot_tok = jnp.zeros((n_slots,), jnp.int32).at[dest].set(tok_flat)
    block_e = jnp.minimum(jnp.searchsorted(pad_end, jnp.arange(n_blocks, dtype=jnp.int32) * MOE_BLOCK,
                                           side='right'), N_EXPERTS - 1).astype(jnp.int32)
    xs = jnp.take(f, slot_tok, axis=0)
    h = gate_up(xs, w_gate, w_up, block_e)
    ys = expert_down(h, w_down, block_e)
    dest2 = dest.reshape(nt, TOP_K)
    y0 = jnp.take(ys, dest2[:, 0], axis=0)
    y1 = jnp.take(ys, dest2[:, 1], axis=0)
    return moe_combine(y0, y1, top_w[:, 0:1], top_w[:, 1:2], x, tab, k_gate, gpb, n_batch)


def hyena_filters(length, w1, b1, w2, b2, freq, w3, decay):
    t = jnp.linspace(0.0, 1.0, length, dtype=F32)[:, None]
    w = 2.0 * math.pi * jnp.arange(length, dtype=F32)[:, None] / length
    bands = jnp.linspace(1e-4, HY_BANDS - 1, HY_BANDS, dtype=F32)
    z = jnp.concatenate([t, jnp.cos(bands * w), -jnp.sin(bands * w)], axis=-1)
    hdn = jnp.sin(freq * (jnp.dot(z, w1, precision=lax.Precision.HIGHEST) + b1))
    hdn = jnp.sin(freq * (jnp.dot(hdn, w2, precision=lax.Precision.HIGHEST) + b2))
    width = decay.shape[-1]
    filt = jnp.dot(hdn, w3, precision=lax.Precision.HIGHEST).reshape(length, HY_ORDER, 2, width)
    window = jnp.exp(-t[:, :, None] * jnp.abs(decay))
    filt = filt * window[:, :, None, :]
    return filt / (jnp.sum(jnp.abs(filt), axis=(0, 2), keepdims=True) + EPS)


def bidir_long_conv(u, h_fwd, h_bwd, skip):
    length = u.shape[1]
    hc = jnp.concatenate([h_fwd[:1] + h_bwd[:1], h_fwd[1:], jnp.zeros_like(h_fwd[:1]), h_bwd[:0:-1]], axis=0)
    y = jnp.fft.irfft(jnp.fft.rfft(u, n=2 * length, axis=1) * jnp.fft.rfft(hc, axis=0)[None],
                      n=2 * length, axis=1)[:, :length]
    return y + skip * u


def hyena_long(u, fargs, skip):
    v, x1, x2 = jnp.split(u, 3, axis=-1)
    filt = hyena_filters(u.shape[1], *fargs)
    z = x1 * bidir_long_conv(v, filt[:, 0, 0], filt[:, 0, 1], skip[0])
    return x2 * bidir_long_conv(z, filt[:, 1, 0], filt[:, 1, 1], skip[1])


def kernel(x, c, ctx, c_ctx, w_mod, b_mod, g_mix, g_ffn, w_in, b_gate, w_br, w_out, da_lambda, da_subln_g,
           lru_conv_w, lru_conv_b, lru_wa, lru_ba, lru_wi, lru_bi, lru_lambda, hy_conv_w, hy_conv_b,
           hy_f_w1, hy_f_b1, hy_f_w2, hy_f_b2, hy_f_freq, hy_f_w3, hy_decay, hy_skip, ffn_w_gate, ffn_w_up,
           ffn_w_down, moe_router, moe_w_gate, moe_w_up, moe_w_down, g_final):
    n_batch, length, d = x.shape
    n_ctx = ctx.shape[1]
    depth = w_mod.shape[0]
    assert n_ctx == ROW_G and length % ROW_G == 0 and d % 128 == 0
    tb = n_ctx + length
    gpb = tb // ROW_G
    nt = n_batch * tb
    c_end = w_in.shape[2]
    cb_k, cb_v, cb_lx, cb_q, cb_ly, cb_hy, cb_g = 0, 1, 2, 3, 4, 5, 8
    assert c_end == 11 * d

    xs = jnp.concatenate([ctx, x], axis=1).reshape(nt, d)
    silu_rows = jnp.concatenate([jax.nn.silu(c), jax.nn.silu(c_ctx)[None, :],
                                 jnp.zeros((16 - n_batch - 1, d), F32)], axis=0)
    tables = rope_tables(length, n_ctx, d)

    for li in range(depth):
        lam_init = 0.8 - 0.6 * math.exp(-0.3 * li)
        tab = mm_f32_bias(silu_rows, w_mod[li], b_mod[li], d)
        tab = tab.reshape(16, 6, d).transpose(1, 0, 2)

        h = norm_mod(xs, g_mix[li], tab, 0, 1, gpb, n_batch)
        tm = _pick(nt, (1024, 512, 256))
        p = mm_bf16(h, w_in[li].astype(BF16), BF16, tm, d, "in_proj")

        q, k = rope_qk(p, tables, n_batch, gpb, cb_q, cb_k, d)
        a_out = diff_attention(q, k, p, da_lambda[li], da_subln_g[li], lam_init, n_batch, gpb, cb_v)

        bw = d // LRU_BLOCKS
        r_out = None
        hf = None
        for direction in range(2):
            w_cat = jnp.concatenate([lru_wa[li, direction], lru_wi[li, direction]], axis=-1).astype(BF16)
            res = lru_pass(p, lru_conv_w[li], lru_conv_b[li], w_cat, lru_ba[li, direction],
                           lru_bi[li, direction], lru_lambda[li, direction], n_batch, gpb, cb_lx,
                           reverse=direction == 1, hf=hf, cb_y=cb_ly)
            if direction == 0:
                hf = res
            else:
                r_out = res
        del bw

        u = hyena_short_conv(p, hy_conv_w[li], hy_conv_b[li], n_batch, gpb, cb_hy, d).reshape(n_batch, tb, 3 * d)
        fargs = (hy_f_w1[li], hy_f_b1[li], hy_f_w2[li], hy_f_b2[li], hy_f_freq[li], hy_f_w3[li], hy_decay[li])
        y_c = hyena_long(u[:, :n_ctx], fargs, hy_skip[li])
        y_l = hyena_long(u[:, n_ctx:], fargs, hy_skip[li])
        y_out = jnp.concatenate([y_c, y_l], axis=1).reshape(nt, d).astype(BF16)

        xs = merge_branches(xs, a_out, r_out, y_out, p, b_gate[li], w_br[li].astype(BF16),
                            w_out[li].astype(BF16), tab, 2, gpb, n_batch, cb_g)

        jj = li // 2
        if li % 2 == 0:
            f = norm_mod(xs, g_ffn[li], tab, 3, 4, gpb, n_batch)
            hh = gate_up(f, ffn_w_gate[jj].astype(BF16), ffn_w_up[jj].astype(BF16))
            xs = down_residual(hh, ffn_w_down[jj].astype(BF16), xs, tab, 5, gpb, n_batch)
        else:
            wr = jnp.concatenate([moe_router[jj], jnp.zeros((d, 128 - N_EXPERTS), F32)], axis=1)
            f, logits = norm_mod(xs, g_ffn[li], tab, 3, 4, gpb, n_batch, w_router=wr)
            xs = moe_layer(f, logits, moe_w_gate[jj].astype(BF16), moe_w_up[jj].astype(BF16),
                           moe_w_down[jj].astype(BF16), xs, tab, 5, gpb, n_batch)

    return final_norm(xs.reshape(n_batch, tb, d), g_final, n_ctx)
```

```python
import functools
import math

import jax
import jax.numpy as jnp
from jax import lax
from jax.experimental import pallas as pl
from jax.experimental.pallas import tpu as pltpu

F32 = jnp.float32
BF16 = jnp.bfloat16

EPS = 1e-6
ROW_G = 256
GRID_W = 64
ROPE_BASE = 10000.0
DA_HEADS = 8
DA_HEAD_DIM = 64
DA_V_DIM = 128
LRU_BLOCKS = 8
LRU_C = 8.0
HY_ORDER = 2
HY_BANDS = 16
N_EXPERTS = 8
TOP_K = 2
MOE_BLOCK = 256
HALO = 16
NEG_BIG = -1e30
VMEM_LIMIT = 56 * 1024 * 1024


def _cparams(n_axes):
    return pltpu.CompilerParams(dimension_semantics=("arbitrary",) * n_axes,
                                vmem_limit_bytes=VMEM_LIMIT)


def _pick(n, prefs):
    for p in prefs:
        if n % p == 0:
            return p
    raise ValueError(f"no tile in {prefs} divides {n}")


def _mod_row(tab_ref, k, group, groups_per_batch, n_batch):
    b = group // groups_per_batch
    row = jnp.where(group % groups_per_batch == 0, n_batch, b)
    return tab_ref[k, pl.ds(row, 1), :]


def _mm_f32_kernel(a_ref, w_ref, b_ref, o_ref):
    o_ref[...] = jnp.dot(a_ref[...], w_ref[...], preferred_element_type=F32) + b_ref[...]


def mm_f32_bias(a, w, b, tn):
    m, k = a.shape
    n = w.shape[1]
    return pl.pallas_call(
        _mm_f32_kernel, grid=(n // tn,),
        in_specs=[pl.BlockSpec((m, k), lambda j: (0, 0)),
                  pl.BlockSpec((k, tn), lambda j: (0, j)),
                  pl.BlockSpec((1, tn), lambda j: (0, j))],
        out_specs=pl.BlockSpec((m, tn), lambda j: (0, j)),
        out_shape=jax.ShapeDtypeStruct((m, n), F32),
        compiler_params=_cparams(1), name="mod_matmul")(a, w, b.reshape(1, n))


def _norm_mod_kernel(x_ref, g_ref, tab_ref, *rest, k_shift, k_scale, gpb, n_batch, router):
    if router:
        wr_ref, o_ref, lg_ref = rest
    else:
        (o_ref,) = rest
    nsub = x_ref.shape[0] // ROW_G
    for s in range(nsub):
        grp = pl.program_id(0) * nsub + s
        rows = pl.ds(s * ROW_G, ROW_G)
        xs = x_ref[rows, :]
        y = xs * lax.rsqrt(jnp.mean(xs * xs, axis=-1, keepdims=True) + EPS) * g_ref[...]
        shift = _mod_row(tab_ref, k_shift, grp, gpb, n_batch)
        scale = _mod_row(tab_ref, k_scale, grp, gpb, n_batch)
        h = y * (1.0 + scale) + shift
        o_ref[rows, :] = h.astype(o_ref.dtype)
        if router:
            lg_ref[rows, :] = jnp.dot(h, wr_ref[...], preferred_element_type=F32,
                                      precision=lax.Precision.HIGHEST)


def norm_mod(x, g, tab, k_shift, k_scale, gpb, n_batch, w_router=None):
    nt, d = x.shape
    tm = _pick(nt, (1024, 512, 256))
    router = w_router is not None
    kern = functools.partial(_norm_mod_kernel, k_shift=k_shift, k_scale=k_scale, gpb=gpb,
                             n_batch=n_batch, router=router)
    in_specs = [pl.BlockSpec((tm, d), lambda i: (i, 0)),
                pl.BlockSpec((1, d), lambda i: (0, 0)),
                pl.BlockSpec(tab.shape, lambda i: (0, 0, 0))]
    out_specs = [pl.BlockSpec((tm, d), lambda i: (i, 0))]
    out_shape = [jax.ShapeDtypeStruct((nt, d), BF16)]
    args = [x, g.reshape(1, d), tab]
    if router:
        in_specs.append(pl.BlockSpec(w_router.shape, lambda i: (0, 0)))
        out_specs.append(pl.BlockSpec((tm, w_router.shape[1]), lambda i: (i, 0)))
        out_shape.append(jax.ShapeDtypeStruct((nt, w_router.shape[1]), F32))
        args.append(w_router)
    out = pl.pallas_call(kern, grid=(nt // tm,), in_specs=in_specs, out_specs=out_specs,
                         out_shape=out_shape, compiler_params=_cparams(1), name="norm_mod")(*args)
    return out if router else out[0]


def _final_norm_kernel(x_ref, g_ref, o_ref):
    xs = x_ref[...]
    o_ref[...] = xs * lax.rsqrt(jnp.mean(xs * xs, axis=-1, keepdims=True) + EPS) * g_ref[...]


def final_norm(x3, g, n_ctx):
    b, tb, d = x3.shape
    length = tb - n_ctx
    tm = ROW_G
    off = n_ctx // tm
    return pl.pallas_call(
        _final_norm_kernel, grid=(b, length // tm),
        in_specs=[pl.BlockSpec((None, tm, d), lambda i, j: (i, j + off, 0)),
                  pl.BlockSpec((1, d), lambda i, j: (0, 0))],
        out_specs=pl.BlockSpec((None, tm, d), lambda i, j: (i, j, 0)),
        out_shape=jax.ShapeDtypeStruct((b, length, d), F32),
        compiler_params=_cparams(2), name="final_norm")(x3, g.reshape(1, d))


def _mm_kernel(a_ref, w_ref, o_ref):
    o_ref[...] = jnp.dot(a_ref[...], w_ref[...], preferred_element_type=F32).astype(o_ref.dtype)


def mm_bf16(a, w, out_dtype, tm, tn, name):
    m, k = a.shape
    n = w.shape[1]
    return pl.pallas_call(
        _mm_kernel, grid=(n // tn, m // tm),
        in_specs=[pl.BlockSpec((tm, k), lambda j, i: (i, 0)),
                  pl.BlockSpec((k, tn), lambda j, i: (0, j))],
        out_specs=pl.BlockSpec((tm, tn), lambda j, i: (i, j)),
        out_shape=jax.ShapeDtypeStruct((m, n), out_dtype),
        compiler_params=_cparams(2), name=name)(a, w)


def _rope_kernel(pq_ref, pk_ref, cos_ref, sa_ref, sb_ref, q_ref, k_ref, *, q_scale):
    width = pq_ref.shape[1]
    cos, sa, sb = cos_ref[...], sa_ref[...], sb_ref[...]

    def rot(t):
        return t * cos + pltpu.roll(t, width - 16, 1) * sa + pltpu.roll(t, 16, 1) * sb

    q_ref[...] = (rot(pq_ref[...].astype(F32)) * q_scale).astype(q_ref.dtype)
    k_ref[...] = rot(pk_ref[...].astype(F32)).astype(k_ref.dtype)


def rope_tables(length, n_ctx, width):
    rows = length // GRID_W
    row = jnp.repeat(jnp.arange(rows, dtype=F32), GRID_W)
    col = jnp.tile(jnp.arange(GRID_W, dtype=F32), rows)
    n_freq = DA_HEAD_DIM // 4
    inv = ROPE_BASE ** (-jnp.arange(n_freq, dtype=F32) / n_freq)
    ar = row[:, None] * inv
    ac = col[:, None] * inv
    ang = jnp.concatenate([ar, ar, ac, ac], axis=-1)
    ang = jnp.concatenate([jnp.zeros((n_ctx, DA_HEAD_DIM), F32), ang], axis=0)
    reps = width // DA_HEAD_DIM
    cos = jnp.tile(jnp.cos(ang), (1, reps))
    sin = jnp.tile(jnp.sin(ang), (1, reps))
    first = (jnp.arange(width) % (2 * n_freq)) < n_freq
    sa = jnp.where(first[None, :], -sin, 0.0)
    sb = jnp.where(first[None, :], 0.0, sin)
    return cos, sa, sb


def rope_qk(p, tables, n_batch, gpb, cb_q, cb_k, width):
    nt = p.shape[0]
    cos, sa, sb = tables
    kern = functools.partial(_rope_kernel, q_scale=math.log2(math.e) * DA_HEAD_DIM ** -0.5)
    tspec = pl.BlockSpec((ROW_G, width), lambda b, j: (j, 0))
    ospec = pl.BlockSpec((ROW_G, width), lambda b, j: (b * gpb + j, 0))
    return pl.pallas_call(
        kern, grid=(n_batch, gpb),
        in_specs=[pl.BlockSpec((ROW_G, width), lambda b, j: (b * gpb + j, cb_q)),
                  pl.BlockSpec((ROW_G, width), lambda b, j: (b * gpb + j, cb_k)),
                  tspec, tspec, tspec],
        out_specs=[ospec, ospec],
        out_shape=[jax.ShapeDtypeStruct((nt, width), BF16)] * 2,
        compiler_params=_cparams(2), name="rope")(p, p, cos, sa, sb)


def _attn_kernel(lamv_ref, g_ref, q_ref, k_ref, v_ref, o_ref, vaug_ref, m_ref, acc_ref,
                 *, lam_init, n_ctx, tk):
    j = pl.program_id(2)
    dv = v_ref.shape[1]

    @pl.when(j == 0)
    def _():
        vaug_ref[:, :dv] = v_ref[...]
        vaug_ref[:, dv:] = jnp.ones((vaug_ref.shape[0], vaug_ref.shape[1] - dv), vaug_ref.dtype)

    q = q_ref[...]
    lane = lax.broadcasted_iota(jnp.int32, q.shape, 1)
    zero = jnp.zeros_like(q)
    qq = jnp.concatenate([jnp.where(lane < DA_HEAD_DIM, q, zero),
                          jnp.where(lane >= DA_HEAD_DIM, q, zero)], axis=0)
    tq = q.shape[0]
    m_ref[...] = jnp.full(m_ref.shape, NEG_BIG, F32)
    acc_ref[...] = jnp.zeros(acc_ref.shape, F32)

    def chunk(start, size):
        kc = k_ref[pl.ds(start, size), :]
        s = lax.dot_general(qq, kc, (((1,), (1,)), ((), ())), preferred_element_type=F32)
        m_prev = m_ref[...]
        m_new = jnp.maximum(m_prev, jnp.max(s, axis=-1, keepdims=True))
        alpha = jnp.exp2(m_prev - m_new)
        p = jnp.exp2(s - jnp.concatenate([m_new] * (size // 128), axis=1))
        pv = jnp.dot(p.astype(BF16), vaug_ref[pl.ds(start, size), :], preferred_element_type=F32)
        acc_ref[...] = jnp.concatenate([alpha] * (acc_ref.shape[1] // 128), axis=1) * acc_ref[...] + pv
        m_ref[...] = m_new

    chunk(0, n_ctx)

    @pl.when(j > 0)
    def _():
        for i in range((k_ref.shape[0] - n_ctx) // tk):
            chunk(n_ctx + i * tk, tk)

    lv = lamv_ref[...]
    lam = (jnp.exp(jnp.sum(lv[0:1] * lv[1:2], axis=-1, keepdims=True))
           - jnp.exp(jnp.sum(lv[2:3] * lv[3:4], axis=-1, keepdims=True)) + lam_init)
    o = acc_ref[:, :dv] / acc_ref[:, dv:]
    o = o[:tq] - lam * o[tq:]
    o = o * lax.rsqrt(jnp.mean(o * o, axis=-1, keepdims=True) + EPS) * g_ref[...]
    o_ref[...] = (o * (1.0 - lam_init)).astype(o_ref.dtype)


def diff_attention(q, k, p, lam_vecs, subln_g, lam_init, n_batch, gpb, cb_v):
    nt, width = q.shape
    tb = gpb * ROW_G
    n_ctx = ROW_G
    tk = _pick(tb - n_ctx, (1024, 512, 256))
    kern = functools.partial(_attn_kernel, lam_init=lam_init, n_ctx=n_ctx, tk=tk)
    assert DA_V_DIM == 128
    hb = 1
    return pl.pallas_call(
        kern, grid=(n_batch, DA_HEADS, gpb),
        in_specs=[pl.BlockSpec(lam_vecs.shape, lambda b, h, j: (0, 0)),
                  pl.BlockSpec((1, DA_V_DIM), lambda b, h, j: (0, 0)),
                  pl.BlockSpec((ROW_G, DA_V_DIM), lambda b, h, j: (b * gpb + j, h)),
                  pl.BlockSpec((tb, DA_V_DIM), lambda b, h, j: (b, h)),
                  pl.BlockSpec((tb, DA_V_DIM), lambda b, h, j: (b, cb_v * (width // DA_V_DIM) + h * hb))],
        out_specs=pl.BlockSpec((ROW_G, DA_V_DIM), lambda b, h, j: (b * gpb + j, h)),
        out_shape=jax.ShapeDtypeStruct((nt, width), BF16),
        scratch_shapes=[pltpu.VMEM((tb, 2 * DA_V_DIM), BF16), pltpu.VMEM((2 * ROW_G, 128), F32),
                        pltpu.VMEM((2 * ROW_G, 2 * DA_V_DIM), F32)],
        compiler_params=_cparams(3), name="diff_attn")(
            lam_vecs, subln_g.reshape(1, DA_V_DIM), q, k, p)


def _fill_padded(xpad_ref, main_ref, prev_ref, next_ref, has_prev, has_next):
    prev = prev_ref[...].astype(F32)[HALO - 8:, :]
    nxt = next_ref[...].astype(F32)[:8, :]
    xpad_ref[0:8, :] = jnp.where(has_prev, prev, 0.0)
    xpad_ref[8:8 + ROW_G, :] = main_ref[...].astype(F32)
    xpad_ref[8 + ROW_G:16 + ROW_G, :] = jnp.where(has_next, nxt, 0.0)


def _dwconv(xpad_ref, w_ref, b_ref, left):
    taps = w_ref.shape[0]
    acc = b_ref[...] + w_ref[0:1, :] * xpad_ref[pl.ds(8 - left, ROW_G), :]
    for t in range(1, taps):
        acc = acc + w_ref[t:t + 1, :] * xpad_ref[pl.ds(8 - left + t, ROW_G), :]
    return acc


def _halo_specs(width, cb, gpb, tile_of, n_groups):
    per = ROW_G // HALO

    def main(b, s, *_):
        return (b * gpb + tile_of(s), cb)

    def prev(b, s, *_):
        return (jnp.maximum((b * gpb + tile_of(s)) * per - 1, 0), cb)

    def nxt(b, s, *_):
        return (jnp.minimum((b * gpb + tile_of(s) + 1) * per, n_groups * per - 1), cb)

    return [pl.BlockSpec((ROW_G, width), main), pl.BlockSpec((HALO, width), prev),
            pl.BlockSpec((HALO, width), nxt)]


def _seq_flags(j, gpb):
    return j >= 2, jnp.logical_and(j >= 1, j <= gpb - 2)


def _scan_tile(a, b, reverse):
    n = a.shape[0]
    row = lax.broadcasted_iota(jnp.int32, a.shape, 0)
    s = 1
    while s < n:
        if reverse:
            keep = row < n - s
            a_sh = jnp.where(keep, pltpu.roll(a, n - s, 0), 1.0)
            b_sh = jnp.where(keep, pltpu.roll(b, n - s, 0), 0.0)
        else:
            keep = row >= s
            a_sh = jnp.where(keep, pltpu.roll(a, s, 0), 1.0)
            b_sh = jnp.where(keep, pltpu.roll(b, s, 0), 0.0)
        b = a * b_sh + b
        a = a * a_sh
        s *= 2
    return a, b


def _gelu_tanh(x):
    return 0.5 * x * (1.0 + jnp.tanh(math.sqrt(2.0 / math.pi) * (x + 0.044715 * (x * x * x))))


def _lru_kernel(main_ref, prev_ref, next_ref, cw_ref, cb_ref, w_ref, ba_ref, bi_ref, lam_ref, *rest,
                gpb, reverse):
    if reverse:
        hf_ref, ly_ref, o_ref, xpad_ref, carry_ref = rest
    else:
        o_ref, xpad_ref, carry_ref = rest
    s = pl.program_id(1)
    j = jnp.where(s == 0, 0, gpb - s) if reverse else s
    has_prev, has_next = _seq_flags(j, gpb)
    _fill_padded(xpad_ref, main_ref, prev_ref, next_ref, has_prev, has_next)
    xc = _dwconv(xpad_ref, cw_ref, cb_ref, 2)
    xcb = xc.astype(BF16)
    bw = w_ref.shape[1]
    r_parts, i_parts = [], []
    for n in range(w_ref.shape[0]):
        res = jnp.dot(xcb[:, n * bw:(n + 1) * bw], w_ref[n], preferred_element_type=F32)
        r_parts.append(res[:, :bw])
        i_parts.append(res[:, bw:])
    r = jax.nn.sigmoid(jnp.concatenate(r_parts, axis=1) + ba_ref[...])
    gate_i = jax.nn.sigmoid(jnp.concatenate(i_parts, axis=1) + bi_ref[...])
    nl = -lam_ref[...]
    softplus = jnp.maximum(nl, 0.0) + jnp.log(1.0 + jnp.exp(-jnp.abs(nl)))
    a = jnp.exp((-LRU_C) * r * softplus)
    bb = jnp.sqrt(1.0 - a * a) * (gate_i * xc)
    a_cum, h = _scan_tile(a, bb, reverse)

    @pl.when(s == 0)
    def _():
        carry_ref[...] = jnp.zeros(carry_ref.shape, F32)

    h = h + a_cum * carry_ref[...]
    carry_ref[...] = h[0:1, :] if reverse else h[ROW_G - 1:ROW_G, :]
    if reverse:
        ly = ly_ref[...].astype(F32)
        o_ref[...] = (_gelu_tanh(ly) * (hf_ref[...] + h)).astype(o_ref.dtype)
    else:
        o_ref[...] = h


def lru_pass(p, conv_w, conv_b, w_cat, ba, bi, lam, n_batch, gpb, cb_x, reverse, hf=None, cb_y=None):
    nt = p.shape[0]
    width = conv_w.shape[1]
    tile_of = (lambda s: jnp.where(s == 0, 0, gpb - s)) if reverse else (lambda s: s)
    kern = functools.partial(_lru_kernel, gpb=gpb, reverse=reverse)
    const2 = lambda b, s: (0, 0)
    in_specs = _halo_specs(width, cb_x, gpb, tile_of, nt // ROW_G) + [
        pl.BlockSpec(conv_w.shape, const2), pl.BlockSpec((1, width), const2),
        pl.BlockSpec(w_cat.shape, lambda b, s: (0, 0, 0)),
        pl.BlockSpec((1, width), const2), pl.BlockSpec((1, width), const2),
        pl.BlockSpec((1, width), const2)]
    args = [p, p, p, conv_w, conv_b.reshape(1, width), w_cat, ba.reshape(1, width),
            bi.reshape(1, width), lam.reshape(1, width)]
    row_spec = lambda cb: pl.BlockSpec((ROW_G, width), lambda b, s: (b * gpb + tile_of(s), cb))
    if reverse:
        in_specs += [row_spec(0), row_spec(cb_y)]
        args += [hf, p]
        out_dtype = BF16
    else:
        out_dtype = F32
    return pl.pallas_call(
        kern, grid=(n_batch, gpb), in_specs=in_specs, out_specs=row_spec(0),
        out_shape=jax.ShapeDtypeStruct((nt, width), out_dtype),
        scratch_shapes=[pltpu.VMEM((ROW_G + 16, width), F32), pltpu.VMEM((1, width), F32)],
        compiler_params=_cparams(2), name="lru_bwd" if reverse else "lru_fwd")(*args)


def _short_conv_kernel(main_ref, prev_ref, next_ref, cw_ref, cb_ref, o_ref, xpad_ref, *, gpb):
    j = pl.program_id(1)
    has_prev, has_next = _seq_flags(j, gpb)
    _fill_padded(xpad_ref, main_ref, prev_ref, next_ref, has_prev, has_next)
    o_ref[...] = _dwconv(xpad_ref, cw_ref, cb_ref, 1)


def hyena_short_conv(p, conv_w, conv_b, n_batch, gpb, cb0, width):
    nt = p.shape[0]
    total = conv_w.shape[1]
    ncb = total // width
    base = _halo_specs(width, 0, gpb, lambda s: s, nt // ROW_G)

    def shifted(spec):
        f = spec.index_map
        return pl.BlockSpec(spec.block_shape, lambda b, j, c: (f(b, j)[0], cb0 + c))

    kern = functools.partial(_short_conv_kernel, gpb=gpb)
    return pl.pallas_call(
        kern, grid=(n_batch, gpb, ncb),
        in_specs=[shifted(sp) for sp in base] + [
            pl.BlockSpec((conv_w.shape[0], width), lambda b, j, c: (0, c)),
            pl.BlockSpec((1, width), lambda b, j, c: (0, c))],
        out_specs=pl.BlockSpec((ROW_G, width), lambda b, j, c: (b * gpb + j, c)),
        out_shape=jax.ShapeDtypeStruct((nt, total), F32),
        scratch_shapes=[pltpu.VMEM((ROW_G + 16, width), F32)],
        compiler_params=_cparams(3), name="hyena_short_conv")(p, p, p, conv_w, conv_b.reshape(1, total))


def _merge_kernel(a_ref, r_ref, y_ref, g0_ref, g1_ref, g2_ref, bg_ref, wbr_ref, wout_ref, tab_ref, x_ref,
                  o_ref, *, k_gate, gpb, n_batch):
    width = a_ref.shape[1]
    m = None
    for k, (br_ref, gt_ref) in enumerate(((a_ref, g0_ref), (r_ref, g1_ref), (y_ref, g2_ref))):
        gate = jax.nn.sigmoid(gt_ref[...].astype(F32) + bg_ref[:, k * width:(k + 1) * width])
        term = gate * jnp.dot(br_ref[...], wbr_ref[k], preferred_element_type=F32)
        m = term if m is None else m + term
    out = jnp.dot(m.astype(BF16), wout_ref[...], preferred_element_type=F32)
    nsub = x_ref.shape[0] // ROW_G
    for s in range(nsub):
        rows = pl.ds(s * ROW_G, ROW_G)
        gmod = _mod_row(tab_ref, k_gate, pl.program_id(0) * nsub + s, gpb, n_batch)
        o_ref[rows, :] = x_ref[rows, :] + gmod * out[s * ROW_G:(s + 1) * ROW_G, :]


def merge_branches(x, a, r, y, p, b_gate, w_br, w_out, tab, k_gate, gpb, n_batch, cb_g):
    nt, d = x.shape
    tm = _pick(nt, (512, 256))
    kern = functools.partial(_merge_kernel, k_gate=k_gate, gpb=gpb, n_batch=n_batch)
    row = pl.BlockSpec((tm, d), lambda i: (i, 0))
    gspec = lambda k: pl.BlockSpec((tm, d), lambda i: (i, cb_g + k))
    return pl.pallas_call(
        kern, grid=(nt // tm,),
        in_specs=[row, row, row, gspec(0), gspec(1), gspec(2),
                  pl.BlockSpec((1, 3 * d), lambda i: (0, 0)),
                  pl.BlockSpec(w_br.shape, lambda i: (0, 0, 0)),
                  pl.BlockSpec(w_out.shape, lambda i: (0, 0)),
                  pl.BlockSpec(tab.shape, lambda i: (0, 0, 0)), row],
        out_specs=row, out_shape=jax.ShapeDtypeStruct((nt, d), F32),
        compiler_params=_cparams(1), name="merge")(a, r, y, p, p, p, b_gate.reshape(1, 3 * d), w_br, w_out, tab, x)


def _gate_up_kernel(*refs, expert):
    if expert:
        _, f_ref, wg_ref, wu_ref, o_ref = refs
    else:
        f_ref, wg_ref, wu_ref, o_ref = refs
    f = f_ref[...]
    g = jnp.dot(f, wg_ref[...], preferred_element_type=F32)
    u = jnp.dot(f, wu_ref[...], preferred_element_type=F32)
    o_ref[...] = (g * jax.nn.sigmoid(g) * u).astype(o_ref.dtype)


def gate_up(f, wg, wu, block_e=None):
    m, d = f.shape
    dff = wg.shape[-1]
    tn = _pick(dff, (1408, 1024, 512, 256, 128))
    if block_e is None:
        tm = _pick(m, (512, 256))
        grid_spec = pltpu.PrefetchScalarGridSpec(
            num_scalar_prefetch=0, grid=(dff // tn, m // tm),
            in_specs=[pl.BlockSpec((tm, d), lambda j, i: (i, 0)),
                      pl.BlockSpec((d, tn), lambda j, i: (0, j)),
                      pl.BlockSpec((d, tn), lambda j, i: (0, j))],
            out_specs=pl.BlockSpec((tm, tn), lambda j, i: (i, j)))
        args = (f, wg, wu)
    else:
        tm = MOE_BLOCK
        grid_spec = pltpu.PrefetchScalarGridSpec(
            num_scalar_prefetch=1, grid=(dff // tn, m // tm),
            in_specs=[pl.BlockSpec((tm, d), lambda j, i, be: (i, 0)),
                      pl.BlockSpec((None, d, tn), lambda j, i, be: (be[i], 0, j)),
                      pl.BlockSpec((None, d, tn), lambda j, i, be: (be[i], 0, j))],
            out_specs=pl.BlockSpec((tm, tn), lambda j, i, be: (i, j)))
        args = (block_e, f, wg, wu)
    return pl.pallas_call(
        functools.partial(_gate_up_kernel, expert=block_e is not None), grid_spec=grid_spec,
        out_shape=jax.ShapeDtypeStruct((m, dff), BF16),
        compiler_params=_cparams(2), name="gate_up")(*args)


def _down_res_kernel(h_ref, wd_ref, tab_ref, x_ref, o_ref, *, k_gate, gpb, n_batch):
    out = jnp.dot(h_ref[...], wd_ref[...], preferred_element_type=F32)
    nsub = x_ref.shape[0] // ROW_G
    for s in range(nsub):
        rows = pl.ds(s * ROW_G, ROW_G)
        gmod = _mod_row(tab_ref, k_gate, pl.program_id(0) * nsub + s, gpb, n_batch)
        o_ref[rows, :] = x_ref[rows, :] + gmod * out[s * ROW_G:(s + 1) * ROW_G, :]


def down_residual(h, wd, x, tab, k_gate, gpb, n_batch):
    nt, d = x.shape
    dff = h.shape[1]
    tm = _pick(nt, (512, 256))
    kern = functools.partial(_down_res_kernel, k_gate=k_gate, gpb=gpb, n_batch=n_batch)
    return pl.pallas_call(
        kern, grid=(nt // tm,),
        in_specs=[pl.BlockSpec((tm, dff), lambda i: (i, 0)),
                  pl.BlockSpec((dff, d), lambda i: (0, 0)),
                  pl.BlockSpec(tab.shape, lambda i: (0, 0, 0)),
                  pl.BlockSpec((tm, d), lambda i: (i, 0))],
        out_specs=pl.BlockSpec((tm, d), lambda i: (i, 0)),
        out_shape=jax.ShapeDtypeStruct((nt, d), F32),
        compiler_params=_cparams(1), name="down_residual")(h, wd, tab, x)


def _expert_down_kernel(be_ref, h_ref, wd_ref, o_ref):
    o_ref[...] = jnp.dot(h_ref[...], wd_ref[...], preferred_element_type=F32)


def expert_down(h, wd, block_e):
    m, dff = h.shape
    d = wd.shape[-1]
    grid_spec = pltpu.PrefetchScalarGridSpec(
        num_scalar_prefetch=1, grid=(m // MOE_BLOCK,),
        in_specs=[pl.BlockSpec((MOE_BLOCK, dff), lambda i, be: (i, 0)),
                  pl.BlockSpec((None, dff, d), lambda i, be: (be[i], 0, 0))],
        out_specs=pl.BlockSpec((MOE_BLOCK, d), lambda i, be: (i, 0)))
    return pl.pallas_call(_expert_down_kernel, grid_spec=grid_spec,
                          out_shape=jax.ShapeDtypeStruct((m, d), F32),
                          compiler_params=_cparams(1), name="expert_down")(block_e, h, wd)


def _combine_kernel(y0_ref, y1_ref, w0_ref, w1_ref, tab_ref, x_ref, o_ref, *, k_gate, gpb, n_batch):
    nsub = x_ref.shape[0] // ROW_G
    for s in range(nsub):
        rows = pl.ds(s * ROW_G, ROW_G)
        gmod = _mod_row(tab_ref, k_gate, pl.program_id(0) * nsub + s, gpb, n_batch)
        y = y0_ref[rows, :] * w0_ref[rows, :] + y1_ref[rows, :] * w1_ref[rows, :]
        o_ref[rows, :] = x_ref[rows, :] + gmod * y


def moe_combine(y0, y1, w0, w1, x, tab, k_gate, gpb, n_batch):
    nt, d = x.shape
    tm = _pick(nt, (512, 256))
    kern = functools.partial(_combine_kernel, k_gate=k_gate, gpb=gpb, n_batch=n_batch)
    row = pl.BlockSpec((tm, d), lambda i: (i, 0))
    wsp = pl.BlockSpec((tm, 1), lambda i: (i, 0))
    return pl.pallas_call(
        kern, grid=(nt // tm,),
        in_specs=[row, row, wsp, wsp, pl.BlockSpec(tab.shape, lambda i: (0, 0, 0)), row],
        out_specs=row, out_shape=jax.ShapeDtypeStruct((nt, d), F32),
        compiler_params=_cparams(1), name="moe_combine")(y0, y1, w0, w1, tab, x)


def moe_layer(f, logits, w_gate, w_up, w_down, x, tab, k_gate, gpb, n_batch):
    nt = f.shape[0]
    n_asg = nt * TOP_K
    top_v, top_i = lax.top_k(logits[:, :N_EXPERTS], TOP_K)
    top_w = jax.nn.softmax(top_v, axis=-1)
    e_flat = top_i.reshape(-1)
    onehot = (e_flat[:, None] == jnp.arange(N_EXPERTS, dtype=e_flat.dtype)[None, :]).astype(jnp.int32)
    csum = jnp.cumsum(onehot, axis=0)
    counts = csum[-1]
    rank = jnp.take_along_axis(csum, e_flat[:, None], axis=1)[:, 0] - 1
    padded = ((counts + MOE_BLOCK - 1) // MOE_BLOCK) * MOE_BLOCK
    pad_end = jnp.cumsum(padded)
    pad_start = pad_end - padded
    dest = (pad_start[e_flat] + rank).astype(jnp.int32)
    n_blocks = -(-n_asg // MOE_BLOCK) + N_EXPERTS
    n_slots = n_blocks * MOE_BLOCK
    tok_flat = jnp.repeat(jnp.arange(nt, dtype=jnp.int32), TOP_K)
    slot_tok = jnp.zeros((n_slots,), jnp.int32).at[dest].set(tok_flat)
    block_e = jnp.minimum(jnp.searchsorted(pad_end, jnp.arange(n_blocks, dtype=jnp.int32) * MOE_BLOCK,
                                           side='right'), N_EXPERTS - 1).astype(jnp.int32)
    xs = jnp.take(f, slot_tok, axis=0)
    h = gate_up(xs, w_gate, w_up, block_e)
    ys = expert_down(h, w_down, block_e)
    dest2 = dest.reshape(nt, TOP_K)
    y0 = jnp.take(ys, dest2[:, 0], axis=0)
    y1 = jnp.take(ys, dest2[:, 1], axis=0)
    return moe_combine(y0, y1, top_w[:, 0:1], top_w[:, 1:2], x, tab, k_gate, gpb, n_batch)


def hyena_filters(length, w1, b1, w2, b2, freq, w3, decay):
    t = jnp.linspace(0.0, 1.0, length, dtype=F32)[:, None]
    w = 2.0 * math.pi * jnp.arange(length, dtype=F32)[:, None] / length
    bands = jnp.linspace(1e-4, HY_BANDS - 1, HY_BANDS, dtype=F32)
    z = jnp.concatenate([t, jnp.cos(bands * w), -jnp.sin(bands * w)], axis=-1)
    hdn = jnp.sin(freq * (jnp.dot(z, w1, precision=lax.Precision.HIGHEST) + b1))
    hdn = jnp.sin(freq * (jnp.dot(hdn, w2, precision=lax.Precision.HIGHEST) + b2))
    width = decay.shape[-1]
    filt = jnp.dot(hdn, w3, precision=lax.Precision.HIGHEST).reshape(length, HY_ORDER, 2, width)
    window = jnp.exp(-t[:, :, None] * jnp.abs(decay))
    filt = filt * window[:, :, None, :]
    return filt / (jnp.sum(jnp.abs(filt), axis=(0, 2), keepdims=True) + EPS)


DFT_ROWS = ROW_G
SLAB = 16


def _cmul_const(ar, ai, c, s):
    tol = 1e-12
    if abs(s) < tol and abs(c - 1.0) < tol:
        return ar, ai
    if abs(s) < tol and abs(c + 1.0) < tol:
        return -ar, -ai
    if abs(c) < tol and abs(s - 1.0) < tol:
        return -ai, ar
    if abs(c) < tol and abs(s + 1.0) < tol:
        return ai, -ar
    return c * ar - s * ai, s * ar + c * ai


def _fft_list(re, im, sign):
    n = len(re)
    if n == 1:
        return list(re), list(im)
    er, ei = _fft_list(re[0::2], im[0::2], sign)
    qr, qi = _fft_list(re[1::2], im[1::2], sign)
    out_r, out_i = [None] * n, [None] * n
    for k in range(n // 2):
        ang = sign * 2.0 * math.pi * k / n
        tr, ti = _cmul_const(qr[k], qi[k], math.cos(ang), math.sin(ang))
        out_r[k], out_i[k] = er[k] + tr, ei[k] + ti
        out_r[k + n // 2], out_i[k + n // 2] = er[k] - tr, ei[k] - ti
    return out_r, out_i


def _fft_fwd_zero_padded(re, im):
    h = len(re)
    n = 2 * h
    er, ei = _fft_list(re, im, -1.0)
    tw = [_cmul_const(re[a], im[a], math.cos(-2.0 * math.pi * a / n), math.sin(-2.0 * math.pi * a / n))
          for a in range(h)]
    qr, qi = _fft_list([t[0] for t in tw], [t[1] for t in tw], -1.0)
    out_r, out_i = [None] * n, [None] * n
    for m in range(h):
        out_r[2 * m], out_i[2 * m] = er[m], ei[m]
        out_r[2 * m + 1], out_i[2 * m + 1] = qr[m], qi[m]
    return out_r, out_i


def _fft_inv_first_half(re, im):
    n = len(re)
    h = n // 2
    er, ei = _fft_list(re[0::2], im[0::2], 1.0)
    qr, qi = _fft_list(re[1::2], im[1::2], 1.0)
    out_r, out_i = [], []
    for a in range(h):
        tr, ti = _cmul_const(qr[a], qi[a], math.cos(2.0 * math.pi * a / n), math.sin(2.0 * math.pi * a / n))
        out_r.append(er[a] + tr)
        out_i.append(ei[a] + ti)
    return out_r, out_i


def _slab_loop(rows, width, fn):
    def body(i, carry):
        r0 = pl.multiple_of(i * SLAB, SLAB)
        for l0 in range(0, width, 128):
            fn(pl.ds(r0, SLAB), slice(l0, l0 + 128))
        return carry
    lax.fori_loop(0, rows // SLAB, body, 0)


def _hy_stage_a_kernel(x_ref, ar_ref, ai_ref, *, tile0, n_in, zero_padded):
    def slab(rows, lanes):
        re = [x_ref[0, tile0 + a, rows, lanes] for a in range(n_in)]
        im = [x_ref[1, tile0 + a, rows, lanes] for a in range(n_in)]
        out_r, out_i = _fft_fwd_zero_padded(re, im) if zero_padded else _fft_list(re, im, -1.0)
        for k in range(len(out_r)):
            ar_ref[k, rows, lanes] = out_r[k].astype(ar_ref.dtype)
            ai_ref[k, rows, lanes] = out_i[k].astype(ai_ref.dtype)
    _slab_loop(ar_ref.shape[1], ar_ref.shape[2], slab)


def hy_stage_a(x5, col0, width, tile0, n_in, zero_padded):
    _, npair, ntile, rows, _ = x5.shape
    n_a = 2 * n_in if zero_padded else n_in
    rb, cb = 64, 256
    kern = functools.partial(_hy_stage_a_kernel, tile0=tile0, n_in=n_in, zero_padded=zero_padded)
    ospec = pl.BlockSpec((None, n_a, rb, cb), lambda p, c, r: (p, 0, r, c))
    return pl.pallas_call(
        kern, grid=(npair, width // cb, rows // rb),
        in_specs=[pl.BlockSpec((2, None, ntile, rb, cb), lambda p, c, r: (0, p, 0, r, col0 // cb + c))],
        out_specs=[ospec, ospec],
        out_shape=[jax.ShapeDtypeStruct((npair, n_a, rows, width), BF16)] * 2,
        compiler_params=_cparams(3), name="hyena_stage_a")(x5)


def _hy_stage_b_kernel(ar_ref, ai_ref, f_ref, *rest, spectrum_only):
    half = ar_ref.shape[0]
    a = jnp.concatenate([ar_ref[...], ai_ref[...]], axis=0)
    x = jnp.dot(f_ref[...], a, preferred_element_type=F32)
    if spectrum_only:
        or_ref, oi_ref = rest
        or_ref[...] = x[:half]
        oi_ref[...] = x[half:]
        return
    g_ref, hr_ref, hi_ref, or_ref, oi_ref = rest
    xr, xi = x[:half], x[half:]
    hr, hi = hr_ref[...], hi_ref[...]
    y = jnp.concatenate([xr * hr - xi * hi, xr * hi + xi * hr], axis=0).astype(BF16)
    z = jnp.dot(g_ref[...], y, preferred_element_type=F32)
    or_ref[...] = z[:half].astype(or_ref.dtype)
    oi_ref[...] = z[half:].astype(oi_ref.dtype)


def hy_stage_b(ar, ai, fmat, gmat=None, hr=None, hi=None):
    npair, n_a, rows, width = ar.shape
    spectrum_only = gmat is None
    aspec = pl.BlockSpec((None, None, rows, width), lambda k, p: (p, k, 0, 0))
    mspec = pl.BlockSpec((None, 2 * rows, 2 * rows), lambda k, p: (k, 0, 0))
    hspec = pl.BlockSpec((None, rows, width), lambda k, p: (k, 0, 0))
    in_specs = [aspec, aspec, mspec]
    args = [ar, ai, fmat]
    if not spectrum_only:
        in_specs += [mspec, hspec, hspec]
        args += [gmat, hr, hi]
    out_dtype = F32 if spectrum_only else BF16
    return pl.pallas_call(
        functools.partial(_hy_stage_b_kernel, spectrum_only=spectrum_only), grid=(n_a, npair),
        in_specs=in_specs, out_specs=[aspec, aspec],
        out_shape=[jax.ShapeDtypeStruct(ar.shape, out_dtype)] * 2,
        compiler_params=_cparams(2), name="hyena_stage_b")(*args)


def _hy_stage_a_inv_kernel(ar_ref, ai_ref, u_ref, g_ref, skip_ref, o_ref, *, tile0_u, tile0_g):
    def slab(rows, lanes):
        re = [ar_ref[k, rows, lanes].astype(F32) for k in range(ar_ref.shape[0])]
        im = [ai_ref[k, rows, lanes].astype(F32) for k in range(ai_ref.shape[0])]
        out = _fft_inv_first_half(re, im)
        skip = skip_ref[:, lanes]
        for part in range(2):
            for a in range(len(out[part])):
                conv = out[part][a] + skip * u_ref[part, tile0_u + a, rows, lanes]
                o_ref[part, a, rows, lanes] = (g_ref[part, tile0_g + a, rows, lanes] * conv).astype(o_ref.dtype)
    _slab_loop(ar_ref.shape[1], ar_ref.shape[2], slab)


def hy_stage_a_inv(ar, ai, u5, u_col0, tile0_u, g5, g_col0, tile0_g, skip, out_dtype):
    npair, n_a, rows, width = ar.shape
    n_out = n_a // 2
    rb, cb = 64, 256
    kern = functools.partial(_hy_stage_a_inv_kernel, tile0_u=tile0_u, tile0_g=tile0_g)
    aspec = pl.BlockSpec((None, n_a, rb, cb), lambda p, c, r: (p, 0, r, c))
    return pl.pallas_call(
        kern, grid=(npair, width // cb, rows // rb),
        in_specs=[aspec, aspec,
                  pl.BlockSpec((2, None, u5.shape[2], rb, cb), lambda p, c, r: (0, p, 0, r, u_col0 // cb + c)),
                  pl.BlockSpec((2, None, g5.shape[2], rb, cb), lambda p, c, r: (0, p, 0, r, g_col0 // cb + c)),
                  pl.BlockSpec((1, cb), lambda p, c, r: (0, c))],
        out_specs=pl.BlockSpec((2, None, n_out, rb, cb), lambda p, c, r: (0, p, 0, r, c)),
        out_shape=jax.ShapeDtypeStruct((2, npair, n_out, rows, width), out_dtype),
        compiler_params=_cparams(3), name="hyena_stage_a_inv")(ar, ai, u5, g5, skip.reshape(1, width))


def dft_matrices(n_a):
    n = n_a * DFT_ROWS
    kb = jnp.arange(DFT_ROWS, dtype=jnp.int32)[None, :, None]
    b = jnp.arange(DFT_ROWS, dtype=jnp.int32)[None, None, :]
    ka = jnp.arange(n_a, dtype=jnp.int32)[:, None, None]
    ang = (-2.0 * math.pi / n) * ((b * (ka + n_a * kb)) % n).astype(F32)
    fr, fi = jnp.cos(ang), jnp.sin(ang)
    blk = jnp.concatenate([jnp.concatenate([fr, -fi], axis=2), jnp.concatenate([fi, fr], axis=2)], axis=1)
    return blk.astype(BF16), (jnp.swapaxes(blk, 1, 2) * (1.0 / n)).astype(BF16)


def hyena_branch(u5, width, tile0, n_tiles, fargs, skip):
    length = n_tiles * DFT_ROWS
    n_a = 2 * n_tiles
    filt = hyena_filters(length, *fargs)
    fmat, gmat = dft_matrices(n_a)
    x5, x_col0, x_tile0 = u5, 0, tile0
    out = None
    for order in range(HY_ORDER):
        h_fwd, h_bwd = filt[:, order, 0], filt[:, order, 1]
        hc = jnp.concatenate([h_fwd[:1] + h_bwd[:1], h_fwd[1:], jnp.zeros_like(h_fwd[:1]), h_bwd[:0:-1]], axis=0)
        hc5 = jnp.stack([hc, jnp.zeros_like(hc)]).reshape(2, 1, n_a, DFT_ROWS, width)
        hr, hi = hy_stage_b(*hy_stage_a(hc5, 0, width, 0, n_a, False), fmat)
        ar, ai = hy_stage_a(x5, x_col0, width, x_tile0, n_tiles, True)
        ar, ai = hy_stage_b(ar, ai, fmat, gmat, hr[0], hi[0])
        last = order == HY_ORDER - 1
        out = hy_stage_a_inv(ar, ai, x5, x_col0, x_tile0, u5, (order + 1) * width, tile0, skip[order],
                             BF16 if last else F32)
        x5, x_col0, x_tile0 = out, 0, 0
    return out


def kernel(x, c, ctx, c_ctx, w_mod, b_mod, g_mix, g_ffn, w_in, b_gate, w_br, w_out, da_lambda, da_subln_g,
           lru_conv_w, lru_conv_b, lru_wa, lru_ba, lru_wi, lru_bi, lru_lambda, hy_conv_w, hy_conv_b,
           hy_f_w1, hy_f_b1, hy_f_w2, hy_f_b2, hy_f_freq, hy_f_w3, hy_decay, hy_skip, ffn_w_gate, ffn_w_up,
           ffn_w_down, moe_router, moe_w_gate, moe_w_up, moe_w_down, g_final):
    n_batch, length, d = x.shape
    n_ctx = ctx.shape[1]
    depth = w_mod.shape[0]
    assert n_ctx == ROW_G and length % ROW_G == 0 and d % 128 == 0
    tb = n_ctx + length
    gpb = tb // ROW_G
    nt = n_batch * tb
    c_end = w_in.shape[2]
    cb_k, cb_v, cb_lx, cb_q, cb_ly, cb_hy, cb_g = 0, 1, 2, 3, 4, 5, 8
    assert c_end == 11 * d

    xs = jnp.concatenate([ctx, x], axis=1).reshape(nt, d)
    silu_rows = jnp.concatenate([jax.nn.silu(c), jax.nn.silu(c_ctx)[None, :],
                                 jnp.zeros((16 - n_batch - 1, d), F32)], axis=0)
    tables = rope_tables(length, n_ctx, d)

    for li in range(depth):
        lam_init = 0.8 - 0.6 * math.exp(-0.3 * li)
        tab = mm_f32_bias(silu_rows, w_mod[li], b_mod[li], d)
        tab = tab.reshape(16, 6, d).transpose(1, 0, 2)

        h = norm_mod(xs, g_mix[li], tab, 0, 1, gpb, n_batch)
        tm = _pick(nt, (1024, 512, 256))
        p = mm_bf16(h, w_in[li].astype(BF16), BF16, tm, d, "in_proj")

        q, k = rope_qk(p, tables, n_batch, gpb, cb_q, cb_k, d)
        a_out = diff_attention(q, k, p, da_lambda[li], da_subln_g[li], lam_init, n_batch, gpb, cb_v)

        bw = d // LRU_BLOCKS
        r_out = None
        hf = None
        for direction in range(2):
            w_cat = jnp.concatenate([lru_wa[li, direction], lru_wi[li, direction]], axis=-1).astype(BF16)
            res = lru_pass(p, lru_conv_w[li], lru_conv_b[li], w_cat, lru_ba[li, direction],
                           lru_bi[li, direction], lru_lambda[li, direction], n_batch, gpb, cb_lx,
                           reverse=direction == 1, hf=hf, cb_y=cb_ly)
            if direction == 0:
                hf = res
            else:
                r_out = res
        del bw

        u = hyena_short_conv(p, hy_conv_w[li], hy_conv_b[li], n_batch, gpb, cb_hy, d)
        u5 = u.reshape(2, n_batch // 2, gpb, ROW_G, 3 * d)
        fargs = (hy_f_w1[li], hy_f_b1[li], hy_f_w2[li], hy_f_b2[li], hy_f_freq[li], hy_f_w3[li], hy_decay[li])
        y_c = hyena_branch(u5, d, 0, 1, fargs, hy_skip[li])
        y_l = hyena_branch(u5, d, 1, gpb - 1, fargs, hy_skip[li])
        y_out = jnp.concatenate([y_c, y_l], axis=2).reshape(nt, d)

        xs = merge_branches(xs, a_out, r_out, y_out, p, b_gate[li], w_br[li].astype(BF16),
                            w_out[li].astype(BF16), tab, 2, gpb, n_batch, cb_g)

        jj = li // 2
        if li % 2 == 0:
            f = norm_mod(xs, g_ffn[li], tab, 3, 4, gpb, n_batch)
            hh = gate_up(f, ffn_w_gate[jj].astype(BF16), ffn_w_up[jj].astype(BF16))
            xs = down_residual(hh, ffn_w_down[jj].astype(BF16), xs, tab, 5, gpb, n_batch)
        else:
            wr = jnp.concatenate([moe_router[jj], jnp.zeros((d, 128 - N_EXPERTS), F32)], axis=1)
            f, logits = norm_mod(xs, g_ffn[li], tab, 3, 4, gpb, n_batch, w_router=wr)
            xs = moe_layer(f, logits, moe_w_gate[jj].astype(BF16), moe_w_up[jj].astype(BF16),
                           moe_w_down[jj].astype(BF16), xs, tab, 5, gpb, n_batch)

    return final_norm(xs.reshape(n_batch, tb, d), g_final, n_ctx)
```

```python
import functools
import math

import jax
import jax.numpy as jnp
from jax import lax
from jax.experimental import pallas as pl
from jax.experimental.pallas import tpu as pltpu

F32 = jnp.float32
BF16 = jnp.bfloat16

EPS = 1e-6
ROW_G = 256
GRID_W = 64
ROPE_BASE = 10000.0
DA_HEADS = 8
DA_HEAD_DIM = 64
DA_V_DIM = 128
LRU_BLOCKS = 8
LRU_C = 8.0
HY_ORDER = 2
HY_BANDS = 16
N_EXPERTS = 8
TOP_K = 2
MOE_BLOCK = 256
HALO = 16
VMEM_LIMIT = 56 * 1024 * 1024


def _cparams(n_axes):
    return pltpu.CompilerParams(dimension_semantics=("arbitrary",) * n_axes,
                                vmem_limit_bytes=VMEM_LIMIT)


def _pick(n, prefs):
    for p in prefs:
        if n % p == 0:
            return p
    raise ValueError(f"no tile in {prefs} divides {n}")


def _mod_row(tab_ref, k, group, groups_per_batch, n_batch):
    b = group // groups_per_batch
    row = jnp.where(group % groups_per_batch == 0, n_batch, b)
    return tab_ref[k, pl.ds(row, 1), :]


def _mm_f32_kernel(a_ref, w_ref, b_ref, o_ref):
    o_ref[...] = jnp.dot(a_ref[...], w_ref[...], preferred_element_type=F32) + b_ref[...]


def mm_f32_bias(a, w, b, tn):
    m, k = a.shape
    n = w.shape[1]
    return pl.pallas_call(
        _mm_f32_kernel, grid=(n // tn,),
        in_specs=[pl.BlockSpec((m, k), lambda j: (0, 0)),
                  pl.BlockSpec((k, tn), lambda j: (0, j)),
                  pl.BlockSpec((1, tn), lambda j: (0, j))],
        out_specs=pl.BlockSpec((m, tn), lambda j: (0, j)),
        out_shape=jax.ShapeDtypeStruct((m, n), F32),
        compiler_params=_cparams(1), name="mod_matmul")(a, w, b.reshape(1, n))


def _norm_mod_kernel(x_ref, g_ref, tab_ref, *rest, k_shift, k_scale, gpb, n_batch, router):
    if router:
        wr_ref, o_ref, lg_ref = rest
    else:
        (o_ref,) = rest
    nsub = x_ref.shape[0] // ROW_G
    for s in range(nsub):
        grp = pl.program_id(0) * nsub + s
        rows = pl.ds(s * ROW_G, ROW_G)
        xs = x_ref[rows, :]
        y = xs * lax.rsqrt(jnp.mean(xs * xs, axis=-1, keepdims=True) + EPS) * g_ref[...]
        shift = _mod_row(tab_ref, k_shift, grp, gpb, n_batch)
        scale = _mod_row(tab_ref, k_scale, grp, gpb, n_batch)
        h = y * (1.0 + scale) + shift
        o_ref[rows, :] = h.astype(o_ref.dtype)
        if router:
            lg_ref[rows, :] = jnp.dot(h, wr_ref[...], preferred_element_type=F32,
                                      precision=lax.Precision.HIGHEST)


def norm_mod(x, g, tab, k_shift, k_scale, gpb, n_batch, w_router=None):
    nt, d = x.shape
    tm = _pick(nt, (1024, 512, 256))
    router = w_router is not None
    kern = functools.partial(_norm_mod_kernel, k_shift=k_shift, k_scale=k_scale, gpb=gpb,
                             n_batch=n_batch, router=router)
    in_specs = [pl.BlockSpec((tm, d), lambda i: (i, 0)),
                pl.BlockSpec((1, d), lambda i: (0, 0)),
                pl.BlockSpec(tab.shape, lambda i: (0, 0, 0))]
    out_specs = [pl.BlockSpec((tm, d), lambda i: (i, 0))]
    out_shape = [jax.ShapeDtypeStruct((nt, d), BF16)]
    args = [x, g.reshape(1, d), tab]
    if router:
        in_specs.append(pl.BlockSpec(w_router.shape, lambda i: (0, 0)))
        out_specs.append(pl.BlockSpec((tm, w_router.shape[1]), lambda i: (i, 0)))
        out_shape.append(jax.ShapeDtypeStruct((nt, w_router.shape[1]), F32))
        args.append(w_router)
    out = pl.pallas_call(kern, grid=(nt // tm,), in_specs=in_specs, out_specs=out_specs,
                         out_shape=out_shape, compiler_params=_cparams(1), name="norm_mod")(*args)
    return out if router else out[0]


def _final_norm_kernel(x_ref, g_ref, o_ref):
    xs = x_ref[...]
    o_ref[...] = xs * lax.rsqrt(jnp.mean(xs * xs, axis=-1, keepdims=True) + EPS) * g_ref[...]


def final_norm(x3, g, n_ctx):
    b, tb, d = x3.shape
    length = tb - n_ctx
    tm = ROW_G
    off = n_ctx // tm
    return pl.pallas_call(
        _final_norm_kernel, grid=(b, length // tm),
        in_specs=[pl.BlockSpec((None, tm, d), lambda i, j: (i, j + off, 0)),
                  pl.BlockSpec((1, d), lambda i, j: (0, 0))],
        out_specs=pl.BlockSpec((None, tm, d), lambda i, j: (i, j, 0)),
        out_shape=jax.ShapeDtypeStruct((b, length, d), F32),
        compiler_params=_cparams(2), name="final_norm")(x3, g.reshape(1, d))


def _mm_kernel(a_ref, w_ref, o_ref):
    o_ref[...] = jnp.dot(a_ref[...], w_ref[...], preferred_element_type=F32).astype(o_ref.dtype)


def mm_bf16(a, w, out_dtype, tm, tn, name):
    m, k = a.shape
    n = w.shape[1]
    return pl.pallas_call(
        _mm_kernel, grid=(n // tn, m // tm),
        in_specs=[pl.BlockSpec((tm, k), lambda j, i: (i, 0)),
                  pl.BlockSpec((k, tn), lambda j, i: (0, j))],
        out_specs=pl.BlockSpec((tm, tn), lambda j, i: (i, j)),
        out_shape=jax.ShapeDtypeStruct((m, n), out_dtype),
        compiler_params=_cparams(2), name=name)(a, w)


def _rope_kernel(pq_ref, pk_ref, cos_ref, sa_ref, sb_ref, q_ref, k_ref, *, q_scale):
    width = pq_ref.shape[1]
    cos, sa, sb = cos_ref[...], sa_ref[...], sb_ref[...]

    def rot(t):
        return t * cos + pltpu.roll(t, width - 16, 1) * sa + pltpu.roll(t, 16, 1) * sb

    q_ref[...] = (rot(pq_ref[...].astype(F32)) * q_scale).astype(q_ref.dtype)
    k_ref[...] = rot(pk_ref[...].astype(F32)).astype(k_ref.dtype)


def rope_tables(length, n_ctx, width):
    rows = length // GRID_W
    row = jnp.repeat(jnp.arange(rows, dtype=F32), GRID_W)
    col = jnp.tile(jnp.arange(GRID_W, dtype=F32), rows)
    n_freq = DA_HEAD_DIM // 4
    inv = ROPE_BASE ** (-jnp.arange(n_freq, dtype=F32) / n_freq)
    ar = row[:, None] * inv
    ac = col[:, None] * inv
    ang = jnp.concatenate([ar, ar, ac, ac], axis=-1)
    ang = jnp.concatenate([jnp.zeros((n_ctx, DA_HEAD_DIM), F32), ang], axis=0)
    reps = width // DA_HEAD_DIM
    cos = jnp.tile(jnp.cos(ang), (1, reps))
    sin = jnp.tile(jnp.sin(ang), (1, reps))
    first = (jnp.arange(width) % (2 * n_freq)) < n_freq
    sa = jnp.where(first[None, :], -sin, 0.0)
    sb = jnp.where(first[None, :], 0.0, sin)
    return cos, sa, sb


def rope_qk(p, tables, n_batch, gpb, cb_q, cb_k, width):
    nt = p.shape[0]
    cos, sa, sb = tables
    kern = functools.partial(_rope_kernel, q_scale=math.log2(math.e) * DA_HEAD_DIM ** -0.5)
    tspec = pl.BlockSpec((ROW_G, width), lambda b, j: (j, 0))
    ospec = pl.BlockSpec((ROW_G, width), lambda b, j: (b * gpb + j, 0))
    return pl.pallas_call(
        kern, grid=(n_batch, gpb),
        in_specs=[pl.BlockSpec((ROW_G, width), lambda b, j: (b * gpb + j, cb_q)),
                  pl.BlockSpec((ROW_G, width), lambda b, j: (b * gpb + j, cb_k)),
                  tspec, tspec, tspec],
        out_specs=[ospec, ospec],
        out_shape=[jax.ShapeDtypeStruct((nt, width), BF16)] * 2,
        compiler_params=_cparams(2), name="rope")(p, p, cos, sa, sb)


def _attn_kernel(lamv_ref, g_ref, q_ref, k_ref, v_ref, o_ref, vaug_ref, m_ref, acc_ref,
                 *, lam_init, n_ctx, tk):
    j = pl.program_id(2)
    dv = v_ref.shape[1]

    @pl.when(j == 0)
    def _():
        vaug_ref[:, :dv] = v_ref[...]
        vaug_ref[:, dv:] = jnp.ones((vaug_ref.shape[0], vaug_ref.shape[1] - dv), vaug_ref.dtype)

    q = q_ref[...]
    lane = lax.broadcasted_iota(jnp.int32, q.shape, 1)
    zero = jnp.zeros_like(q)
    qq = jnp.concatenate([jnp.where(lane < DA_HEAD_DIM, q, zero),
                          jnp.where(lane >= DA_HEAD_DIM, q, zero)], axis=0)
    tq = q.shape[0]

    def scores(start, size):
        kc = k_ref[pl.ds(start, size), :]
        return lax.dot_general(qq, kc, (((1,), (1,)), ((), ())), preferred_element_type=F32)

    def update(s, start, size, first=False):
        m_new = jnp.max(s, axis=-1, keepdims=True)
        if first:
            m_new = jnp.broadcast_to(m_new, m_ref.shape)
        else:
            m_prev = m_ref[...]
            m_new = jnp.maximum(m_prev, m_new)
        p = jnp.exp2(s - jnp.concatenate([m_new] * (size // 128), axis=1))
        pv = jnp.dot(p.astype(BF16), vaug_ref[pl.ds(start, size), :], preferred_element_type=F32)
        if first:
            acc_ref[...] = pv
        else:
            alpha = jnp.exp2(m_prev - m_new)
            acc_ref[...] = jnp.concatenate([alpha] * (acc_ref.shape[1] // 128), axis=1) * acc_ref[...] + pv
        m_ref[...] = m_new

    s_ctx = scores(0, n_ctx)

    @pl.when(j == 0)
    def _():
        update(s_ctx, 0, n_ctx, first=True)

    @pl.when(j > 0)
    def _():
        starts = [n_ctx + i * tk for i in range((k_ref.shape[0] - n_ctx) // tk)]
        s_next = scores(starts[0], tk)
        update(s_ctx, 0, n_ctx, first=True)
        for i, start in enumerate(starts):
            s_cur = s_next
            if i + 1 < len(starts):
                s_next = scores(starts[i + 1], tk)
            update(s_cur, start, tk)

    lv = lamv_ref[...]
    lam = (jnp.exp(jnp.sum(lv[0:1] * lv[1:2], axis=-1, keepdims=True))
           - jnp.exp(jnp.sum(lv[2:3] * lv[3:4], axis=-1, keepdims=True)) + lam_init)
    o = acc_ref[:, :dv] / acc_ref[:, dv:]
    o = o[:tq] - lam * o[tq:]
    o = o * lax.rsqrt(jnp.mean(o * o, axis=-1, keepdims=True) + EPS) * g_ref[...]
    o_ref[...] = (o * (1.0 - lam_init)).astype(o_ref.dtype)


def diff_attention(q, k, p, lam_vecs, subln_g, lam_init, n_batch, gpb, cb_v):
    nt, width = q.shape
    tb = gpb * ROW_G
    n_ctx = ROW_G
    tk = _pick(tb - n_ctx, (2048, 1024, 512, 256))
    kern = functools.partial(_attn_kernel, lam_init=lam_init, n_ctx=n_ctx, tk=tk)
    assert DA_V_DIM == 128
    hb = 1
    return pl.pallas_call(
        kern, grid=(n_batch, DA_HEADS, gpb),
        in_specs=[pl.BlockSpec(lam_vecs.shape, lambda b, h, j: (0, 0)),
                  pl.BlockSpec((1, DA_V_DIM), lambda b, h, j: (0, 0)),
                  pl.BlockSpec((ROW_G, DA_V_DIM), lambda b, h, j: (b * gpb + j, h)),
                  pl.BlockSpec((tb, DA_V_DIM), lambda b, h, j: (b, h)),
                  pl.BlockSpec((tb, DA_V_DIM), lambda b, h, j: (b, cb_v * (width // DA_V_DIM) + h * hb))],
        out_specs=pl.BlockSpec((ROW_G, DA_V_DIM), lambda b, h, j: (b * gpb + j, h)),
        out_shape=jax.ShapeDtypeStruct((nt, width), BF16),
        scratch_shapes=[pltpu.VMEM((tb, 2 * DA_V_DIM), BF16), pltpu.VMEM((2 * ROW_G, 128), F32),
                        pltpu.VMEM((2 * ROW_G, 2 * DA_V_DIM), F32)],
        compiler_params=_cparams(3), name="diff_attn")(
            lam_vecs, subln_g.reshape(1, DA_V_DIM), q, k, p)


def _fill_padded(xpad_ref, main_ref, prev_ref, next_ref, has_prev, has_next):
    prev = prev_ref[...].astype(F32)[HALO - 8:, :]
    nxt = next_ref[...].astype(F32)[:8, :]
    xpad_ref[0:8, :] = jnp.where(has_prev, prev, 0.0)
    xpad_ref[8:8 + ROW_G, :] = main_ref[...].astype(F32)
    xpad_ref[8 + ROW_G:16 + ROW_G, :] = jnp.where(has_next, nxt, 0.0)


def _dwconv(xpad_ref, w_ref, b_ref, left):
    taps = w_ref.shape[0]
    acc = b_ref[...] + w_ref[0:1, :] * xpad_ref[pl.ds(8 - left, ROW_G), :]
    for t in range(1, taps):
        acc = acc + w_ref[t:t + 1, :] * xpad_ref[pl.ds(8 - left + t, ROW_G), :]
    return acc


def _halo_specs(width, cb, gpb, tile_of, n_groups):
    per = ROW_G // HALO

    def main(b, s, *_):
        return (b * gpb + tile_of(s), cb)

    def prev(b, s, *_):
        return (jnp.maximum((b * gpb + tile_of(s)) * per - 1, 0), cb)

    def nxt(b, s, *_):
        return (jnp.minimum((b * gpb + tile_of(s) + 1) * per, n_groups * per - 1), cb)

    return [pl.BlockSpec((ROW_G, width), main), pl.BlockSpec((HALO, width), prev),
            pl.BlockSpec((HALO, width), nxt)]


def _seq_flags(j, gpb):
    return j >= 2, jnp.logical_and(j >= 1, j <= gpb - 2)


SCAN_G = 8


def _scan_groups(a, b, reverse):
    shape = a.shape
    a = a.reshape(shape[0] // SCAN_G, SCAN_G, shape[1])
    b = b.reshape(a.shape)
    pos = lax.broadcasted_iota(jnp.int32, a.shape, 1)
    s = 1
    while s < SCAN_G:
        keep = pos < SCAN_G - s if reverse else pos >= s
        shift = SCAN_G - s if reverse else s
        a_sh = jnp.where(keep, pltpu.roll(a, shift, 1), 1.0)
        b_sh = jnp.where(keep, pltpu.roll(b, shift, 1), 0.0)
        b = a * b_sh + b
        a = a * a_sh
        s *= 2
    return a.reshape(shape), b.reshape(shape)


def _gelu_tanh(x):
    return 0.5 * x * (1.0 + jnp.tanh(math.sqrt(2.0 / math.pi) * (x + 0.044715 * (x * x * x))))


def _lru_kernel(main_ref, prev_ref, next_ref, cw_ref, cb_ref, w_ref, ba_ref, bi_ref, lam_ref, *rest,
                gpb, reverse):
    if reverse:
        hf_ref, ly_ref, o_ref, xpad_ref, carry_ref = rest
    else:
        o_ref, xpad_ref, carry_ref = rest
    s = pl.program_id(1)
    j = jnp.where(s == 0, 0, gpb - s) if reverse else s
    has_prev, has_next = _seq_flags(j, gpb)
    _fill_padded(xpad_ref, main_ref, prev_ref, next_ref, has_prev, has_next)
    xc = _dwconv(xpad_ref, cw_ref, cb_ref, 2)
    xcb = xc.astype(BF16)
    bw = w_ref.shape[1]
    r_parts, i_parts = [], []
    for n in range(w_ref.shape[0]):
        res = jnp.dot(xcb[:, n * bw:(n + 1) * bw], w_ref[n], preferred_element_type=F32)
        r_parts.append(res[:, :bw])
        i_parts.append(res[:, bw:])
    r = jax.nn.sigmoid(jnp.concatenate(r_parts, axis=1) + ba_ref[...])
    gate_i = jax.nn.sigmoid(jnp.concatenate(i_parts, axis=1) + bi_ref[...])
    nl = -lam_ref[...]
    softplus = jnp.maximum(nl, 0.0) + jnp.log(1.0 + jnp.exp(-jnp.abs(nl)))
    a = jnp.exp((-LRU_C) * r * softplus)
    bb = jnp.sqrt(1.0 - a * a) * (gate_i * xc)
    a_grp, h_grp = _scan_groups(a, bb, reverse)

    @pl.when(s == 0)
    def _():
        carry_ref[...] = jnp.zeros(carry_ref.shape, F32)

    h = carry_ref[...]
    n_grp = ROW_G // SCAN_G
    pieces = [None] * n_grp
    for g in (reversed(range(n_grp)) if reverse else range(n_grp)):
        rows = slice(g * SCAN_G, (g + 1) * SCAN_G)
        hg = a_grp[rows] * h + h_grp[rows]
        pieces[g] = hg
        h = hg[0:1, :] if reverse else hg[SCAN_G - 1:SCAN_G, :]
    carry_ref[...] = h
    h_all = jnp.concatenate(pieces, axis=0)
    if reverse:
        ly = ly_ref[...].astype(F32)
        o_ref[...] = (_gelu_tanh(ly) * (hf_ref[...] + h_all)).astype(o_ref.dtype)
    else:
        o_ref[...] = h_all


def lru_pass(p, conv_w, conv_b, w_cat, ba, bi, lam, n_batch, gpb, cb_x, reverse, hf=None, cb_y=None):
    nt = p.shape[0]
    width = conv_w.shape[1]
    tile_of = (lambda s: jnp.where(s == 0, 0, gpb - s)) if reverse else (lambda s: s)
    kern = functools.partial(_lru_kernel, gpb=gpb, reverse=reverse)
    const2 = lambda b, s: (0, 0)
    in_specs = _halo_specs(width, cb_x, gpb, tile_of, nt // ROW_G) + [
        pl.BlockSpec(conv_w.shape, const2), pl.BlockSpec((1, width), const2),
        pl.BlockSpec(w_cat.shape, lambda b, s: (0, 0, 0)),
        pl.BlockSpec((1, width), const2), pl.BlockSpec((1, width), const2),
        pl.BlockSpec((1, width), const2)]
    args = [p, p, p, conv_w, conv_b.reshape(1, width), w_cat, ba.reshape(1, width),
            bi.reshape(1, width), lam.reshape(1, width)]
    row_spec = lambda cb: pl.BlockSpec((ROW_G, width), lambda b, s: (b * gpb + tile_of(s), cb))
    if reverse:
        in_specs += [row_spec(0), row_spec(cb_y)]
        args += [hf, p]
        out_dtype = BF16
    else:
        out_dtype = F32
    return pl.pallas_call(
        kern, grid=(n_batch, gpb), in_specs=in_specs, out_specs=row_spec(0),
        out_shape=jax.ShapeDtypeStruct((nt, width), out_dtype),
        scratch_shapes=[pltpu.VMEM((ROW_G + 16, width), F32), pltpu.VMEM((1, width), F32)],
        compiler_params=_cparams(2), name="lru_bwd" if reverse else "lru_fwd")(*args)


def _short_conv_kernel(main_ref, prev_ref, next_ref, cw_ref, cb_ref, o_ref, xpad_ref, *, gpb):
    j = pl.program_id(1)
    has_prev, has_next = _seq_flags(j, gpb)
    _fill_padded(xpad_ref, main_ref, prev_ref, next_ref, has_prev, has_next)
    o_ref[...] = _dwconv(xpad_ref, cw_ref, cb_ref, 1).astype(o_ref.dtype)


def hyena_short_conv(p, conv_w, conv_b, n_batch, gpb, cb0, width):
    nt = p.shape[0]
    total = conv_w.shape[1]
    ncb = total // width
    base = _halo_specs(width, 0, gpb, lambda s: s, nt // ROW_G)

    def shifted(spec):
        f = spec.index_map
        return pl.BlockSpec(spec.block_shape, lambda b, j, c: (f(b, j)[0], cb0 + c))

    kern = functools.partial(_short_conv_kernel, gpb=gpb)
    return pl.pallas_call(
        kern, grid=(n_batch, gpb, ncb),
        in_specs=[shifted(sp) for sp in base] + [
            pl.BlockSpec((conv_w.shape[0], width), lambda b, j, c: (0, c)),
            pl.BlockSpec((1, width), lambda b, j, c: (0, c))],
        out_specs=pl.BlockSpec((ROW_G, width), lambda b, j, c: (b * gpb + j, c)),
        out_shape=jax.ShapeDtypeStruct((nt, total), BF16),
        scratch_shapes=[pltpu.VMEM((ROW_G + 16, width), F32)],
        compiler_params=_cparams(3), name="hyena_short_conv")(p, p, p, conv_w, conv_b.reshape(1, total))


def _merge_kernel(a_ref, r_ref, y_ref, g0_ref, g1_ref, g2_ref, bg_ref, wbr_ref, wout_ref, tab_ref, x_ref,
                  o_ref, *, k_gate, gpb, n_batch):
    width = a_ref.shape[1]
    m = None
    for k, (br_ref, gt_ref) in enumerate(((a_ref, g0_ref), (r_ref, g1_ref), (y_ref, g2_ref))):
        gate = jax.nn.sigmoid(gt_ref[...].astype(F32) + bg_ref[:, k * width:(k + 1) * width])
        term = gate * jnp.dot(br_ref[...], wbr_ref[k], preferred_element_type=F32)
        m = term if m is None else m + term
    out = jnp.dot(m.astype(BF16), wout_ref[...], preferred_element_type=F32)
    nsub = x_ref.shape[0] // ROW_G
    for s in range(nsub):
        rows = pl.ds(s * ROW_G, ROW_G)
        gmod = _mod_row(tab_ref, k_gate, pl.program_id(0) * nsub + s, gpb, n_batch)
        o_ref[rows, :] = x_ref[rows, :] + gmod * out[s * ROW_G:(s + 1) * ROW_G, :]


def merge_branches(x, a, r, y, p, b_gate, w_br, w_out, tab, k_gate, gpb, n_batch, cb_g):
    nt, d = x.shape
    tm = _pick(nt, (512, 256))
    kern = functools.partial(_merge_kernel, k_gate=k_gate, gpb=gpb, n_batch=n_batch)
    row = pl.BlockSpec((tm, d), lambda i: (i, 0))
    gspec = lambda k: pl.BlockSpec((tm, d), lambda i: (i, cb_g + k))
    return pl.pallas_call(
        kern, grid=(nt // tm,),
        in_specs=[row, row, row, gspec(0), gspec(1), gspec(2),
                  pl.BlockSpec((1, 3 * d), lambda i: (0, 0)),
                  pl.BlockSpec(w_br.shape, lambda i: (0, 0, 0)),
                  pl.BlockSpec(w_out.shape, lambda i: (0, 0)),
                  pl.BlockSpec(tab.shape, lambda i: (0, 0, 0)), row],
        out_specs=row, out_shape=jax.ShapeDtypeStruct((nt, d), F32),
        compiler_params=_cparams(1), name="merge")(a, r, y, p, p, p, b_gate.reshape(1, 3 * d), w_br, w_out, tab, x)


def _gate_up_kernel(*refs, expert):
    if expert:
        _, f_ref, wg_ref, wu_ref, o_ref = refs
    else:
        f_ref, wg_ref, wu_ref, o_ref = refs
    f = f_ref[...]
    g = jnp.dot(f, wg_ref[...], preferred_element_type=F32)
    u = jnp.dot(f, wu_ref[...], preferred_element_type=F32)
    o_ref[...] = (g * jax.nn.sigmoid(g) * u).astype(o_ref.dtype)


def gate_up(f, wg, wu, block_e=None):
    m, d = f.shape
    dff = wg.shape[-1]
    tn = _pick(dff, (1408, 1024, 512, 256, 128))
    if block_e is None:
        tm = _pick(m, (512, 256))
        grid_spec = pltpu.PrefetchScalarGridSpec(
            num_scalar_prefetch=0, grid=(dff // tn, m // tm),
            in_specs=[pl.BlockSpec((tm, d), lambda j, i: (i, 0)),
                      pl.BlockSpec((d, tn), lambda j, i: (0, j)),
                      pl.BlockSpec((d, tn), lambda j, i: (0, j))],
            out_specs=pl.BlockSpec((tm, tn), lambda j, i: (i, j)))
        args = (f, wg, wu)
    else:
        tm = MOE_BLOCK
        grid_spec = pltpu.PrefetchScalarGridSpec(
            num_scalar_prefetch=1, grid=(dff // tn, m // tm),
            in_specs=[pl.BlockSpec((tm, d), lambda j, i, be: (i, 0)),
                      pl.BlockSpec((None, d, tn), lambda j, i, be: (be[i], 0, j)),
                      pl.BlockSpec((None, d, tn), lambda j, i, be: (be[i], 0, j))],
            out_specs=pl.BlockSpec((tm, tn), lambda j, i, be: (i, j)))
        args = (block_e, f, wg, wu)
    return pl.pallas_call(
        functools.partial(_gate_up_kernel, expert=block_e is not None), grid_spec=grid_spec,
        out_shape=jax.ShapeDtypeStruct((m, dff), BF16),
        compiler_params=_cparams(2), name="gate_up")(*args)


def _down_res_kernel(h_ref, wd_ref, tab_ref, x_ref, o_ref, *, k_gate, gpb, n_batch):
    out = jnp.dot(h_ref[...], wd_ref[...], preferred_element_type=F32)
    nsub = x_ref.shape[0] // ROW_G
    for s in range(nsub):
        rows = pl.ds(s * ROW_G, ROW_G)
        gmod = _mod_row(tab_ref, k_gate, pl.program_id(0) * nsub + s, gpb, n_batch)
        o_ref[rows, :] = x_ref[rows, :] + gmod * out[s * ROW_G:(s + 1) * ROW_G, :]


def down_residual(h, wd, x, tab, k_gate, gpb, n_batch):
    nt, d = x.shape
    dff = h.shape[1]
    tm = _pick(nt, (512, 256))
    kern = functools.partial(_down_res_kernel, k_gate=k_gate, gpb=gpb, n_batch=n_batch)
    return pl.pallas_call(
        kern, grid=(nt // tm,),
        in_specs=[pl.BlockSpec((tm, dff), lambda i: (i, 0)),
                  pl.BlockSpec((dff, d), lambda i: (0, 0)),
                  pl.BlockSpec(tab.shape, lambda i: (0, 0, 0)),
                  pl.BlockSpec((tm, d), lambda i: (i, 0))],
        out_specs=pl.BlockSpec((tm, d), lambda i: (i, 0)),
        out_shape=jax.ShapeDtypeStruct((nt, d), F32),
        compiler_params=_cparams(1), name="down_residual")(h, wd, tab, x)


def _expert_down_kernel(be_ref, h_ref, wd_ref, o_ref):
    o_ref[...] = jnp.dot(h_ref[...], wd_ref[...], preferred_element_type=F32)


def expert_down(h, wd, block_e):
    m, dff = h.shape
    d = wd.shape[-1]
    grid_spec = pltpu.PrefetchScalarGridSpec(
        num_scalar_prefetch=1, grid=(m // MOE_BLOCK,),
        in_specs=[pl.BlockSpec((MOE_BLOCK, dff), lambda i, be: (i, 0)),
                  pl.BlockSpec((None, dff, d), lambda i, be: (be[i], 0, 0))],
        out_specs=pl.BlockSpec((MOE_BLOCK, d), lambda i, be: (i, 0)))
    return pl.pallas_call(_expert_down_kernel, grid_spec=grid_spec,
                          out_shape=jax.ShapeDtypeStruct((m, d), F32),
                          compiler_params=_cparams(1), name="expert_down")(block_e, h, wd)


def _combine_kernel(y0_ref, y1_ref, w0_ref, w1_ref, tab_ref, x_ref, o_ref, *, k_gate, gpb, n_batch):
    nsub = x_ref.shape[0] // ROW_G
    for s in range(nsub):
        rows = pl.ds(s * ROW_G, ROW_G)
        gmod = _mod_row(tab_ref, k_gate, pl.program_id(0) * nsub + s, gpb, n_batch)
        y = y0_ref[rows, :] * w0_ref[rows, :] + y1_ref[rows, :] * w1_ref[rows, :]
        o_ref[rows, :] = x_ref[rows, :] + gmod * y


def moe_combine(y0, y1, w0, w1, x, tab, k_gate, gpb, n_batch):
    nt, d = x.shape
    tm = _pick(nt, (512, 256))
    kern = functools.partial(_combine_kernel, k_gate=k_gate, gpb=gpb, n_batch=n_batch)
    row = pl.BlockSpec((tm, d), lambda i: (i, 0))
    wsp = pl.BlockSpec((tm, 1), lambda i: (i, 0))
    return pl.pallas_call(
        kern, grid=(nt // tm,),
        in_specs=[row, row, wsp, wsp, pl.BlockSpec(tab.shape, lambda i: (0, 0, 0)), row],
        out_specs=row, out_shape=jax.ShapeDtypeStruct((nt, d), F32),
        compiler_params=_cparams(1), name="moe_combine")(y0, y1, w0, w1, tab, x)


def moe_layer(f, logits, w_gate, w_up, w_down, x, tab, k_gate, gpb, n_batch):
    nt = f.shape[0]
    n_asg = nt * TOP_K
    top_v, top_i = lax.top_k(logits[:, :N_EXPERTS], TOP_K)
    top_w = jax.nn.softmax(top_v, axis=-1)
    e_flat = top_i.reshape(-1)
    onehot = (e_flat[:, None] == jnp.arange(N_EXPERTS, dtype=e_flat.dtype)[None, :]).astype(jnp.int32)
    csum = jnp.cumsum(onehot, axis=0)
    counts = csum[-1]
    rank = jnp.take_along_axis(csum, e_flat[:, None], axis=1)[:, 0] - 1
    padded = ((counts + MOE_BLOCK - 1) // MOE_BLOCK) * MOE_BLOCK
    pad_end = jnp.cumsum(padded)
    pad_start = pad_end - padded
    dest = (pad_start[e_flat] + rank).astype(jnp.int32)
    n_blocks = -(-n_asg // MOE_BLOCK) + N_EXPERTS
    n_slots = n_blocks * MOE_BLOCK
    tok_flat = jnp.repeat(jnp.arange(nt, dtype=jnp.int32), TOP_K)
    slot_tok = jnp.zeros((n_slots,), jnp.int32).at[dest].set(tok_flat)
    block_e = jnp.minimum(jnp.searchsorted(pad_end, jnp.arange(n_blocks, dtype=jnp.int32) * MOE_BLOCK,
                                           side='right'), N_EXPERTS - 1).astype(jnp.int32)
    xs = jnp.take(f, slot_tok, axis=0)
    h = gate_up(xs, w_gate, w_up, block_e)
    ys = expert_down(h, w_down, block_e)
    dest2 = dest.reshape(nt, TOP_K)
    y0 = jnp.take(ys, dest2[:, 0], axis=0)
    y1 = jnp.take(ys, dest2[:, 1], axis=0)
    return moe_combine(y0, y1, top_w[:, 0:1], top_w[:, 1:2], x, tab, k_gate, gpb, n_batch)


def hyena_conv_filters(length, w1, b1, w2, b2, freq, w3, decay):
    n = 2 * length
    r = jnp.arange(n, dtype=jnp.int32)
    tap = jnp.where(r < length, r, jnp.where(r == length, 0, n - r))
    t = jnp.linspace(0.0, 1.0, length, dtype=F32)[tap][:, None]
    w = 2.0 * math.pi * tap.astype(F32)[:, None] / length
    bands = jnp.linspace(1e-4, HY_BANDS - 1, HY_BANDS, dtype=F32)
    z = jnp.concatenate([t, jnp.cos(bands * w), -jnp.sin(bands * w)], axis=-1)
    hdn = jnp.sin(freq * (jnp.dot(z, w1, precision=lax.Precision.HIGHEST) + b1))
    hdn = jnp.sin(freq * (jnp.dot(hdn, w2, precision=lax.Precision.HIGHEST) + b2))
    width = decay.shape[-1]
    filt = jnp.dot(hdn, w3, precision=lax.Precision.HIGHEST).reshape(n, HY_ORDER, 2, width)
    filt = filt * jnp.exp(-t[:, :, None] * jnp.abs(decay))[:, :, None, :]
    norm = jnp.sum(jnp.abs(filt[:length]), axis=(0, 2)) + EPS
    fwd, bwd = filt[:, :, 0], filt[:, :, 1]
    rr = r[:, None, None]
    hc = jnp.where(rr < length, fwd, bwd) + jnp.where(rr == 0, bwd, 0.0)
    return jnp.where(rr == length, 0.0, hc) / norm


DFT_ROWS = ROW_G
SLAB = 16


def _cmul_const(ar, ai, c, s):
    tol = 1e-12
    if abs(s) < tol and abs(c - 1.0) < tol:
        return ar, ai
    if abs(s) < tol and abs(c + 1.0) < tol:
        return -ar, -ai
    if abs(c) < tol and abs(s - 1.0) < tol:
        return -ai, ar
    if abs(c) < tol and abs(s + 1.0) < tol:
        return ai, -ar
    return c * ar - s * ai, s * ar + c * ai


def _fft_list(re, im, sign):
    n = len(re)
    if n == 1:
        return list(re), list(im)
    er, ei = _fft_list(re[0::2], im[0::2], sign)
    qr, qi = _fft_list(re[1::2], im[1::2], sign)
    out_r, out_i = [None] * n, [None] * n
    for k in range(n // 2):
        ang = sign * 2.0 * math.pi * k / n
        tr, ti = _cmul_const(qr[k], qi[k], math.cos(ang), math.sin(ang))
        out_r[k], out_i[k] = er[k] + tr, ei[k] + ti
        out_r[k + n // 2], out_i[k + n // 2] = er[k] - tr, ei[k] - ti
    return out_r, out_i


def _fft_fwd_zero_padded(re, im):
    h = len(re)
    n = 2 * h
    er, ei = _fft_list(re, im, -1.0)
    tw = [_cmul_const(re[a], im[a], math.cos(-2.0 * math.pi * a / n), math.sin(-2.0 * math.pi * a / n))
          for a in range(h)]
    qr, qi = _fft_list([t[0] for t in tw], [t[1] for t in tw], -1.0)
    out_r, out_i = [None] * n, [None] * n
    for m in range(h):
        out_r[2 * m], out_i[2 * m] = er[m], ei[m]
        out_r[2 * m + 1], out_i[2 * m + 1] = qr[m], qi[m]
    return out_r, out_i


def _fft_inv_first_half(re, im):
    n = len(re)
    h = n // 2
    er, ei = _fft_list(re[0::2], im[0::2], 1.0)
    qr, qi = _fft_list(re[1::2], im[1::2], 1.0)
    out_r, out_i = [], []
    for a in range(h):
        tr, ti = _cmul_const(qr[a], qi[a], math.cos(2.0 * math.pi * a / n), math.sin(2.0 * math.pi * a / n))
        out_r.append(er[a] + tr)
        out_i.append(ei[a] + ti)
    return out_r, out_i


def _slab_loop(rows, width, fn):
    def body(i, carry):
        r0 = pl.multiple_of(i * SLAB, SLAB)
        for l0 in range(0, width, 128):
            fn(pl.ds(r0, SLAB), slice(l0, l0 + 128))
        return carry
    lax.fori_loop(0, rows // SLAB, body, 0)


def _hy_stage_a_kernel(x_ref, ar_ref, ai_ref, *, tile0, n_in, zero_padded):
    def slab(rows, lanes):
        re = [x_ref[0, tile0 + a, rows, lanes].astype(F32) for a in range(n_in)]
        im = [x_ref[1, tile0 + a, rows, lanes].astype(F32) for a in range(n_in)]
        out_r, out_i = _fft_fwd_zero_padded(re, im) if zero_padded else _fft_list(re, im, -1.0)
        for k in range(len(out_r)):
            ar_ref[k, rows, lanes] = out_r[k].astype(ar_ref.dtype)
            ai_ref[k, rows, lanes] = out_i[k].astype(ai_ref.dtype)
    _slab_loop(ar_ref.shape[1], ar_ref.shape[2], slab)


def hy_stage_a(x5, col0, width, tile0, n_in, zero_padded):
    _, npair, ntile, rows, _ = x5.shape
    n_a = 2 * n_in if zero_padded else n_in
    rb, cb = 64, 256
    kern = functools.partial(_hy_stage_a_kernel, tile0=tile0, n_in=n_in, zero_padded=zero_padded)
    ospec = pl.BlockSpec((None, n_a, rb, cb), lambda p, c, r: (p, 0, r, c))
    return pl.pallas_call(
        kern, grid=(npair, width // cb, rows // rb),
        in_specs=[pl.BlockSpec((2, None, ntile, rb, cb), lambda p, c, r: (0, p, 0, r, col0 // cb + c))],
        out_specs=[ospec, ospec],
        out_shape=[jax.ShapeDtypeStruct((npair, n_a, rows, width), BF16)] * 2,
        compiler_params=_cparams(3), name="hyena_stage_a")(x5)


def _hy_stage_b_kernel(ar_ref, ai_ref, f_ref, *rest, spectrum_only):
    half = ar_ref.shape[0]
    a = jnp.concatenate([ar_ref[...], ai_ref[...]], axis=0)
    x = jnp.dot(f_ref[...], a, preferred_element_type=F32)
    if spectrum_only:
        or_ref, oi_ref = rest
        or_ref[...] = x[:half]
        oi_ref[...] = x[half:]
        return
    g_ref, hr_ref, hi_ref, or_ref, oi_ref = rest
    xr, xi = x[:half], x[half:]
    hr, hi = hr_ref[...], hi_ref[...]
    y = jnp.concatenate([xr * hr - xi * hi, xr * hi + xi * hr], axis=0).astype(BF16)
    z = jnp.dot(g_ref[...], y, preferred_element_type=F32)
    or_ref[...] = z[:half].astype(or_ref.dtype)
    oi_ref[...] = z[half:].astype(oi_ref.dtype)


def hy_stage_b(ar, ai, fmat, gmat=None, hr=None, hi=None):
    npair, n_a, rows, width = ar.shape
    spectrum_only = gmat is None
    aspec = pl.BlockSpec((None, None, rows, width), lambda k, p: (p, k, 0, 0))
    mspec = pl.BlockSpec((None, 2 * rows, 2 * rows), lambda k, p: (k, 0, 0))
    hspec = pl.BlockSpec((None, rows, width), lambda k, p: (k, 0, 0))
    in_specs = [aspec, aspec, mspec]
    args = [ar, ai, fmat]
    if not spectrum_only:
        in_specs += [mspec, hspec, hspec]
        args += [gmat, hr, hi]
    out_dtype = F32 if spectrum_only else BF16
    return pl.pallas_call(
        functools.partial(_hy_stage_b_kernel, spectrum_only=spectrum_only), grid=(n_a, npair),
        in_specs=in_specs, out_specs=[aspec, aspec],
        out_shape=[jax.ShapeDtypeStruct(ar.shape, out_dtype)] * 2,
        compiler_params=_cparams(2), name="hyena_stage_b")(*args)


def _hy_stage_a_inv_kernel(ar_ref, ai_ref, u_ref, g_ref, skip_ref, o_ref, *, tile0_u, tile0_g):
    def slab(rows, lanes):
        re = [ar_ref[k, rows, lanes].astype(F32) for k in range(ar_ref.shape[0])]
        im = [ai_ref[k, rows, lanes].astype(F32) for k in range(ai_ref.shape[0])]
        out = _fft_inv_first_half(re, im)
        skip = skip_ref[:, lanes]
        for part in range(2):
            for a in range(len(out[part])):
                conv = out[part][a] + skip * u_ref[part, tile0_u + a, rows, lanes].astype(F32)
                gate = g_ref[part, tile0_g + a, rows, lanes].astype(F32)
                o_ref[part, a, rows, lanes] = (gate * conv).astype(o_ref.dtype)
    _slab_loop(ar_ref.shape[1], ar_ref.shape[2], slab)


def hy_stage_a_inv(ar, ai, u5, u_col0, tile0_u, g5, g_col0, tile0_g, skip, out_dtype):
    npair, n_a, rows, width = ar.shape
    n_out = n_a // 2
    rb, cb = 64, 256
    kern = functools.partial(_hy_stage_a_inv_kernel, tile0_u=tile0_u, tile0_g=tile0_g)
    aspec = pl.BlockSpec((None, n_a, rb, cb), lambda p, c, r: (p, 0, r, c))
    return pl.pallas_call(
        kern, grid=(npair, width // cb, rows // rb),
        in_specs=[aspec, aspec,
                  pl.BlockSpec((2, None, u5.shape[2], rb, cb), lambda p, c, r: (0, p, 0, r, u_col0 // cb + c)),
                  pl.BlockSpec((2, None, g5.shape[2], rb, cb), lambda p, c, r: (0, p, 0, r, g_col0 // cb + c)),
                  pl.BlockSpec((1, cb), lambda p, c, r: (0, c))],
        out_specs=pl.BlockSpec((2, None, n_out, rb, cb), lambda p, c, r: (0, p, 0, r, c)),
        out_shape=jax.ShapeDtypeStruct((2, npair, n_out, rows, width), out_dtype),
        compiler_params=_cparams(3), name="hyena_stage_a_inv")(ar, ai, u5, g5, skip.reshape(1, width))


def dft_matrices(n_a):
    n = n_a * DFT_ROWS
    kb = jnp.arange(DFT_ROWS, dtype=jnp.int32)[None, :, None]
    b = jnp.arange(DFT_ROWS, dtype=jnp.int32)[None, None, :]
    ka = jnp.arange(n_a, dtype=jnp.int32)[:, None, None]
    ang = (-2.0 * math.pi / n) * ((b * (ka + n_a * kb)) % n).astype(F32)
    fr, fi = jnp.cos(ang), jnp.sin(ang)
    blk = jnp.concatenate([jnp.concatenate([fr, -fi], axis=2), jnp.concatenate([fi, fr], axis=2)], axis=1)
    return blk.astype(BF16), (jnp.swapaxes(blk, 1, 2) * (1.0 / n)).astype(BF16)


def hyena_branch(u5, width, tile0, n_tiles, fargs, skip):
    length = n_tiles * DFT_ROWS
    n_a = 2 * n_tiles
    fmat, gmat = dft_matrices(n_a)
    hc = jnp.moveaxis(hyena_conv_filters(length, *fargs), 1, 0).reshape(HY_ORDER, n_a, DFT_ROWS, width)
    hr, hi = hy_stage_b(*hy_stage_a(jnp.stack([hc, jnp.zeros_like(hc)]), 0, width, 0, n_a, False), fmat)
    x5, x_col0, x_tile0 = u5, 0, tile0
    out = None
    for order in range(HY_ORDER):
        ar, ai = hy_stage_a(x5, x_col0, width, x_tile0, n_tiles, True)
        ar, ai = hy_stage_b(ar, ai, fmat, gmat, hr[order], hi[order])
        out = hy_stage_a_inv(ar, ai, x5, x_col0, x_tile0, u5, (order + 1) * width, tile0, skip[order], BF16)
        x5, x_col0, x_tile0 = out, 0, 0
    return out


def kernel(x, c, ctx, c_ctx, w_mod, b_mod, g_mix, g_ffn, w_in, b_gate, w_br, w_out, da_lambda, da_subln_g,
           lru_conv_w, lru_conv_b, lru_wa, lru_ba, lru_wi, lru_bi, lru_lambda, hy_conv_w, hy_conv_b,
           hy_f_w1, hy_f_b1, hy_f_w2, hy_f_b2, hy_f_freq, hy_f_w3, hy_decay, hy_skip, ffn_w_gate, ffn_w_up,
           ffn_w_down, moe_router, moe_w_gate, moe_w_up, moe_w_down, g_final):
    n_batch, length, d = x.shape
    n_ctx = ctx.shape[1]
    depth = w_mod.shape[0]
    assert n_ctx == ROW_G and length % ROW_G == 0 and d % 128 == 0
    tb = n_ctx + length
    gpb = tb // ROW_G
    nt = n_batch * tb
    c_end = w_in.shape[2]
    cb_k, cb_v, cb_lx, cb_q, cb_ly, cb_hy, cb_g = 0, 1, 2, 3, 4, 5, 8
    assert c_end == 11 * d

    xs = jnp.concatenate([ctx, x], axis=1).reshape(nt, d)
    silu_rows = jnp.concatenate([jax.nn.silu(c), jax.nn.silu(c_ctx)[None, :],
                                 jnp.zeros((16 - n_batch - 1, d), F32)], axis=0)
    tables = rope_tables(length, n_ctx, d)

    for li in range(depth):
        lam_init = 0.8 - 0.6 * math.exp(-0.3 * li)
        tab = mm_f32_bias(silu_rows, w_mod[li], b_mod[li], d)
        tab = tab.reshape(16, 6, d).transpose(1, 0, 2)

        h = norm_mod(xs, g_mix[li], tab, 0, 1, gpb, n_batch)
        tm = _pick(nt, (1024, 512, 256))
        p = mm_bf16(h, w_in[li].astype(BF16), BF16, tm, d, "in_proj")

        q, k = rope_qk(p, tables, n_batch, gpb, cb_q, cb_k, d)
        a_out = diff_attention(q, k, p, da_lambda[li], da_subln_g[li], lam_init, n_batch, gpb, cb_v)

        bw = d // LRU_BLOCKS
        r_out = None
        hf = None
        for direction in range(2):
            w_cat = jnp.concatenate([lru_wa[li, direction], lru_wi[li, direction]], axis=-1).astype(BF16)
            res = lru_pass(p, lru_conv_w[li], lru_conv_b[li], w_cat, lru_ba[li, direction],
                           lru_bi[li, direction], lru_lambda[li, direction], n_batch, gpb, cb_lx,
                           reverse=direction == 1, hf=hf, cb_y=cb_ly)
            if direction == 0:
                hf = res
            else:
                r_out = res
        del bw

        u = hyena_short_conv(p, hy_conv_w[li], hy_conv_b[li], n_batch, gpb, cb_hy, d)
        u5 = u.reshape(2, n_batch // 2, gpb, ROW_G, 3 * d)
        fargs = (hy_f_w1[li], hy_f_b1[li], hy_f_w2[li], hy_f_b2[li], hy_f_freq[li], hy_f_w3[li], hy_decay[li])
        y_c = hyena_branch(u5, d, 0, 1, fargs, hy_skip[li])
        y_l = hyena_branch(u5, d, 1, gpb - 1, fargs, hy_skip[li])
        y_out = jnp.concatenate([y_c, y_l], axis=2).reshape(nt, d)

        xs = merge_branches(xs, a_out, r_out, y_out, p, b_gate[li], w_br[li].astype(BF16),
                            w_out[li].astype(BF16), tab, 2, gpb, n_batch, cb_g)

        jj = li // 2
        if li % 2 == 0:
            f = norm_mod(xs, g_ffn[li], tab, 3, 4, gpb, n_batch)
            hh = gate_up(f, ffn_w_gate[jj].astype(BF16), ffn_w_up[jj].astype(BF16))
            xs = down_residual(hh, ffn_w_down[jj].astype(BF16), xs, tab, 5, gpb, n_batch)
        else:
            wr = jnp.concatenate([moe_router[jj], jnp.zeros((d, 128 - N_EXPERTS), F32)], axis=1)
            f, logits = norm_mod(xs, g_ffn[li], tab, 3, 4, gpb, n_batch, w_router=wr)
            xs = moe_layer(f, logits, moe_w_gate[jj].astype(BF16), moe_w_up[jj].astype(BF16),
                           moe_w_down[jj].astype(BF16), xs, tab, 5, gpb, n_batch)

    return final_norm(xs.reshape(n_batch, tb, d), g_final, n_ctx)
```

```python
import functools
import math

import jax
import jax.numpy as jnp
from jax import lax
from jax.experimental import pallas as pl
from jax.experimental.pallas import tpu as pltpu

F32 = jnp.float32
BF16 = jnp.bfloat16

EPS = 1e-6
ROW_G = 256
GRID_W = 64
ROPE_BASE = 10000.0
DA_HEADS = 8
DA_HEAD_DIM = 64
DA_V_DIM = 128
LRU_BLOCKS = 8
LRU_C = 8.0
HY_ORDER = 2
HY_BANDS = 16
N_EXPERTS = 8
TOP_K = 2
MOE_BLOCK = 256
HALO = 16
VMEM_LIMIT = 56 * 1024 * 1024


def _cparams(n_axes):
    return pltpu.CompilerParams(dimension_semantics=("arbitrary",) * n_axes,
                                vmem_limit_bytes=VMEM_LIMIT)


def _pick(n, prefs):
    for p in prefs:
        if n % p == 0:
            return p
    raise ValueError(f"no tile in {prefs} divides {n}")


def _mod_row(tab_ref, k, group, groups_per_batch, n_batch):
    b = group // groups_per_batch
    row = jnp.where(group % groups_per_batch == 0, n_batch, b)
    return tab_ref[k, pl.ds(row, 1), :]


def _mm_f32_kernel(a_ref, w_ref, b_ref, o_ref):
    o_ref[...] = jnp.dot(a_ref[...], w_ref[...], preferred_element_type=F32) + b_ref[...]


def mm_f32_bias(a, w, b, tn):
    m, k = a.shape
    n = w.shape[1]
    return pl.pallas_call(
        _mm_f32_kernel, grid=(n // tn,),
        in_specs=[pl.BlockSpec((m, k), lambda j: (0, 0)),
                  pl.BlockSpec((k, tn), lambda j: (0, j)),
                  pl.BlockSpec((1, tn), lambda j: (0, j))],
        out_specs=pl.BlockSpec((m, tn), lambda j: (0, j)),
        out_shape=jax.ShapeDtypeStruct((m, n), F32),
        compiler_params=_cparams(1), name="mod_matmul")(a, w, b.reshape(1, n))


def _norm_mod_kernel(x_ref, g_ref, tab_ref, *rest, k_shift, k_scale, gpb, n_batch, router):
    if router:
        wr_ref, o_ref, lg_ref = rest
    else:
        (o_ref,) = rest
    nsub = x_ref.shape[0] // ROW_G
    for s in range(nsub):
        grp = pl.program_id(0) * nsub + s
        rows = pl.ds(s * ROW_G, ROW_G)
        xs = x_ref[rows, :]
        y = xs * lax.rsqrt(jnp.mean(xs * xs, axis=-1, keepdims=True) + EPS) * g_ref[...]
        shift = _mod_row(tab_ref, k_shift, grp, gpb, n_batch)
        scale = _mod_row(tab_ref, k_scale, grp, gpb, n_batch)
        h = y * (1.0 + scale) + shift
        o_ref[rows, :] = h.astype(o_ref.dtype)
        if router:
            lg_ref[rows, :] = jnp.dot(h, wr_ref[...], preferred_element_type=F32,
                                      precision=lax.Precision.HIGHEST)


def norm_mod(x, g, tab, k_shift, k_scale, gpb, n_batch, w_router=None):
    nt, d = x.shape
    tm = _pick(nt, (1024, 512, 256))
    router = w_router is not None
    kern = functools.partial(_norm_mod_kernel, k_shift=k_shift, k_scale=k_scale, gpb=gpb,
                             n_batch=n_batch, router=router)
    in_specs = [pl.BlockSpec((tm, d), lambda i: (i, 0)),
                pl.BlockSpec((1, d), lambda i: (0, 0)),
                pl.BlockSpec(tab.shape, lambda i: (0, 0, 0))]
    out_specs = [pl.BlockSpec((tm, d), lambda i: (i, 0))]
    out_shape = [jax.ShapeDtypeStruct((nt, d), BF16)]
    args = [x, g.reshape(1, d), tab]
    if router:
        in_specs.append(pl.BlockSpec(w_router.shape, lambda i: (0, 0)))
        out_specs.append(pl.BlockSpec((tm, w_router.shape[1]), lambda i: (i, 0)))
        out_shape.append(jax.ShapeDtypeStruct((nt, w_router.shape[1]), F32))
        args.append(w_router)
    out = pl.pallas_call(kern, grid=(nt // tm,), in_specs=in_specs, out_specs=out_specs,
                         out_shape=out_shape, compiler_params=_cparams(1), name="norm_mod")(*args)
    return out if router else out[0]


def _final_norm_kernel(x_ref, g_ref, o_ref):
    xs = x_ref[...]
    o_ref[...] = xs * lax.rsqrt(jnp.mean(xs * xs, axis=-1, keepdims=True) + EPS) * g_ref[...]


def final_norm(x3, g, n_ctx):
    b, tb, d = x3.shape
    length = tb - n_ctx
    tm = ROW_G
    off = n_ctx // tm
    return pl.pallas_call(
        _final_norm_kernel, grid=(b, length // tm),
        in_specs=[pl.BlockSpec((None, tm, d), lambda i, j: (i, j + off, 0)),
                  pl.BlockSpec((1, d), lambda i, j: (0, 0))],
        out_specs=pl.BlockSpec((None, tm, d), lambda i, j: (i, j, 0)),
        out_shape=jax.ShapeDtypeStruct((b, length, d), F32),
        compiler_params=_cparams(2), name="final_norm")(x3, g.reshape(1, d))


ROPE_LANES = 128


def _in_proj_kernel(a_ref, w_ref, cos_ref, sa_ref, sb_ref, o_ref, *, cb_k, cb_q, q_scale, gpb):
    n = pl.program_id(0)
    is_rot = jnp.logical_or(n == cb_k, n == cb_q)

    @pl.when(jnp.logical_not(is_rot))
    def _():
        o_ref[...] = jnp.dot(a_ref[...], w_ref[...], preferred_element_type=F32).astype(o_ref.dtype)

    @pl.when(is_rot)
    def _():
        scale = jnp.where(n == cb_q, q_scale, 1.0)
        nsub = a_ref.shape[0] // ROW_G
        for s in range(nsub):
            rows = slice(s * ROW_G, (s + 1) * ROW_G)
            acc = jnp.dot(a_ref[rows, :], w_ref[...], preferred_element_type=F32)
            j = (pl.program_id(1) * nsub + s) % gpb
            trow = pl.ds(pl.multiple_of(j * ROW_G, ROW_G), ROW_G)
            cos, sa, sb = cos_ref[trow, :] * scale, sa_ref[trow, :] * scale, sb_ref[trow, :] * scale
            for c in range(acc.shape[1] // ROPE_LANES):
                t = acc[:, c * ROPE_LANES:(c + 1) * ROPE_LANES]
                rot = (t * cos + pltpu.roll(t, ROPE_LANES - 16, 1) * sa + pltpu.roll(t, 16, 1) * sb)
                o_ref[rows, c * ROPE_LANES:(c + 1) * ROPE_LANES] = rot.astype(o_ref.dtype)


def in_proj(h, w, tables, gpb, cb_k, cb_q):
    m, k = h.shape
    n = w.shape[1]
    tm = _pick(m, (1024, 512, 256))
    tn = k
    cos, sa, sb = tables
    kern = functools.partial(_in_proj_kernel, cb_k=cb_k, cb_q=cb_q, gpb=gpb,
                             q_scale=math.log2(math.e) * DA_HEAD_DIM ** -0.5)
    tspec = pl.BlockSpec(cos.shape, lambda j, i: (0, 0))
    return pl.pallas_call(
        kern, grid=(n // tn, m // tm),
        in_specs=[pl.BlockSpec((tm, k), lambda j, i: (i, 0)),
                  pl.BlockSpec((k, tn), lambda j, i: (0, j)), tspec, tspec, tspec],
        out_specs=pl.BlockSpec((tm, tn), lambda j, i: (i, j)),
        out_shape=jax.ShapeDtypeStruct((m, n), BF16),
        compiler_params=_cparams(2), name="in_proj")(h, w, cos, sa, sb)


def rope_tables(length, n_ctx, width):
    rows = length // GRID_W
    row = jnp.repeat(jnp.arange(rows, dtype=F32), GRID_W)
    col = jnp.tile(jnp.arange(GRID_W, dtype=F32), rows)
    n_freq = DA_HEAD_DIM // 4
    inv = ROPE_BASE ** (-jnp.arange(n_freq, dtype=F32) / n_freq)
    ar = row[:, None] * inv
    ac = col[:, None] * inv
    ang = jnp.concatenate([ar, ar, ac, ac], axis=-1)
    ang = jnp.concatenate([jnp.zeros((n_ctx, DA_HEAD_DIM), F32), ang], axis=0)
    reps = width // DA_HEAD_DIM
    cos = jnp.tile(jnp.cos(ang), (1, reps))
    sin = jnp.tile(jnp.sin(ang), (1, reps))
    first = (jnp.arange(width) % (2 * n_freq)) < n_freq
    sa = jnp.where(first[None, :], -sin, 0.0)
    sb = jnp.where(first[None, :], 0.0, sin)
    return cos, sa, sb


def _attn_kernel(lamv_ref, g_ref, q_ref, k_ref, v_ref, o_ref, vaug_ref, m_ref, acc_ref,
                 *, lam_init, n_ctx, tk):
    j = pl.program_id(2)
    dv = v_ref.shape[1]

    @pl.when(j == 0)
    def _():
        vaug_ref[:, :dv] = v_ref[...]
        vaug_ref[:, dv:] = jnp.ones((vaug_ref.shape[0], vaug_ref.shape[1] - dv), vaug_ref.dtype)

    q = q_ref[...]
    lane = lax.broadcasted_iota(jnp.int32, q.shape, 1)
    zero = jnp.zeros_like(q)
    qq = jnp.concatenate([jnp.where(lane < DA_HEAD_DIM, q, zero),
                          jnp.where(lane >= DA_HEAD_DIM, q, zero)], axis=0)
    tq = q.shape[0]

    def scores(start, size):
        kc = k_ref[pl.ds(start, size), :]
        return lax.dot_general(qq, kc, (((1,), (1,)), ((), ())), preferred_element_type=F32)

    def update(s, start, size, first=False):
        m_new = jnp.max(s, axis=-1, keepdims=True)
        if first:
            m_new = jnp.broadcast_to(m_new, m_ref.shape)
        else:
            m_prev = m_ref[...]
            m_new = jnp.maximum(m_prev, m_new)
        p = jnp.exp2(s - jnp.concatenate([m_new] * (size // 128), axis=1))
        pv = jnp.dot(p.astype(BF16), vaug_ref[pl.ds(start, size), :], preferred_element_type=F32)
        if first:
            acc_ref[...] = pv
        else:
            alpha = jnp.exp2(m_prev - m_new)
            acc_ref[...] = jnp.concatenate([alpha] * (acc_ref.shape[1] // 128), axis=1) * acc_ref[...] + pv
        m_ref[...] = m_new

    s_ctx = scores(0, n_ctx)

    @pl.when(j == 0)
    def _():
        update(s_ctx, 0, n_ctx, first=True)

    @pl.when(j > 0)
    def _():
        starts = [n_ctx + i * tk for i in range((k_ref.shape[0] - n_ctx) // tk)]
        s_next = scores(starts[0], tk)
        update(s_ctx, 0, n_ctx, first=True)
        for i, start in enumerate(starts):
            s_cur = s_next
            if i + 1 < len(starts):
                s_next = scores(starts[i + 1], tk)
            update(s_cur, start, tk)

    lv = lamv_ref[...]
    lam = (jnp.exp(jnp.sum(lv[0:1] * lv[1:2], axis=-1, keepdims=True))
           - jnp.exp(jnp.sum(lv[2:3] * lv[3:4], axis=-1, keepdims=True)) + lam_init)
    o = acc_ref[:, :dv] / acc_ref[:, dv:]
    o = o[:tq] - lam * o[tq:]
    o = o * lax.rsqrt(jnp.mean(o * o, axis=-1, keepdims=True) + EPS) * g_ref[...]
    o_ref[...] = (o * (1.0 - lam_init)).astype(o_ref.dtype)


def diff_attention(p, width, lam_vecs, subln_g, lam_init, n_batch, gpb, cb_q, cb_k, cb_v):
    nt = p.shape[0]
    tb = gpb * ROW_G
    n_ctx = ROW_G
    tk = _pick(tb - n_ctx, (2048, 1024, 512, 256))
    kern = functools.partial(_attn_kernel, lam_init=lam_init, n_ctx=n_ctx, tk=tk)
    assert DA_V_DIM == 128
    hpb = width // DA_V_DIM
    return pl.pallas_call(
        kern, grid=(n_batch, DA_HEADS, gpb),
        in_specs=[pl.BlockSpec(lam_vecs.shape, lambda b, h, j: (0, 0)),
                  pl.BlockSpec((1, DA_V_DIM), lambda b, h, j: (0, 0)),
                  pl.BlockSpec((ROW_G, DA_V_DIM), lambda b, h, j: (b * gpb + j, cb_q * hpb + h)),
                  pl.BlockSpec((tb, DA_V_DIM), lambda b, h, j: (b, cb_k * hpb + h)),
                  pl.BlockSpec((tb, DA_V_DIM), lambda b, h, j: (b, cb_v * hpb + h))],
        out_specs=pl.BlockSpec((ROW_G, DA_V_DIM), lambda b, h, j: (b * gpb + j, h)),
        out_shape=jax.ShapeDtypeStruct((nt, width), BF16),
        scratch_shapes=[pltpu.VMEM((tb, 2 * DA_V_DIM), BF16), pltpu.VMEM((2 * ROW_G, 128), F32),
                        pltpu.VMEM((2 * ROW_G, 2 * DA_V_DIM), F32)],
        compiler_params=_cparams(3), name="diff_attn")(
            lam_vecs, subln_g.reshape(1, DA_V_DIM), p, p, p)


def _shift_matrix(n, d):
    row = lax.broadcasted_iota(jnp.int32, (n, n), 0)
    col = lax.broadcasted_iota(jnp.int32, (n, n), 1)
    return jnp.where(col == row + d, 1.0, 0.0).astype(BF16)


def _dwconv(main_ref, prev_ref, next_ref, w_ref, b_ref, left, has_prev, has_next):
    x = main_ref[...]
    acc = b_ref[...] + w_ref[left:left + 1, :] * x.astype(F32)
    prev8 = prev_ref[...].astype(F32)[HALO - 8:, :]
    next8 = next_ref[...].astype(F32)[:8, :]
    sub = lax.broadcasted_iota(jnp.int32, (8, 1), 0)
    first = jnp.zeros_like(prev8)
    last = jnp.zeros_like(next8)
    for j in range(w_ref.shape[0]):
        d = j - left
        if d == 0:
            continue
        wj = w_ref[j:j + 1, :]
        acc = acc + wj * jnp.dot(_shift_matrix(ROW_G, d), x, preferred_element_type=F32)
        if d < 0:
            mask = jnp.logical_and(sub < -d, has_prev)
            first = first + wj * jnp.where(mask, pltpu.roll(prev8, -d, 0), 0.0)
        else:
            mask = jnp.logical_and(sub >= 8 - d, has_next)
            last = last + wj * jnp.where(mask, pltpu.roll(next8, 8 - d, 0), 0.0)
    return jnp.concatenate([acc[0:8] + first, acc[8:ROW_G - 8], acc[ROW_G - 8:] + last], axis=0)


def _halo_specs(width, cb, gpb, tile_of, n_groups):
    per = ROW_G // HALO

    def main(b, s, *_):
        return (b * gpb + tile_of(s), cb)

    def prev(b, s, *_):
        return (jnp.maximum((b * gpb + tile_of(s)) * per - 1, 0), cb)

    def nxt(b, s, *_):
        return (jnp.minimum((b * gpb + tile_of(s) + 1) * per, n_groups * per - 1), cb)

    return [pl.BlockSpec((ROW_G, width), main), pl.BlockSpec((HALO, width), prev),
            pl.BlockSpec((HALO, width), nxt)]


def _seq_flags(j, gpb):
    return j >= 2, jnp.logical_and(j >= 1, j <= gpb - 2)


SCAN_G = 8


def _scan_groups(a, b, reverse):
    shape = a.shape
    a = a.reshape(shape[0] // SCAN_G, SCAN_G, shape[1])
    b = b.reshape(a.shape)
    pos = lax.broadcasted_iota(jnp.int32, a.shape, 1)
    s = 1
    while s < SCAN_G:
        keep = pos < SCAN_G - s if reverse else pos >= s
        shift = SCAN_G - s if reverse else s
        a_sh = jnp.where(keep, pltpu.roll(a, shift, 1), 1.0)
        b_sh = jnp.where(keep, pltpu.roll(b, shift, 1), 0.0)
        b = a * b_sh + b
        a = a * a_sh
        s *= 2
    return a.reshape(shape), b.reshape(shape)


def _gelu_tanh(x):
    return 0.5 * x * (1.0 + jnp.tanh(math.sqrt(2.0 / math.pi) * (x + 0.044715 * (x * x * x))))


def _lru_kernel(main_ref, prev_ref, next_ref, cw_ref, cb_ref, w_ref, ba_ref, bi_ref, lam_ref, *rest,
                gpb, reverse):
    if reverse:
        hf_ref, ly_ref, o_ref, carry_ref = rest
    else:
        o_ref, carry_ref = rest
    s = pl.program_id(1)
    j = jnp.where(s == 0, 0, gpb - s) if reverse else s
    has_prev, has_next = _seq_flags(j, gpb)
    xc = _dwconv(main_ref, prev_ref, next_ref, cw_ref, cb_ref, 2, has_prev, has_next)
    xcb = xc.astype(BF16)
    bw = w_ref.shape[1]
    r_parts, i_parts = [], []
    for n in range(w_ref.shape[0]):
        res = jnp.dot(xcb[:, n * bw:(n + 1) * bw], w_ref[n], preferred_element_type=F32)
        r_parts.append(res[:, :bw])
        i_parts.append(res[:, bw:])
    r = jax.nn.sigmoid(jnp.concatenate(r_parts, axis=1) + ba_ref[...])
    gate_i = jax.nn.sigmoid(jnp.concatenate(i_parts, axis=1) + bi_ref[...])
    nl = -lam_ref[...]
    softplus = jnp.maximum(nl, 0.0) + jnp.log(1.0 + jnp.exp(-jnp.abs(nl)))
    a = jnp.exp((-LRU_C) * r * softplus)
    bb = jnp.sqrt(1.0 - a * a) * (gate_i * xc)
    a_grp, h_grp = _scan_groups(a, bb, reverse)

    @pl.when(s == 0)
    def _():
        carry_ref[...] = jnp.zeros(carry_ref.shape, F32)

    h = carry_ref[...]
    n_grp = ROW_G // SCAN_G
    pieces = [None] * n_grp
    for g in (reversed(range(n_grp)) if reverse else range(n_grp)):
        rows = slice(g * SCAN_G, (g + 1) * SCAN_G)
        hg = a_grp[rows] * h + h_grp[rows]
        pieces[g] = hg
        h = hg[0:1, :] if reverse else hg[SCAN_G - 1:SCAN_G, :]
    carry_ref[...] = h
    h_all = jnp.concatenate(pieces, axis=0)
    if reverse:
        ly = ly_ref[...].astype(F32)
        o_ref[...] = (_gelu_tanh(ly) * (hf_ref[...] + h_all)).astype(o_ref.dtype)
    else:
        o_ref[...] = h_all


def lru_pass(p, conv_w, conv_b, w_cat, ba, bi, lam, n_batch, gpb, cb_x, reverse, hf=None, cb_y=None):
    nt = p.shape[0]
    width = conv_w.shape[1]
    tile_of = (lambda s: jnp.where(s == 0, 0, gpb - s)) if reverse else (lambda s: s)
    kern = functools.partial(_lru_kernel, gpb=gpb, reverse=reverse)
    const2 = lambda b, s: (0, 0)
    in_specs = _halo_specs(width, cb_x, gpb, tile_of, nt // ROW_G) + [
        pl.BlockSpec(conv_w.shape, const2), pl.BlockSpec((1, width), const2),
        pl.BlockSpec(w_cat.shape, lambda b, s: (0, 0, 0)),
        pl.BlockSpec((1, width), const2), pl.BlockSpec((1, width), const2),
        pl.BlockSpec((1, width), const2)]
    args = [p, p, p, conv_w, conv_b.reshape(1, width), w_cat, ba.reshape(1, width),
            bi.reshape(1, width), lam.reshape(1, width)]
    row_spec = lambda cb: pl.BlockSpec((ROW_G, width), lambda b, s: (b * gpb + tile_of(s), cb))
    if reverse:
        in_specs += [row_spec(0), row_spec(cb_y)]
        args += [hf, p]
        out_dtype = BF16
    else:
        out_dtype = F32
    return pl.pallas_call(
        kern, grid=(n_batch, gpb), in_specs=in_specs, out_specs=row_spec(0),
        out_shape=jax.ShapeDtypeStruct((nt, width), out_dtype),
        scratch_shapes=[pltpu.VMEM((1, width), F32)],
        compiler_params=_cparams(2), name="lru_bwd" if reverse else "lru_fwd")(*args)


def _short_conv_kernel(main_ref, prev_ref, next_ref, cw_ref, cb_ref, o_ref, *, gpb):
    j = pl.program_id(1)
    has_prev, has_next = _seq_flags(j, gpb)
    o_ref[...] = _dwconv(main_ref, prev_ref, next_ref, cw_ref, cb_ref, 1, has_prev, has_next).astype(o_ref.dtype)


def hyena_short_conv(p, conv_w, conv_b, n_batch, gpb, cb0, width):
    nt = p.shape[0]
    total = conv_w.shape[1]
    ncb = total // width
    base = _halo_specs(width, 0, gpb, lambda s: s, nt // ROW_G)

    def shifted(spec):
        f = spec.index_map
        return pl.BlockSpec(spec.block_shape, lambda b, j, c: (f(b, j)[0], cb0 + c))

    kern = functools.partial(_short_conv_kernel, gpb=gpb)
    return pl.pallas_call(
        kern, grid=(n_batch, gpb, ncb),
        in_specs=[shifted(sp) for sp in base] + [
            pl.BlockSpec((conv_w.shape[0], width), lambda b, j, c: (0, c)),
            pl.BlockSpec((1, width), lambda b, j, c: (0, c))],
        out_specs=pl.BlockSpec((ROW_G, width), lambda b, j, c: (b * gpb + j, c)),
        out_shape=jax.ShapeDtypeStruct((nt, total), BF16),
        compiler_params=_cparams(3), name="hyena_short_conv")(p, p, p, conv_w, conv_b.reshape(1, total))


def _merge_kernel(a_ref, r_ref, y_ref, g0_ref, g1_ref, g2_ref, bg_ref, wbr_ref, wout_ref, tab_ref, x_ref,
                  o_ref, *, k_gate, gpb, n_batch):
    width = a_ref.shape[1]
    m = None
    for k, (br_ref, gt_ref) in enumerate(((a_ref, g0_ref), (r_ref, g1_ref), (y_ref, g2_ref))):
        gate = jax.nn.sigmoid(gt_ref[...].astype(F32) + bg_ref[:, k * width:(k + 1) * width])
        term = gate * jnp.dot(br_ref[...], wbr_ref[k], preferred_element_type=F32)
        m = term if m is None else m + term
    out = jnp.dot(m.astype(BF16), wout_ref[...], preferred_element_type=F32)
    nsub = x_ref.shape[0] // ROW_G
    for s in range(nsub):
        rows = pl.ds(s * ROW_G, ROW_G)
        gmod = _mod_row(tab_ref, k_gate, pl.program_id(0) * nsub + s, gpb, n_batch)
        o_ref[rows, :] = x_ref[rows, :] + gmod * out[s * ROW_G:(s + 1) * ROW_G, :]


def merge_branches(x, a, r, y, p, b_gate, w_br, w_out, tab, k_gate, gpb, n_batch, cb_g):
    nt, d = x.shape
    tm = _pick(nt, (512, 256))
    kern = functools.partial(_merge_kernel, k_gate=k_gate, gpb=gpb, n_batch=n_batch)
    row = pl.BlockSpec((tm, d), lambda i: (i, 0))
    gspec = lambda k: pl.BlockSpec((tm, d), lambda i: (i, cb_g + k))
    return pl.pallas_call(
        kern, grid=(nt // tm,),
        in_specs=[row, row, row, gspec(0), gspec(1), gspec(2),
                  pl.BlockSpec((1, 3 * d), lambda i: (0, 0)),
                  pl.BlockSpec(w_br.shape, lambda i: (0, 0, 0)),
                  pl.BlockSpec(w_out.shape, lambda i: (0, 0)),
                  pl.BlockSpec(tab.shape, lambda i: (0, 0, 0)), row],
        out_specs=row, out_shape=jax.ShapeDtypeStruct((nt, d), F32),
        compiler_params=_cparams(1), name="merge")(a, r, y, p, p, p, b_gate.reshape(1, 3 * d), w_br, w_out, tab, x)


def _gate_up_kernel(*refs, expert):
    if expert:
        _, f_ref, wg_ref, wu_ref, o_ref = refs
    else:
        f_ref, wg_ref, wu_ref, o_ref = refs
    f = f_ref[...]
    g = jnp.dot(f, wg_ref[...], preferred_element_type=F32)
    u = jnp.dot(f, wu_ref[...], preferred_element_type=F32)
    o_ref[...] = (g * jax.nn.sigmoid(g) * u).astype(o_ref.dtype)


def gate_up(f, wg, wu, block_e=None):
    m, d = f.shape
    dff = wg.shape[-1]
    tn = _pick(dff, (1408, 1024, 512, 256, 128))
    if block_e is None:
        tm = _pick(m, (512, 256))
        grid_spec = pltpu.PrefetchScalarGridSpec(
            num_scalar_prefetch=0, grid=(dff // tn, m // tm),
            in_specs=[pl.BlockSpec((tm, d), lambda j, i: (i, 0)),
                      pl.BlockSpec((d, tn), lambda j, i: (0, j)),
                      pl.BlockSpec((d, tn), lambda j, i: (0, j))],
            out_specs=pl.BlockSpec((tm, tn), lambda j, i: (i, j)))
        args = (f, wg, wu)
    else:
        tm = MOE_BLOCK
        grid_spec = pltpu.PrefetchScalarGridSpec(
            num_scalar_prefetch=1, grid=(dff // tn, m // tm),
            in_specs=[pl.BlockSpec((tm, d), lambda j, i, be: (i, 0)),
                      pl.BlockSpec((None, d, tn), lambda j, i, be: (be[i], 0, j)),
                      pl.BlockSpec((None, d, tn), lambda j, i, be: (be[i], 0, j))],
            out_specs=pl.BlockSpec((tm, tn), lambda j, i, be: (i, j)))
        args = (block_e, f, wg, wu)
    return pl.pallas_call(
        functools.partial(_gate_up_kernel, expert=block_e is not None), grid_spec=grid_spec,
        out_shape=jax.ShapeDtypeStruct((m, dff), BF16),
        compiler_params=_cparams(2), name="gate_up")(*args)


def _down_res_kernel(h_ref, wd_ref, tab_ref, x_ref, o_ref, *, k_gate, gpb, n_batch):
    out = jnp.dot(h_ref[...], wd_ref[...], preferred_element_type=F32)
    nsub = x_ref.shape[0] // ROW_G
    for s in range(nsub):
        rows = pl.ds(s * ROW_G, ROW_G)
        gmod = _mod_row(tab_ref, k_gate, pl.program_id(0) * nsub + s, gpb, n_batch)
        o_ref[rows, :] = x_ref[rows, :] + gmod * out[s * ROW_G:(s + 1) * ROW_G, :]


def down_residual(h, wd, x, tab, k_gate, gpb, n_batch):
    nt, d = x.shape
    dff = h.shape[1]
    tm = _pick(nt, (512, 256))
    kern = functools.partial(_down_res_kernel, k_gate=k_gate, gpb=gpb, n_batch=n_batch)
    return pl.pallas_call(
        kern, grid=(nt // tm,),
        in_specs=[pl.BlockSpec((tm, dff), lambda i: (i, 0)),
                  pl.BlockSpec((dff, d), lambda i: (0, 0)),
                  pl.BlockSpec(tab.shape, lambda i: (0, 0, 0)),
                  pl.BlockSpec((tm, d), lambda i: (i, 0))],
        out_specs=pl.BlockSpec((tm, d), lambda i: (i, 0)),
        out_shape=jax.ShapeDtypeStruct((nt, d), F32),
        compiler_params=_cparams(1), name="down_residual")(h, wd, tab, x)


def _expert_down_kernel(be_ref, h_ref, wd_ref, o_ref):
    o_ref[...] = jnp.dot(h_ref[...], wd_ref[...], preferred_element_type=F32)


def expert_down(h, wd, block_e):
    m, dff = h.shape
    d = wd.shape[-1]
    grid_spec = pltpu.PrefetchScalarGridSpec(
        num_scalar_prefetch=1, grid=(m // MOE_BLOCK,),
        in_specs=[pl.BlockSpec((MOE_BLOCK, dff), lambda i, be: (i, 0)),
                  pl.BlockSpec((None, dff, d), lambda i, be: (be[i], 0, 0))],
        out_specs=pl.BlockSpec((MOE_BLOCK, d), lambda i, be: (i, 0)))
    return pl.pallas_call(_expert_down_kernel, grid_spec=grid_spec,
                          out_shape=jax.ShapeDtypeStruct((m, d), F32),
                          compiler_params=_cparams(1), name="expert_down")(block_e, h, wd)


def _combine_kernel(y0_ref, y1_ref, w0_ref, w1_ref, tab_ref, x_ref, o_ref, *, k_gate, gpb, n_batch):
    nsub = x_ref.shape[0] // ROW_G
    for s in range(nsub):
        rows = pl.ds(s * ROW_G, ROW_G)
        gmod = _mod_row(tab_ref, k_gate, pl.program_id(0) * nsub + s, gpb, n_batch)
        y = y0_ref[rows, :] * w0_ref[rows, :] + y1_ref[rows, :] * w1_ref[rows, :]
        o_ref[rows, :] = x_ref[rows, :] + gmod * y


def moe_combine(y0, y1, w0, w1, x, tab, k_gate, gpb, n_batch):
    nt, d = x.shape
    tm = _pick(nt, (512, 256))
    kern = functools.partial(_combine_kernel, k_gate=k_gate, gpb=gpb, n_batch=n_batch)
    row = pl.BlockSpec((tm, d), lambda i: (i, 0))
    wsp = pl.BlockSpec((tm, 1), lambda i: (i, 0))
    return pl.pallas_call(
        kern, grid=(nt // tm,),
        in_specs=[row, row, wsp, wsp, pl.BlockSpec(tab.shape, lambda i: (0, 0, 0)), row],
        out_specs=row, out_shape=jax.ShapeDtypeStruct((nt, d), F32),
        compiler_params=_cparams(1), name="moe_combine")(y0, y1, w0, w1, tab, x)


def moe_layer(f, logits, w_gate, w_up, w_down, x, tab, k_gate, gpb, n_batch):
    nt = f.shape[0]
    n_asg = nt * TOP_K
    top_v, top_i = lax.top_k(logits[:, :N_EXPERTS], TOP_K)
    top_w = jax.nn.softmax(top_v, axis=-1)
    e_flat = top_i.reshape(-1)
    onehot = (e_flat[:, None] == jnp.arange(N_EXPERTS, dtype=e_flat.dtype)[None, :]).astype(jnp.int32)
    csum = jnp.cumsum(onehot, axis=0)
    counts = csum[-1]
    rank = jnp.take_along_axis(csum, e_flat[:, None], axis=1)[:, 0] - 1
    padded = ((counts + MOE_BLOCK - 1) // MOE_BLOCK) * MOE_BLOCK
    pad_end = jnp.cumsum(padded)
    pad_start = pad_end - padded
    dest = (pad_start[e_flat] + rank).astype(jnp.int32)
    n_blocks = -(-n_asg // MOE_BLOCK) + N_EXPERTS
    n_slots = n_blocks * MOE_BLOCK
    tok_flat = jnp.repeat(jnp.arange(nt, dtype=jnp.int32), TOP_K)
    slot_tok = jnp.zeros((n_slots,), jnp.int32).at[dest].set(tok_flat)
    block_e = jnp.minimum(jnp.searchsorted(pad_end, jnp.arange(n_blocks, dtype=jnp.int32) * MOE_BLOCK,
                                           side='right'), N_EXPERTS - 1).astype(jnp.int32)
    xs = jnp.take(f, slot_tok, axis=0)
    h = gate_up(xs, w_gate, w_up, block_e)
    ys = expert_down(h, w_down, block_e)
    dest2 = dest.reshape(nt, TOP_K)
    y0 = jnp.take(ys, dest2[:, 0], axis=0)
    y1 = jnp.take(ys, dest2[:, 1], axis=0)
    return moe_combine(y0, y1, top_w[:, 0:1], top_w[:, 1:2], x, tab, k_gate, gpb, n_batch)


HY_PAD = 128


def _hy_filter_kernel(z_ref, t_ref, w1_ref, b1_ref, w2_ref, b2_ref, fr_ref, w3_ref, dec_ref, hc_ref, norm_ref,
                      *, length):
    i = pl.program_id(0)
    rows = z_ref.shape[0]
    width = dec_ref.shape[1]
    hp = lax.Precision.HIGHEST
    fr = fr_ref[...]
    hdn = jnp.sin(fr * (jnp.dot(z_ref[...], w1_ref[...], precision=hp, preferred_element_type=F32) + b1_ref[...]))
    hdn = jnp.sin(fr * (jnp.dot(hdn, w2_ref[...], precision=hp, preferred_element_type=F32) + b2_ref[...]))
    w3 = w3_ref[...]
    h_hi, w_hi = hdn.astype(BF16), w3.astype(BF16)
    h_lo = (hdn - h_hi.astype(F32)).astype(BF16)
    w_lo = (w3 - w_hi.astype(F32)).astype(BF16)
    filt = (jnp.dot(h_hi, w_hi, preferred_element_type=F32) + jnp.dot(h_hi, w_lo, preferred_element_type=F32)
            + jnp.dot(h_lo, w_hi, preferred_element_type=F32))
    t = t_ref[...]
    r = i * rows + lax.broadcasted_iota(jnp.int32, (rows, 1), 0)
    first_half = r < length

    @pl.when(i == 0)
    def _():
        norm_ref[...] = jnp.full(norm_ref.shape, EPS, F32)

    for o in range(HY_ORDER):
        win = jnp.exp(-t * jnp.abs(dec_ref[o:o + 1, :]))
        fwd = filt[:, (2 * o) * width:(2 * o + 1) * width] * win
        bwd = filt[:, (2 * o + 1) * width:(2 * o + 2) * width] * win
        mass = jnp.where(first_half, jnp.abs(fwd) + jnp.abs(bwd), 0.0)
        norm_ref[o:o + 1, :] += jnp.sum(mass, axis=0, keepdims=True)
        hc = jnp.where(first_half, fwd, bwd) + jnp.where(r == 0, bwd, 0.0)
        hc_ref[o] = jnp.where(r == length, 0.0, hc)


def hyena_conv_filters(length, w1, b1, w2, b2, freq, w3, decay):
    n = 2 * length
    width = decay.shape[-1]
    r = jnp.arange(n, dtype=jnp.int32)
    tap = jnp.where(r < length, r, jnp.where(r == length, 0, n - r))
    t = jnp.linspace(0.0, 1.0, length, dtype=F32)[tap][:, None]
    w = 2.0 * math.pi * tap.astype(F32)[:, None] / length
    bands = jnp.linspace(1e-4, HY_BANDS - 1, HY_BANDS, dtype=F32)
    z = jnp.concatenate([t, jnp.cos(bands * w), -jnp.sin(bands * w)], axis=-1)
    assert max(w1.shape) <= HY_PAD

    def pad_to(x, shape):
        return jnp.pad(x, [(0, s - d) for s, d in zip(shape, x.shape)])

    args = (pad_to(z, (n, HY_PAD)), t, pad_to(w1, (HY_PAD, HY_PAD)), pad_to(b1[None, :], (1, HY_PAD)),
            pad_to(w2, (HY_PAD, HY_PAD)), pad_to(b2[None, :], (1, HY_PAD)), pad_to(freq[None, :], (1, HY_PAD)),
            pad_to(w3, (HY_PAD, w3.shape[1])), decay)
    const = lambda i: (0, 0)
    return pl.pallas_call(
        functools.partial(_hy_filter_kernel, length=length), grid=(n // DFT_ROWS,),
        in_specs=[pl.BlockSpec((DFT_ROWS, HY_PAD), lambda i: (i, 0)), pl.BlockSpec((DFT_ROWS, 1), lambda i: (i, 0)),
                  pl.BlockSpec((HY_PAD, HY_PAD), const), pl.BlockSpec((1, HY_PAD), const),
                  pl.BlockSpec((HY_PAD, HY_PAD), const), pl.BlockSpec((1, HY_PAD), const),
                  pl.BlockSpec((1, HY_PAD), const), pl.BlockSpec((HY_PAD, w3.shape[1]), const),
                  pl.BlockSpec(decay.shape, const)],
        out_specs=[pl.BlockSpec((HY_ORDER, DFT_ROWS, width), lambda i: (0, i, 0)),
                   pl.BlockSpec((HY_ORDER, width), const)],
        out_shape=[jax.ShapeDtypeStruct((HY_ORDER, n, width), F32),
                   jax.ShapeDtypeStruct((HY_ORDER, width), F32)],
        compiler_params=_cparams(1), name="hyena_filter")(*args)


DFT_ROWS = ROW_G
SLAB = 16


def _cmul_const(ar, ai, c, s):
    tol = 1e-12
    if abs(s) < tol and abs(c - 1.0) < tol:
        return ar, ai
    if abs(s) < tol and abs(c + 1.0) < tol:
        return -ar, -ai
    if abs(c) < tol and abs(s - 1.0) < tol:
        return -ai, ar
    if abs(c) < tol and abs(s + 1.0) < tol:
        return ai, -ar
    return c * ar - s * ai, s * ar + c * ai


def _fft_list(re, im, sign):
    n = len(re)
    if n == 1:
        return list(re), list(im)
    er, ei = _fft_list(re[0::2], im[0::2], sign)
    qr, qi = _fft_list(re[1::2], im[1::2], sign)
    out_r, out_i = [None] * n, [None] * n
    for k in range(n // 2):
        ang = sign * 2.0 * math.pi * k / n
        tr, ti = _cmul_const(qr[k], qi[k], math.cos(ang), math.sin(ang))
        out_r[k], out_i[k] = er[k] + tr, ei[k] + ti
        out_r[k + n // 2], out_i[k + n // 2] = er[k] - tr, ei[k] - ti
    return out_r, out_i


def _fft_fwd_zero_padded(re, im):
    h = len(re)
    n = 2 * h
    er, ei = _fft_list(re, im, -1.0)
    tw = [_cmul_const(re[a], im[a], math.cos(-2.0 * math.pi * a / n), math.sin(-2.0 * math.pi * a / n))
          for a in range(h)]
    qr, qi = _fft_list([t[0] for t in tw], [t[1] for t in tw], -1.0)
    out_r, out_i = [None] * n, [None] * n
    for m in range(h):
        out_r[2 * m], out_i[2 * m] = er[m], ei[m]
        out_r[2 * m + 1], out_i[2 * m + 1] = qr[m], qi[m]
    return out_r, out_i


def _fft_inv_first_half(re, im):
    n = len(re)
    h = n // 2
    er, ei = _fft_list(re[0::2], im[0::2], 1.0)
    qr, qi = _fft_list(re[1::2], im[1::2], 1.0)
    out_r, out_i = [], []
    for a in range(h):
        tr, ti = _cmul_const(qr[a], qi[a], math.cos(2.0 * math.pi * a / n), math.sin(2.0 * math.pi * a / n))
        out_r.append(er[a] + tr)
        out_i.append(ei[a] + ti)
    return out_r, out_i


def _slab_loop(rows, width, fn):
    def body(i, carry):
        r0 = pl.multiple_of(i * SLAB, SLAB)
        for l0 in range(0, width, 128):
            fn(pl.ds(r0, SLAB), slice(l0, l0 + 128))
        return carry
    lax.fori_loop(0, rows // SLAB, body, 0)


def _hy_stage_a_kernel(x_ref, ar_ref, ai_ref, *, tile0, n_in, zero_padded):
    def slab(rows, lanes):
        re = [x_ref[0, tile0 + a, rows, lanes].astype(F32) for a in range(n_in)]
        im = [x_ref[1, tile0 + a, rows, lanes].astype(F32) for a in range(n_in)]
        out_r, out_i = _fft_fwd_zero_padded(re, im) if zero_padded else _fft_list(re, im, -1.0)
        for k in range(len(out_r)):
            ar_ref[k, rows, lanes] = out_r[k].astype(ar_ref.dtype)
            ai_ref[k, rows, lanes] = out_i[k].astype(ai_ref.dtype)
    _slab_loop(ar_ref.shape[1], ar_ref.shape[2], slab)


def hy_stage_a(x5, col0, width, tile0, n_in, zero_padded):
    _, npair, ntile, rows, _ = x5.shape
    n_a = 2 * n_in if zero_padded else n_in
    rb, cb = 64, 256
    kern = functools.partial(_hy_stage_a_kernel, tile0=tile0, n_in=n_in, zero_padded=zero_padded)
    ospec = pl.BlockSpec((None, n_a, rb, cb), lambda p, c, r: (p, 0, r, c))
    return pl.pallas_call(
        kern, grid=(npair, width // cb, rows // rb),
        in_specs=[pl.BlockSpec((2, None, ntile, rb, cb), lambda p, c, r: (0, p, 0, r, col0 // cb + c))],
        out_specs=[ospec, ospec],
        out_shape=[jax.ShapeDtypeStruct((npair, n_a, rows, width), BF16)] * 2,
        compiler_params=_cparams(3), name="hyena_stage_a")(x5)


def _hy_stage_b_kernel(ar_ref, ai_ref, f_ref, *rest, spectrum_only):
    half = ar_ref.shape[0]
    a = jnp.concatenate([ar_ref[...], ai_ref[...]], axis=0)
    x = jnp.dot(f_ref[...], a, preferred_element_type=F32)
    if spectrum_only:
        norm_ref, or_ref, oi_ref = rest
        inv = 1.0 / norm_ref[...]
        or_ref[...] = x[:half] * inv
        oi_ref[...] = x[half:] * inv
        return
    g_ref, hr_ref, hi_ref, or_ref, oi_ref = rest
    xr, xi = x[:half], x[half:]
    hr, hi = hr_ref[...], hi_ref[...]
    y = jnp.concatenate([xr * hr - xi * hi, xr * hi + xi * hr], axis=0).astype(BF16)
    z = jnp.dot(g_ref[...], y, preferred_element_type=F32)
    or_ref[...] = z[:half].astype(or_ref.dtype)
    oi_ref[...] = z[half:].astype(oi_ref.dtype)


def hy_stage_b(ar, ai, fmat, gmat=None, hr=None, hi=None, order=None, norm=None):
    npair, n_a, rows, width = ar.shape
    spectrum_only = gmat is None
    aspec = pl.BlockSpec((None, None, rows, width), lambda k, p: (p, k, 0, 0))
    mspec = pl.BlockSpec((None, 2 * rows, 2 * rows), lambda k, p: (k, 0, 0))
    in_specs = [aspec, aspec, mspec]
    args = [ar, ai, fmat]
    if spectrum_only:
        in_specs.append(pl.BlockSpec((None, 1, width), lambda k, p: (p, 0, 0)))
        args.append(norm.reshape(npair, 1, width))
    else:
        hspec = pl.BlockSpec((None, None, rows, width), lambda k, p: (order, k, 0, 0))
        in_specs += [mspec, hspec, hspec]
        args += [gmat, hr, hi]
    out_dtype = F32 if spectrum_only else BF16
    return pl.pallas_call(
        functools.partial(_hy_stage_b_kernel, spectrum_only=spectrum_only), grid=(n_a, npair),
        in_specs=in_specs, out_specs=[aspec, aspec],
        out_shape=[jax.ShapeDtypeStruct(ar.shape, out_dtype)] * 2,
        compiler_params=_cparams(2), name="hyena_stage_b")(*args)


def _hy_stage_a_inv_kernel(ar_ref, ai_ref, u_ref, g_ref, skip_ref, o_ref, *, tile0_u, tile0_g):
    def slab(rows, lanes):
        re = [ar_ref[k, rows, lanes].astype(F32) for k in range(ar_ref.shape[0])]
        im = [ai_ref[k, rows, lanes].astype(F32) for k in range(ai_ref.shape[0])]
        out = _fft_inv_first_half(re, im)
        skip = skip_ref[:, lanes]
        for part in range(2):
            for a in range(len(out[part])):
                conv = out[part][a] + skip * u_ref[part, tile0_u + a, rows, lanes].astype(F32)
                gate = g_ref[part, tile0_g + a, rows, lanes].astype(F32)
                o_ref[part, a, rows, lanes] = (gate * conv).astype(o_ref.dtype)
    _slab_loop(ar_ref.shape[1], ar_ref.shape[2], slab)


def hy_stage_a_inv(ar, ai, u5, u_col0, tile0_u, g5, g_col0, tile0_g, skip, out_dtype):
    npair, n_a, rows, width = ar.shape
    n_out = n_a // 2
    rb, cb = 64, 256
    kern = functools.partial(_hy_stage_a_inv_kernel, tile0_u=tile0_u, tile0_g=tile0_g)
    aspec = pl.BlockSpec((None, n_a, rb, cb), lambda p, c, r: (p, 0, r, c))
    return pl.pallas_call(
        kern, grid=(npair, width // cb, rows // rb),
        in_specs=[aspec, aspec,
                  pl.BlockSpec((2, None, u5.shape[2], rb, cb), lambda p, c, r: (0, p, 0, r, u_col0 // cb + c)),
                  pl.BlockSpec((2, None, g5.shape[2], rb, cb), lambda p, c, r: (0, p, 0, r, g_col0 // cb + c)),
                  pl.BlockSpec((1, cb), lambda p, c, r: (0, c))],
        out_specs=pl.BlockSpec((2, None, n_out, rb, cb), lambda p, c, r: (0, p, 0, r, c)),
        out_shape=jax.ShapeDtypeStruct((2, npair, n_out, rows, width), out_dtype),
        compiler_params=_cparams(3), name="hyena_stage_a_inv")(ar, ai, u5, g5, skip.reshape(1, width))


def dft_matrices(n_a):
    n = n_a * DFT_ROWS
    kb = jnp.arange(DFT_ROWS, dtype=jnp.int32)[None, :, None]
    b = jnp.arange(DFT_ROWS, dtype=jnp.int32)[None, None, :]
    ka = jnp.arange(n_a, dtype=jnp.int32)[:, None, None]
    ang = (-2.0 * math.pi / n) * ((b * (ka + n_a * kb)) % n).astype(F32)
    fr, fi = jnp.cos(ang), jnp.sin(ang)
    blk = jnp.concatenate([jnp.concatenate([fr, -fi], axis=2), jnp.concatenate([fi, fr], axis=2)], axis=1)
    return blk.astype(BF16), (jnp.swapaxes(blk, 1, 2) * (1.0 / n)).astype(BF16)


def hyena_branch(u5, width, tile0, n_tiles, fargs, skip):
    length = n_tiles * DFT_ROWS
    n_a = 2 * n_tiles
    fmat, gmat = dft_matrices(n_a)
    hc, norm = hyena_conv_filters(length, *fargs)
    hc = hc.reshape(HY_ORDER, n_a, DFT_ROWS, width)
    hr, hi = hy_stage_b(*hy_stage_a(jnp.stack([hc, jnp.zeros_like(hc)]), 0, width, 0, n_a, False), fmat,
                        norm=norm)
    x5, x_col0, x_tile0 = u5, 0, tile0
    out = None
    for order in range(HY_ORDER):
        ar, ai = hy_stage_a(x5, x_col0, width, x_tile0, n_tiles, True)
        ar, ai = hy_stage_b(ar, ai, fmat, gmat, hr, hi, order=order)
        out = hy_stage_a_inv(ar, ai, x5, x_col0, x_tile0, u5, (order + 1) * width, tile0, skip[order], BF16)
        x5, x_col0, x_tile0 = out, 0, 0
    return out


def kernel(x, c, ctx, c_ctx, w_mod, b_mod, g_mix, g_ffn, w_in, b_gate, w_br, w_out, da_lambda, da_subln_g,
           lru_conv_w, lru_conv_b, lru_wa, lru_ba, lru_wi, lru_bi, lru_lambda, hy_conv_w, hy_conv_b,
           hy_f_w1, hy_f_b1, hy_f_w2, hy_f_b2, hy_f_freq, hy_f_w3, hy_decay, hy_skip, ffn_w_gate, ffn_w_up,
           ffn_w_down, moe_router, moe_w_gate, moe_w_up, moe_w_down, g_final):
    n_batch, length, d = x.shape
    n_ctx = ctx.shape[1]
    depth = w_mod.shape[0]
    assert n_ctx == ROW_G and length % ROW_G == 0 and d % 128 == 0
    tb = n_ctx + length
    gpb = tb // ROW_G
    nt = n_batch * tb
    c_end = w_in.shape[2]
    cb_k, cb_v, cb_lx, cb_q, cb_ly, cb_hy, cb_g = 0, 1, 2, 3, 4, 5, 8
    assert c_end == 11 * d

    xs = jnp.concatenate([ctx, x], axis=1).reshape(nt, d)
    silu_rows = jnp.concatenate([jax.nn.silu(c), jax.nn.silu(c_ctx)[None, :],
                                 jnp.zeros((16 - n_batch - 1, d), F32)], axis=0)
    tables = rope_tables(length, n_ctx, ROPE_LANES)

    for li in range(depth):
        lam_init = 0.8 - 0.6 * math.exp(-0.3 * li)
        tab = mm_f32_bias(silu_rows, w_mod[li], b_mod[li], d)
        tab = tab.reshape(16, 6, d).transpose(1, 0, 2)

        h = norm_mod(xs, g_mix[li], tab, 0, 1, gpb, n_batch)
        p = in_proj(h, w_in[li].astype(BF16), tables, gpb, cb_k, cb_q)
        a_out = diff_attention(p, d, da_lambda[li], da_subln_g[li], lam_init, n_batch, gpb, cb_q, cb_k, cb_v)

        bw = d // LRU_BLOCKS
        r_out = None
        hf = None
        for direction in range(2):
            w_cat = jnp.concatenate([lru_wa[li, direction], lru_wi[li, direction]], axis=-1).astype(BF16)
            res = lru_pass(p, lru_conv_w[li], lru_conv_b[li], w_cat, lru_ba[li, direction],
                           lru_bi[li, direction], lru_lambda[li, direction], n_batch, gpb, cb_lx,
                           reverse=direction == 1, hf=hf, cb_y=cb_ly)
            if direction == 0:
                hf = res
            else:
                r_out = res
        del bw

        u = hyena_short_conv(p, hy_conv_w[li], hy_conv_b[li], n_batch, gpb, cb_hy, d)
        u5 = u.reshape(2, n_batch // 2, gpb, ROW_G, 3 * d)
        fargs = (hy_f_w1[li], hy_f_b1[li], hy_f_w2[li], hy_f_b2[li], hy_f_freq[li], hy_f_w3[li], hy_decay[li])
        y_c = hyena_branch(u5, d, 0, 1, fargs, hy_skip[li])
        y_l = hyena_branch(u5, d, 1, gpb - 1, fargs, hy_skip[li])
        y_out = jnp.concatenate([y_c, y_l], axis=2).reshape(nt, d)

        xs = merge_branches(xs, a_out, r_out, y_out, p, b_gate[li], w_br[li].astype(BF16),
                            w_out[li].astype(BF16), tab, 2, gpb, n_batch, cb_g)

        jj = li // 2
        if li % 2 == 0:
            f = norm_mod(xs, g_ffn[li], tab, 3, 4, gpb, n_batch)
            hh = gate_up(f, ffn_w_gate[jj].astype(BF16), ffn_w_up[jj].astype(BF16))
            xs = down_residual(hh, ffn_w_down[jj].astype(BF16), xs, tab, 5, gpb, n_batch)
        else:
            wr = jnp.concatenate([moe_router[jj], jnp.zeros((d, 128 - N_EXPERTS), F32)], axis=1)
            f, logits = norm_mod(xs, g_ffn[li], tab, 3, 4, gpb, n_batch, w_router=wr)
            xs = moe_layer(f, logits, moe_w_gate[jj].astype(BF16), moe_w_up[jj].astype(BF16),
                           moe_w_down[jj].astype(BF16), xs, tab, 5, gpb, n_batch)

    return final_norm(xs.reshape(n_batch, tb, d), g_final, n_ctx)
```

```python
import functools
import math

import jax
import jax.numpy as jnp
from jax import lax
from jax.experimental import pallas as pl
from jax.experimental.pallas import tpu as pltpu

F32 = jnp.float32
BF16 = jnp.bfloat16

EPS = 1e-6
ROW_G = 256
GRID_W = 64
ROPE_BASE = 10000.0
DA_HEADS = 8
DA_HEAD_DIM = 64
DA_V_DIM = 128
LRU_BLOCKS = 8
LRU_C = 8.0
HY_ORDER = 2
HY_BANDS = 16
N_EXPERTS = 8
TOP_K = 2
MOE_BLOCK = 256
HALO = 16
VMEM_LIMIT = 56 * 1024 * 1024


def _cparams(n_axes):
    return pltpu.CompilerParams(dimension_semantics=("arbitrary",) * n_axes,
                                vmem_limit_bytes=VMEM_LIMIT)


def _pick(n, prefs):
    for p in prefs:
        if n % p == 0:
            return p
    raise ValueError(f"no tile in {prefs} divides {n}")


def _mod_row(tab_ref, k, group, groups_per_batch, n_batch):
    b = group // groups_per_batch
    row = jnp.where(group % groups_per_batch == 0, n_batch, b)
    return tab_ref[k, pl.ds(row, 1), :]


def _mm_f32_kernel(a_ref, w_ref, b_ref, o_ref):
    o_ref[...] = jnp.dot(a_ref[...], w_ref[...], preferred_element_type=F32) + b_ref[...]


def mm_f32_bias(a, w, b, tn):
    m, k = a.shape
    n = w.shape[1]
    return pl.pallas_call(
        _mm_f32_kernel, grid=(n // tn,),
        in_specs=[pl.BlockSpec((m, k), lambda j: (0, 0)),
                  pl.BlockSpec((k, tn), lambda j: (0, j)),
                  pl.BlockSpec((1, tn), lambda j: (0, j))],
        out_specs=pl.BlockSpec((m, tn), lambda j: (0, j)),
        out_shape=jax.ShapeDtypeStruct((m, n), F32),
        compiler_params=_cparams(1), name="mod_matmul")(a, w, b.reshape(1, n))


def _norm_mod_kernel(x_ref, g_ref, tab_ref, *rest, k_shift, k_scale, gpb, n_batch, router):
    if router:
        wr_ref, o_ref, lg_ref = rest
    else:
        (o_ref,) = rest
    nsub = x_ref.shape[0] // ROW_G
    for s in range(nsub):
        grp = pl.program_id(0) * nsub + s
        rows = pl.ds(s * ROW_G, ROW_G)
        xs = x_ref[rows, :]
        y = xs * lax.rsqrt(jnp.mean(xs * xs, axis=-1, keepdims=True) + EPS) * g_ref[...]
        shift = _mod_row(tab_ref, k_shift, grp, gpb, n_batch)
        scale = _mod_row(tab_ref, k_scale, grp, gpb, n_batch)
        h = y * (1.0 + scale) + shift
        o_ref[rows, :] = h.astype(o_ref.dtype)
        if router:
            lg_ref[rows, :] = jnp.dot(h, wr_ref[...], preferred_element_type=F32,
                                      precision=lax.Precision.HIGHEST)


def norm_mod(x, g, tab, k_shift, k_scale, gpb, n_batch, w_router=None):
    nt, d = x.shape
    tm = _pick(nt, (1024, 512, 256))
    router = w_router is not None
    kern = functools.partial(_norm_mod_kernel, k_shift=k_shift, k_scale=k_scale, gpb=gpb,
                             n_batch=n_batch, router=router)
    in_specs = [pl.BlockSpec((tm, d), lambda i: (i, 0)),
                pl.BlockSpec((1, d), lambda i: (0, 0)),
                pl.BlockSpec(tab.shape, lambda i: (0, 0, 0))]
    out_specs = [pl.BlockSpec((tm, d), lambda i: (i, 0))]
    out_shape = [jax.ShapeDtypeStruct((nt, d), BF16)]
    args = [x, g.reshape(1, d), tab]
    if router:
        in_specs.append(pl.BlockSpec(w_router.shape, lambda i: (0, 0)))
        out_specs.append(pl.BlockSpec((tm, w_router.shape[1]), lambda i: (i, 0)))
        out_shape.append(jax.ShapeDtypeStruct((nt, w_router.shape[1]), F32))
        args.append(w_router)
    out = pl.pallas_call(kern, grid=(nt // tm,), in_specs=in_specs, out_specs=out_specs,
                         out_shape=out_shape, compiler_params=_cparams(1), name="norm_mod")(*args)
    return out if router else out[0]


def _final_norm_kernel(x_ref, g_ref, o_ref):
    xs = x_ref[...]
    o_ref[...] = xs * lax.rsqrt(jnp.mean(xs * xs, axis=-1, keepdims=True) + EPS) * g_ref[...]


def final_norm(x3, g, n_ctx):
    b, tb, d = x3.shape
    length = tb - n_ctx
    tm = ROW_G
    off = n_ctx // tm
    return pl.pallas_call(
        _final_norm_kernel, grid=(b, length // tm),
        in_specs=[pl.BlockSpec((None, tm, d), lambda i, j: (i, j + off, 0)),
                  pl.BlockSpec((1, d), lambda i, j: (0, 0))],
        out_specs=pl.BlockSpec((None, tm, d), lambda i, j: (i, j, 0)),
        out_shape=jax.ShapeDtypeStruct((b, length, d), F32),
        compiler_params=_cparams(2), name="final_norm")(x3, g.reshape(1, d))


ROPE_LANES = 128


def _in_proj_kernel(a_ref, w_ref, cos_ref, sa_ref, sb_ref, o_ref, *, cb_k, cb_q, q_scale, gpb):
    n = pl.program_id(0)
    is_rot = jnp.logical_or(n == cb_k, n == cb_q)

    @pl.when(jnp.logical_not(is_rot))
    def _():
        o_ref[...] = jnp.dot(a_ref[...], w_ref[...], preferred_element_type=F32).astype(o_ref.dtype)

    @pl.when(is_rot)
    def _():
        scale = jnp.where(n == cb_q, q_scale, 1.0)
        nsub = a_ref.shape[0] // ROW_G
        for s in range(nsub):
            rows = slice(s * ROW_G, (s + 1) * ROW_G)
            acc = jnp.dot(a_ref[rows, :], w_ref[...], preferred_element_type=F32)
            j = (pl.program_id(1) * nsub + s) % gpb
            trow = pl.ds(pl.multiple_of(j * ROW_G, ROW_G), ROW_G)
            cos, sa, sb = cos_ref[trow, :] * scale, sa_ref[trow, :] * scale, sb_ref[trow, :] * scale
            for c in range(acc.shape[1] // ROPE_LANES):
                t = acc[:, c * ROPE_LANES:(c + 1) * ROPE_LANES]
                rot = (t * cos + pltpu.roll(t, ROPE_LANES - 16, 1) * sa + pltpu.roll(t, 16, 1) * sb)
                o_ref[rows, c * ROPE_LANES:(c + 1) * ROPE_LANES] = rot.astype(o_ref.dtype)


def in_proj(h, w, tables, gpb, cb_k, cb_q):
    m, k = h.shape
    n = w.shape[1]
    tm = _pick(m, (1024, 512, 256))
    tn = k
    cos, sa, sb = tables
    kern = functools.partial(_in_proj_kernel, cb_k=cb_k, cb_q=cb_q, gpb=gpb,
                             q_scale=math.log2(math.e) * DA_HEAD_DIM ** -0.5)
    tspec = pl.BlockSpec(cos.shape, lambda j, i: (0, 0))
    return pl.pallas_call(
        kern, grid=(n // tn, m // tm),
        in_specs=[pl.BlockSpec((tm, k), lambda j, i: (i, 0)),
                  pl.BlockSpec((k, tn), lambda j, i: (0, j)), tspec, tspec, tspec],
        out_specs=pl.BlockSpec((tm, tn), lambda j, i: (i, j)),
        out_shape=jax.ShapeDtypeStruct((m, n), BF16),
        compiler_params=_cparams(2), name="in_proj")(h, w, cos, sa, sb)


def rope_tables(length, n_ctx, width):
    rows = length // GRID_W
    row = jnp.repeat(jnp.arange(rows, dtype=F32), GRID_W)
    col = jnp.tile(jnp.arange(GRID_W, dtype=F32), rows)
    n_freq = DA_HEAD_DIM // 4
    inv = ROPE_BASE ** (-jnp.arange(n_freq, dtype=F32) / n_freq)
    ar = row[:, None] * inv
    ac = col[:, None] * inv
    ang = jnp.concatenate([ar, ar, ac, ac], axis=-1)
    ang = jnp.concatenate([jnp.zeros((n_ctx, DA_HEAD_DIM), F32), ang], axis=0)
    reps = width // DA_HEAD_DIM
    cos = jnp.tile(jnp.cos(ang), (1, reps))
    sin = jnp.tile(jnp.sin(ang), (1, reps))
    first = (jnp.arange(width) % (2 * n_freq)) < n_freq
    sa = jnp.where(first[None, :], -sin, 0.0)
    sb = jnp.where(first[None, :], 0.0, sin)
    return cos, sa, sb


def _attn_kernel(lamv_ref, g_ref, q_ref, k_ref, v_ref, o_ref, vaug_ref, m_ref, acc_ref,
                 *, lam_init, n_ctx, tk):
    j = pl.program_id(2)
    dv = v_ref.shape[1]

    @pl.when(j == 0)
    def _():
        vaug_ref[:, :dv] = v_ref[...]
        vaug_ref[:, dv:] = jnp.ones((vaug_ref.shape[0], vaug_ref.shape[1] - dv), vaug_ref.dtype)

    q = q_ref[...]
    lane = lax.broadcasted_iota(jnp.int32, q.shape, 1)
    zero = jnp.zeros_like(q)
    qq = jnp.concatenate([jnp.where(lane < DA_HEAD_DIM, q, zero),
                          jnp.where(lane >= DA_HEAD_DIM, q, zero)], axis=0)
    tq = q.shape[0]

    def scores(start, size):
        kc = k_ref[pl.ds(start, size), :]
        return lax.dot_general(qq, kc, (((1,), (1,)), ((), ())), preferred_element_type=F32)

    def update(s, start, size, first=False):
        m_new = jnp.max(s, axis=-1, keepdims=True)
        if first:
            m_new = jnp.broadcast_to(m_new, m_ref.shape)
        else:
            m_prev = m_ref[...]
            m_new = jnp.maximum(m_prev, m_new)
        p = jnp.exp2(s - jnp.concatenate([m_new] * (size // 128), axis=1))
        pv = jnp.dot(p.astype(BF16), vaug_ref[pl.ds(start, size), :], preferred_element_type=F32)
        if first:
            acc_ref[...] = pv
        else:
            alpha = jnp.exp2(m_prev - m_new)
            acc_ref[...] = jnp.concatenate([alpha] * (acc_ref.shape[1] // 128), axis=1) * acc_ref[...] + pv
        m_ref[...] = m_new

    s_ctx = scores(0, n_ctx)

    @pl.when(j == 0)
    def _():
        update(s_ctx, 0, n_ctx, first=True)

    @pl.when(j > 0)
    def _():
        starts = [n_ctx + i * tk for i in range((k_ref.shape[0] - n_ctx) // tk)]
        s_next = scores(starts[0], tk)
        update(s_ctx, 0, n_ctx, first=True)
        for i, start in enumerate(starts):
            s_cur = s_next
            if i + 1 < len(starts):
                s_next = scores(starts[i + 1], tk)
            update(s_cur, start, tk)

    lv = lamv_ref[...]
    lam = (jnp.exp(jnp.sum(lv[0:1] * lv[1:2], axis=-1, keepdims=True))
           - jnp.exp(jnp.sum(lv[2:3] * lv[3:4], axis=-1, keepdims=True)) + lam_init)
    o = acc_ref[:, :dv] / acc_ref[:, dv:]
    o = o[:tq] - lam * o[tq:]
    o = o * lax.rsqrt(jnp.mean(o * o, axis=-1, keepdims=True) + EPS) * g_ref[...]
    o_ref[...] = (o * (1.0 - lam_init)).astype(o_ref.dtype)


def diff_attention(p, width, lam_vecs, subln_g, lam_init, n_batch, gpb, cb_q, cb_k, cb_v):
    nt = p.shape[0]
    tb = gpb * ROW_G
    n_ctx = ROW_G
    tk = _pick(tb - n_ctx, (2048, 1024, 512, 256))
    kern = functools.partial(_attn_kernel, lam_init=lam_init, n_ctx=n_ctx, tk=tk)
    assert DA_V_DIM == 128
    hpb = width // DA_V_DIM
    return pl.pallas_call(
        kern, grid=(n_batch, DA_HEADS, gpb),
        in_specs=[pl.BlockSpec(lam_vecs.shape, lambda b, h, j: (0, 0)),
                  pl.BlockSpec((1, DA_V_DIM), lambda b, h, j: (0, 0)),
                  pl.BlockSpec((ROW_G, DA_V_DIM), lambda b, h, j: (b * gpb + j, cb_q * hpb + h)),
                  pl.BlockSpec((tb, DA_V_DIM), lambda b, h, j: (b, cb_k * hpb + h)),
                  pl.BlockSpec((tb, DA_V_DIM), lambda b, h, j: (b, cb_v * hpb + h))],
        out_specs=pl.BlockSpec((ROW_G, DA_V_DIM), lambda b, h, j: (b * gpb + j, h)),
        out_shape=jax.ShapeDtypeStruct((nt, width), BF16),
        scratch_shapes=[pltpu.VMEM((tb, 2 * DA_V_DIM), BF16), pltpu.VMEM((2 * ROW_G, 128), F32),
                        pltpu.VMEM((2 * ROW_G, 2 * DA_V_DIM), F32)],
        compiler_params=_cparams(3), name="diff_attn")(
            lam_vecs, subln_g.reshape(1, DA_V_DIM), p, p, p)


def _shift_matrix(n, d):
    row = lax.broadcasted_iota(jnp.int32, (n, n), 0)
    col = lax.broadcasted_iota(jnp.int32, (n, n), 1)
    return jnp.where(col == row + d, 1.0, 0.0).astype(BF16)


def _dwconv(main_ref, prev_ref, next_ref, w_ref, b_ref, left, has_prev, has_next):
    x = main_ref[...]
    acc = b_ref[...] + w_ref[left:left + 1, :] * x.astype(F32)
    prev8 = prev_ref[...].astype(F32)[HALO - 8:, :]
    next8 = next_ref[...].astype(F32)[:8, :]
    sub = lax.broadcasted_iota(jnp.int32, (8, 1), 0)
    first = jnp.zeros_like(prev8)
    last = jnp.zeros_like(next8)
    for j in range(w_ref.shape[0]):
        d = j - left
        if d == 0:
            continue
        wj = w_ref[j:j + 1, :]
        acc = acc + wj * jnp.dot(_shift_matrix(ROW_G, d), x, preferred_element_type=F32)
        if d < 0:
            mask = jnp.logical_and(sub < -d, has_prev)
            first = first + wj * jnp.where(mask, pltpu.roll(prev8, -d, 0), 0.0)
        else:
            mask = jnp.logical_and(sub >= 8 - d, has_next)
            last = last + wj * jnp.where(mask, pltpu.roll(next8, 8 - d, 0), 0.0)
    return jnp.concatenate([acc[0:8] + first, acc[8:ROW_G - 8], acc[ROW_G - 8:] + last], axis=0)


def _halo_specs(width, cb, gpb, tile_of, n_groups):
    per = ROW_G // HALO

    def main(b, s, *_):
        return (b * gpb + tile_of(s), cb)

    def prev(b, s, *_):
        return (jnp.maximum((b * gpb + tile_of(s)) * per - 1, 0), cb)

    def nxt(b, s, *_):
        return (jnp.minimum((b * gpb + tile_of(s) + 1) * per, n_groups * per - 1), cb)

    return [pl.BlockSpec((ROW_G, width), main), pl.BlockSpec((HALO, width), prev),
            pl.BlockSpec((HALO, width), nxt)]


def _seq_flags(j, gpb):
    return j >= 2, jnp.logical_and(j >= 1, j <= gpb - 2)


SCAN_G = 8


def _scan_groups(a, b, reverse):
    shape = a.shape
    a = a.reshape(shape[0] // SCAN_G, SCAN_G, shape[1])
    b = b.reshape(a.shape)
    pos = lax.broadcasted_iota(jnp.int32, a.shape, 1)
    s = 1
    while s < SCAN_G:
        keep = pos < SCAN_G - s if reverse else pos >= s
        shift = SCAN_G - s if reverse else s
        a_sh = jnp.where(keep, pltpu.roll(a, shift, 1), 1.0)
        b_sh = jnp.where(keep, pltpu.roll(b, shift, 1), 0.0)
        b = a * b_sh + b
        a = a * a_sh
        s *= 2
    return a.reshape(shape), b.reshape(shape)


def _gelu_tanh(x):
    return 0.5 * x * (1.0 + jnp.tanh(math.sqrt(2.0 / math.pi) * (x + 0.044715 * (x * x * x))))


def _lru_kernel(main_ref, prev_ref, next_ref, cw_ref, cb_ref, w_ref, ba_ref, bi_ref, lam_ref, *rest,
                gpb, reverse):
    if reverse:
        hf_ref, ly_ref, o_ref, carry_ref = rest
    else:
        o_ref, carry_ref = rest
    s = pl.program_id(1)
    j = jnp.where(s == 0, 0, gpb - s) if reverse else s
    has_prev, has_next = _seq_flags(j, gpb)
    xc = _dwconv(main_ref, prev_ref, next_ref, cw_ref, cb_ref, 2, has_prev, has_next)
    xcb = xc.astype(BF16)
    bw = w_ref.shape[1]
    r_parts, i_parts = [], []
    for n in range(w_ref.shape[0]):
        res = jnp.dot(xcb[:, n * bw:(n + 1) * bw], w_ref[n], preferred_element_type=F32)
        r_parts.append(res[:, :bw])
        i_parts.append(res[:, bw:])
    r = jax.nn.sigmoid(jnp.concatenate(r_parts, axis=1) + ba_ref[...])
    gate_i = jax.nn.sigmoid(jnp.concatenate(i_parts, axis=1) + bi_ref[...])
    nl = -lam_ref[...]
    softplus = jnp.maximum(nl, 0.0) + jnp.log(1.0 + jnp.exp(-jnp.abs(nl)))
    a = jnp.exp((-LRU_C) * r * softplus)
    bb = jnp.sqrt(1.0 - a * a) * (gate_i * xc)
    a_grp, h_grp = _scan_groups(a, bb, reverse)

    @pl.when(s == 0)
    def _():
        carry_ref[...] = jnp.zeros(carry_ref.shape, F32)

    h = carry_ref[...]
    n_grp = ROW_G // SCAN_G
    pieces = [None] * n_grp
    for g in (reversed(range(n_grp)) if reverse else range(n_grp)):
        rows = slice(g * SCAN_G, (g + 1) * SCAN_G)
        hg = a_grp[rows] * h + h_grp[rows]
        pieces[g] = hg
        h = hg[0:1, :] if reverse else hg[SCAN_G - 1:SCAN_G, :]
    carry_ref[...] = h
    h_all = jnp.concatenate(pieces, axis=0)
    if reverse:
        ly = ly_ref[...].astype(F32)
        o_ref[...] = (_gelu_tanh(ly) * (hf_ref[...] + h_all)).astype(o_ref.dtype)
    else:
        o_ref[...] = h_all


def lru_pass(p, conv_w, conv_b, w_cat, ba, bi, lam, n_batch, gpb, cb_x, reverse, hf=None, cb_y=None):
    nt = p.shape[0]
    width = conv_w.shape[1]
    tile_of = (lambda s: jnp.where(s == 0, 0, gpb - s)) if reverse else (lambda s: s)
    kern = functools.partial(_lru_kernel, gpb=gpb, reverse=reverse)
    const2 = lambda b, s: (0, 0)
    in_specs = _halo_specs(width, cb_x, gpb, tile_of, nt // ROW_G) + [
        pl.BlockSpec(conv_w.shape, const2), pl.BlockSpec((1, width), const2),
        pl.BlockSpec(w_cat.shape, lambda b, s: (0, 0, 0)),
        pl.BlockSpec((1, width), const2), pl.BlockSpec((1, width), const2),
        pl.BlockSpec((1, width), const2)]
    args = [p, p, p, conv_w, conv_b.reshape(1, width), w_cat, ba.reshape(1, width),
            bi.reshape(1, width), lam.reshape(1, width)]
    row_spec = lambda cb: pl.BlockSpec((ROW_G, width), lambda b, s: (b * gpb + tile_of(s), cb))
    if reverse:
        in_specs += [row_spec(0), row_spec(cb_y)]
        args += [hf, p]
        out_dtype = BF16
    else:
        out_dtype = F32
    return pl.pallas_call(
        kern, grid=(n_batch, gpb), in_specs=in_specs, out_specs=row_spec(0),
        out_shape=jax.ShapeDtypeStruct((nt, width), out_dtype),
        scratch_shapes=[pltpu.VMEM((1, width), F32)],
        compiler_params=_cparams(2), name="lru_bwd" if reverse else "lru_fwd")(*args)


def _short_conv_kernel(main_ref, prev_ref, next_ref, cw_ref, cb_ref, o_ref, *, gpb):
    j = pl.program_id(1)
    has_prev, has_next = _seq_flags(j, gpb)
    o_ref[...] = _dwconv(main_ref, prev_ref, next_ref, cw_ref, cb_ref, 1, has_prev, has_next).astype(o_ref.dtype)


def hyena_short_conv(p, conv_w, conv_b, n_batch, gpb, cb0, width):
    nt = p.shape[0]
    total = conv_w.shape[1]
    ncb = total // width
    base = _halo_specs(width, 0, gpb, lambda s: s, nt // ROW_G)

    def shifted(spec):
        f = spec.index_map
        return pl.BlockSpec(spec.block_shape, lambda b, j, c: (f(b, j)[0], cb0 + c))

    kern = functools.partial(_short_conv_kernel, gpb=gpb)
    return pl.pallas_call(
        kern, grid=(n_batch, gpb, ncb),
        in_specs=[shifted(sp) for sp in base] + [
            pl.BlockSpec((conv_w.shape[0], width), lambda b, j, c: (0, c)),
            pl.BlockSpec((1, width), lambda b, j, c: (0, c))],
        out_specs=pl.BlockSpec((ROW_G, width), lambda b, j, c: (b * gpb + j, c)),
        out_shape=jax.ShapeDtypeStruct((nt, total), BF16),
        compiler_params=_cparams(3), name="hyena_short_conv")(p, p, p, conv_w, conv_b.reshape(1, total))


def _merge_kernel(a_ref, r_ref, y_ref, g0_ref, g1_ref, g2_ref, bg_ref, wbr_ref, wout_ref, tab_ref, x_ref,
                  o_ref, *, k_gate, gpb, n_batch):
    width = a_ref.shape[1]
    m = None
    for k, (br_ref, gt_ref) in enumerate(((a_ref, g0_ref), (r_ref, g1_ref), (y_ref, g2_ref))):
        gate = jax.nn.sigmoid(gt_ref[...].astype(F32) + bg_ref[:, k * width:(k + 1) * width])
        term = gate * jnp.dot(br_ref[...], wbr_ref[k], preferred_element_type=F32)
        m = term if m is None else m + term
    out = jnp.dot(m.astype(BF16), wout_ref[...], preferred_element_type=F32)
    nsub = x_ref.shape[0] // ROW_G
    for s in range(nsub):
        rows = pl.ds(s * ROW_G, ROW_G)
        gmod = _mod_row(tab_ref, k_gate, pl.program_id(0) * nsub + s, gpb, n_batch)
        o_ref[rows, :] = x_ref[rows, :] + gmod * out[s * ROW_G:(s + 1) * ROW_G, :]


def merge_branches(x, a, r, y, p, b_gate, w_br, w_out, tab, k_gate, gpb, n_batch, cb_g):
    nt, d = x.shape
    tm = _pick(nt, (512, 256))
    kern = functools.partial(_merge_kernel, k_gate=k_gate, gpb=gpb, n_batch=n_batch)
    row = pl.BlockSpec((tm, d), lambda i: (i, 0))
    gspec = lambda k: pl.BlockSpec((tm, d), lambda i: (i, cb_g + k))
    return pl.pallas_call(
        kern, grid=(nt // tm,),
        in_specs=[row, row, row, gspec(0), gspec(1), gspec(2),
                  pl.BlockSpec((1, 3 * d), lambda i: (0, 0)),
                  pl.BlockSpec(w_br.shape, lambda i: (0, 0, 0)),
                  pl.BlockSpec(w_out.shape, lambda i: (0, 0)),
                  pl.BlockSpec(tab.shape, lambda i: (0, 0, 0)), row],
        out_specs=row, out_shape=jax.ShapeDtypeStruct((nt, d), F32),
        compiler_params=_cparams(1), name="merge")(a, r, y, p, p, p, b_gate.reshape(1, 3 * d), w_br, w_out, tab, x)


def _gate_up_kernel(*refs, expert):
    if expert:
        _, f_ref, wg_ref, wu_ref, o_ref = refs
    else:
        f_ref, wg_ref, wu_ref, o_ref = refs
    f = f_ref[...]
    g = jnp.dot(f, wg_ref[...], preferred_element_type=F32)
    u = jnp.dot(f, wu_ref[...], preferred_element_type=F32)
    o_ref[...] = (g * jax.nn.sigmoid(g) * u).astype(o_ref.dtype)


def gate_up(f, wg, wu, block_e=None):
    m, d = f.shape
    dff = wg.shape[-1]
    tn = _pick(dff, (1408, 1024, 512, 256, 128))
    if block_e is None:
        tm = _pick(m, (512, 256))
        grid_spec = pltpu.PrefetchScalarGridSpec(
            num_scalar_prefetch=0, grid=(dff // tn, m // tm),
            in_specs=[pl.BlockSpec((tm, d), lambda j, i: (i, 0)),
                      pl.BlockSpec((d, tn), lambda j, i: (0, j)),
                      pl.BlockSpec((d, tn), lambda j, i: (0, j))],
            out_specs=pl.BlockSpec((tm, tn), lambda j, i: (i, j)))
        args = (f, wg, wu)
    else:
        tm = MOE_BLOCK
        grid_spec = pltpu.PrefetchScalarGridSpec(
            num_scalar_prefetch=1, grid=(dff // tn, m // tm),
            in_specs=[pl.BlockSpec((tm, d), lambda j, i, be: (i, 0)),
                      pl.BlockSpec((None, d, tn), lambda j, i, be: (be[i], 0, j)),
                      pl.BlockSpec((None, d, tn), lambda j, i, be: (be[i], 0, j))],
            out_specs=pl.BlockSpec((tm, tn), lambda j, i, be: (i, j)))
        args = (block_e, f, wg, wu)
    return pl.pallas_call(
        functools.partial(_gate_up_kernel, expert=block_e is not None), grid_spec=grid_spec,
        out_shape=jax.ShapeDtypeStruct((m, dff), BF16),
        compiler_params=_cparams(2), name="gate_up")(*args)


def _down_res_kernel(h_ref, wd_ref, tab_ref, x_ref, o_ref, *, k_gate, gpb, n_batch):
    out = jnp.dot(h_ref[...], wd_ref[...], preferred_element_type=F32)
    nsub = x_ref.shape[0] // ROW_G
    for s in range(nsub):
        rows = pl.ds(s * ROW_G, ROW_G)
        gmod = _mod_row(tab_ref, k_gate, pl.program_id(0) * nsub + s, gpb, n_batch)
        o_ref[rows, :] = x_ref[rows, :] + gmod * out[s * ROW_G:(s + 1) * ROW_G, :]


def down_residual(h, wd, x, tab, k_gate, gpb, n_batch):
    nt, d = x.shape
    dff = h.shape[1]
    tm = _pick(nt, (512, 256))
    kern = functools.partial(_down_res_kernel, k_gate=k_gate, gpb=gpb, n_batch=n_batch)
    return pl.pallas_call(
        kern, grid=(nt // tm,),
        in_specs=[pl.BlockSpec((tm, dff), lambda i: (i, 0)),
                  pl.BlockSpec((dff, d), lambda i: (0, 0)),
                  pl.BlockSpec(tab.shape, lambda i: (0, 0, 0)),
                  pl.BlockSpec((tm, d), lambda i: (i, 0))],
        out_specs=pl.BlockSpec((tm, d), lambda i: (i, 0)),
        out_shape=jax.ShapeDtypeStruct((nt, d), F32),
        compiler_params=_cparams(1), name="down_residual")(h, wd, tab, x)


def _expert_down_kernel(be_ref, h_ref, wd_ref, o_ref):
    o_ref[...] = jnp.dot(h_ref[...], wd_ref[...], preferred_element_type=F32)


def expert_down(h, wd, block_e):
    m, dff = h.shape
    d = wd.shape[-1]
    grid_spec = pltpu.PrefetchScalarGridSpec(
        num_scalar_prefetch=1, grid=(m // MOE_BLOCK,),
        in_specs=[pl.BlockSpec((MOE_BLOCK, dff), lambda i, be: (i, 0)),
                  pl.BlockSpec((None, dff, d), lambda i, be: (be[i], 0, 0))],
        out_specs=pl.BlockSpec((MOE_BLOCK, d), lambda i, be: (i, 0)))
    return pl.pallas_call(_expert_down_kernel, grid_spec=grid_spec,
                          out_shape=jax.ShapeDtypeStruct((m, d), F32),
                          compiler_params=_cparams(1), name="expert_down")(block_e, h, wd)


def _combine_kernel(y0_ref, y1_ref, w0_ref, w1_ref, tab_ref, x_ref, o_ref, *, k_gate, gpb, n_batch):
    nsub = x_ref.shape[0] // ROW_G
    for s in range(nsub):
        rows = pl.ds(s * ROW_G, ROW_G)
        gmod = _mod_row(tab_ref, k_gate, pl.program_id(0) * nsub + s, gpb, n_batch)
        y = y0_ref[rows, :] * w0_ref[rows, :] + y1_ref[rows, :] * w1_ref[rows, :]
        o_ref[rows, :] = x_ref[rows, :] + gmod * y


def moe_combine(y0, y1, w0, w1, x, tab, k_gate, gpb, n_batch):
    nt, d = x.shape
    tm = _pick(nt, (512, 256))
    kern = functools.partial(_combine_kernel, k_gate=k_gate, gpb=gpb, n_batch=n_batch)
    row = pl.BlockSpec((tm, d), lambda i: (i, 0))
    wsp = pl.BlockSpec((tm, 1), lambda i: (i, 0))
    return pl.pallas_call(
        kern, grid=(nt // tm,),
        in_specs=[row, row, wsp, wsp, pl.BlockSpec(tab.shape, lambda i: (0, 0, 0)), row],
        out_specs=row, out_shape=jax.ShapeDtypeStruct((nt, d), F32),
        compiler_params=_cparams(1), name="moe_combine")(y0, y1, w0, w1, tab, x)


def moe_layer(f, logits, w_gate, w_up, w_down, x, tab, k_gate, gpb, n_batch):
    nt = f.shape[0]
    n_asg = nt * TOP_K
    top_v, top_i = lax.top_k(logits[:, :N_EXPERTS], TOP_K)
    top_w = jax.nn.softmax(top_v, axis=-1)
    e_flat = top_i.reshape(-1)
    onehot = (e_flat[:, None] == jnp.arange(N_EXPERTS, dtype=e_flat.dtype)[None, :]).astype(jnp.int32)
    csum = jnp.cumsum(onehot, axis=0)
    counts = csum[-1]
    rank = jnp.take_along_axis(csum, e_flat[:, None], axis=1)[:, 0] - 1
    padded = ((counts + MOE_BLOCK - 1) // MOE_BLOCK) * MOE_BLOCK
    pad_end = jnp.cumsum(padded)
    pad_start = pad_end - padded
    dest = (pad_start[e_flat] + rank).astype(jnp.int32)
    n_blocks = -(-n_asg // MOE_BLOCK) + N_EXPERTS
    n_slots = n_blocks * MOE_BLOCK
    block_e = jnp.minimum(jnp.searchsorted(pad_end, jnp.arange(n_blocks, dtype=jnp.int32) * MOE_BLOCK,
                                           side='right'), N_EXPERTS - 1).astype(jnp.int32)
    order = jnp.argsort(e_flat, stable=True).astype(jnp.int32)
    e_slot = jnp.repeat(block_e, MOE_BLOCK)
    within = jnp.arange(n_slots, dtype=jnp.int32) - pad_start[e_slot].astype(jnp.int32)
    valid = within < counts[e_slot]
    src = jnp.where(valid, (jnp.cumsum(counts) - counts)[e_slot].astype(jnp.int32) + within, 0)
    slot_tok = jnp.where(valid, order[src] // TOP_K, 0)
    xs = jnp.take(f, slot_tok, axis=0)
    h = gate_up(xs, w_gate, w_up, block_e)
    ys = expert_down(h, w_down, block_e)
    dest2 = dest.reshape(nt, TOP_K)
    y0 = jnp.take(ys, dest2[:, 0], axis=0)
    y1 = jnp.take(ys, dest2[:, 1], axis=0)
    return moe_combine(y0, y1, top_w[:, 0:1], top_w[:, 1:2], x, tab, k_gate, gpb, n_batch)


HY_PAD = 128


def _hy_filter_kernel(z_ref, t_ref, w1_ref, b1_ref, w2_ref, b2_ref, fr_ref, w3_ref, dec_ref, hc_ref, norm_ref,
                      *, length):
    i = pl.program_id(0)
    rows = z_ref.shape[0]
    width = dec_ref.shape[1]
    hp = lax.Precision.HIGHEST
    fr = fr_ref[...]
    hdn = jnp.sin(fr * (jnp.dot(z_ref[...], w1_ref[...], precision=hp, preferred_element_type=F32) + b1_ref[...]))
    hdn = jnp.sin(fr * (jnp.dot(hdn, w2_ref[...], precision=hp, preferred_element_type=F32) + b2_ref[...]))
    w3 = w3_ref[...]
    h_hi, w_hi = hdn.astype(BF16), w3.astype(BF16)
    h_lo = (hdn - h_hi.astype(F32)).astype(BF16)
    w_lo = (w3 - w_hi.astype(F32)).astype(BF16)
    filt = (jnp.dot(h_hi, w_hi, preferred_element_type=F32) + jnp.dot(h_hi, w_lo, preferred_element_type=F32)
            + jnp.dot(h_lo, w_hi, preferred_element_type=F32))
    t = t_ref[...]
    r = i * rows + lax.broadcasted_iota(jnp.int32, (rows, 1), 0)
    first_half = r < length

    @pl.when(i == 0)
    def _():
        norm_ref[...] = jnp.full(norm_ref.shape, EPS, F32)

    for o in range(HY_ORDER):
        win = jnp.exp(-t * jnp.abs(dec_ref[o:o + 1, :]))
        fwd = filt[:, (2 * o) * width:(2 * o + 1) * width] * win
        bwd = filt[:, (2 * o + 1) * width:(2 * o + 2) * width] * win
        mass = jnp.where(first_half, jnp.abs(fwd) + jnp.abs(bwd), 0.0)
        norm_ref[o:o + 1, :] += jnp.sum(mass, axis=0, keepdims=True)
        hc = jnp.where(first_half, fwd, bwd) + jnp.where(r == 0, bwd, 0.0)
        hc_ref[o] = jnp.where(r == length, 0.0, hc)


def hyena_conv_filters(length, w1, b1, w2, b2, freq, w3, decay):
    n = 2 * length
    width = decay.shape[-1]
    r = jnp.arange(n, dtype=jnp.int32)
    tap = jnp.where(r < length, r, jnp.where(r == length, 0, n - r))
    t = jnp.linspace(0.0, 1.0, length, dtype=F32)[tap][:, None]
    w = 2.0 * math.pi * tap.astype(F32)[:, None] / length
    bands = jnp.linspace(1e-4, HY_BANDS - 1, HY_BANDS, dtype=F32)
    z = jnp.concatenate([t, jnp.cos(bands * w), -jnp.sin(bands * w)], axis=-1)
    assert max(w1.shape) <= HY_PAD

    def pad_to(x, shape):
        return jnp.pad(x, [(0, s - d) for s, d in zip(shape, x.shape)])

    args = (pad_to(z, (n, HY_PAD)), t, pad_to(w1, (HY_PAD, HY_PAD)), pad_to(b1[None, :], (1, HY_PAD)),
            pad_to(w2, (HY_PAD, HY_PAD)), pad_to(b2[None, :], (1, HY_PAD)), pad_to(freq[None, :], (1, HY_PAD)),
            pad_to(w3, (HY_PAD, w3.shape[1])), decay)
    const = lambda i: (0, 0)
    return pl.pallas_call(
        functools.partial(_hy_filter_kernel, length=length), grid=(n // DFT_ROWS,),
        in_specs=[pl.BlockSpec((DFT_ROWS, HY_PAD), lambda i: (i, 0)), pl.BlockSpec((DFT_ROWS, 1), lambda i: (i, 0)),
                  pl.BlockSpec((HY_PAD, HY_PAD), const), pl.BlockSpec((1, HY_PAD), const),
                  pl.BlockSpec((HY_PAD, HY_PAD), const), pl.BlockSpec((1, HY_PAD), const),
                  pl.BlockSpec((1, HY_PAD), const), pl.BlockSpec((HY_PAD, w3.shape[1]), const),
                  pl.BlockSpec(decay.shape, const)],
        out_specs=[pl.BlockSpec((HY_ORDER, DFT_ROWS, width), lambda i: (0, i, 0)),
                   pl.BlockSpec((HY_ORDER, width), const)],
        out_shape=[jax.ShapeDtypeStruct((HY_ORDER, n, width), F32),
                   jax.ShapeDtypeStruct((HY_ORDER, width), F32)],
        compiler_params=_cparams(1), name="hyena_filter")(*args)


DFT_ROWS = ROW_G
GRP = 16


def _hy_stage_a_kernel(x_ref, w_ref, ar_ref, ai_ref, *, tile0, n_in):
    planes = x_ref.shape[0]
    n_a = ar_ref.shape[0]
    for g in range(ar_ref.shape[1] // GRP):
        rows = slice(g * GRP, (g + 1) * GRP)
        rhs = jnp.concatenate([x_ref[q, tile0 + a, rows, :].astype(BF16)
                               for q in range(planes) for a in range(n_in)], axis=0)
        out = jnp.dot(w_ref[...], rhs, preferred_element_type=F32)
        for k in range(n_a):
            ar_ref[k, rows, :] = out[k * GRP:(k + 1) * GRP].astype(ar_ref.dtype)
            ai_ref[k, rows, :] = out[(n_a + k) * GRP:(n_a + k + 1) * GRP].astype(ai_ref.dtype)


def hy_stage_a(x5, wmat, col0, width, tile0, n_in):
    planes, npair, ntile, rows, _ = x5.shape
    n_a = wmat.shape[0] // (2 * GRP)
    assert wmat.shape[1] == planes * n_in * GRP
    rb, cb = 64, 256
    kern = functools.partial(_hy_stage_a_kernel, tile0=tile0, n_in=n_in)
    ospec = pl.BlockSpec((None, n_a, rb, cb), lambda p, c, r: (p, 0, r, c))
    return pl.pallas_call(
        kern, grid=(npair, width // cb, rows // rb),
        in_specs=[pl.BlockSpec((planes, None, ntile, rb, cb), lambda p, c, r: (0, p, 0, r, col0 // cb + c)),
                  pl.BlockSpec(wmat.shape, lambda p, c, r: (0, 0))],
        out_specs=[ospec, ospec],
        out_shape=[jax.ShapeDtypeStruct((npair, n_a, rows, width), BF16)] * 2,
        compiler_params=_cparams(3), name="hyena_stage_a")(x5, wmat)


def _hy_stage_b_kernel(ar_ref, ai_ref, f_ref, *rest, spectrum_only):
    half = ar_ref.shape[0]
    a = jnp.concatenate([ar_ref[...], ai_ref[...]], axis=0)
    x = jnp.dot(f_ref[...], a, preferred_element_type=F32)
    if spectrum_only:
        norm_ref, or_ref, oi_ref = rest
        inv = 1.0 / norm_ref[...]
        or_ref[...] = x[:half] * inv
        oi_ref[...] = x[half:] * inv
        return
    g_ref, hr_ref, hi_ref, or_ref, oi_ref = rest
    xr, xi = x[:half], x[half:]
    hr, hi = hr_ref[...], hi_ref[...]
    y = jnp.concatenate([xr * hr - xi * hi, xr * hi + xi * hr], axis=0).astype(BF16)
    z = jnp.dot(g_ref[...], y, preferred_element_type=F32)
    or_ref[...] = z[:half].astype(or_ref.dtype)
    oi_ref[...] = z[half:].astype(oi_ref.dtype)


def hy_stage_b(ar, ai, fmat, gmat=None, hr=None, hi=None, order=None, norm=None):
    npair, n_a, rows, width = ar.shape
    spectrum_only = gmat is None
    aspec = pl.BlockSpec((None, None, rows, width), lambda k, p: (p, k, 0, 0))
    mspec = pl.BlockSpec((None, 2 * rows, 2 * rows), lambda k, p: (k, 0, 0))
    in_specs = [aspec, aspec, mspec]
    args = [ar, ai, fmat]
    if spectrum_only:
        in_specs.append(pl.BlockSpec((None, 1, width), lambda k, p: (p, 0, 0)))
        args.append(norm.reshape(npair, 1, width))
    else:
        hspec = pl.BlockSpec((None, None, rows, width), lambda k, p: (order, k, 0, 0))
        in_specs += [mspec, hspec, hspec]
        args += [gmat, hr, hi]
    out_dtype = F32 if spectrum_only else BF16
    return pl.pallas_call(
        functools.partial(_hy_stage_b_kernel, spectrum_only=spectrum_only), grid=(n_a, npair),
        in_specs=in_specs, out_specs=[aspec, aspec],
        out_shape=[jax.ShapeDtypeStruct(ar.shape, out_dtype)] * 2,
        compiler_params=_cparams(2), name="hyena_stage_b")(*args)


def _hy_stage_a_inv_kernel(ar_ref, ai_ref, w_ref, u_ref, g_ref, skip_ref, o_ref, *, tile0_u, tile0_g):
    n_a = ar_ref.shape[0]
    n_out = o_ref.shape[1]
    skip = skip_ref[...]
    for g in range(ar_ref.shape[1] // GRP):
        rows = slice(g * GRP, (g + 1) * GRP)
        rhs = jnp.concatenate([ar_ref[k, rows, :] for k in range(n_a)]
                              + [ai_ref[k, rows, :] for k in range(n_a)], axis=0)
        out = jnp.dot(w_ref[...], rhs, preferred_element_type=F32)
        for part in range(2):
            for a in range(n_out):
                conv = (out[(part * n_out + a) * GRP:(part * n_out + a + 1) * GRP]
                        + skip * u_ref[part, tile0_u + a, rows, :].astype(F32))
                gate = g_ref[part, tile0_g + a, rows, :].astype(F32)
                o_ref[part, a, rows, :] = (gate * conv).astype(o_ref.dtype)


def hy_stage_a_inv(ar, ai, wmat, u5, u_col0, tile0_u, g5, g_col0, tile0_g, skip, out_dtype):
    npair, n_a, rows, width = ar.shape
    n_out = n_a // 2
    assert wmat.shape == (2 * n_out * GRP, 2 * n_a * GRP)
    rb, cb = 64, 256
    kern = functools.partial(_hy_stage_a_inv_kernel, tile0_u=tile0_u, tile0_g=tile0_g)
    aspec = pl.BlockSpec((None, n_a, rb, cb), lambda p, c, r: (p, 0, r, c))
    return pl.pallas_call(
        kern, grid=(npair, width // cb, rows // rb),
        in_specs=[aspec, aspec, pl.BlockSpec(wmat.shape, lambda p, c, r: (0, 0)),
                  pl.BlockSpec((2, None, u5.shape[2], rb, cb), lambda p, c, r: (0, p, 0, r, u_col0 // cb + c)),
                  pl.BlockSpec((2, None, g5.shape[2], rb, cb), lambda p, c, r: (0, p, 0, r, g_col0 // cb + c)),
                  pl.BlockSpec((1, cb), lambda p, c, r: (0, c))],
        out_specs=pl.BlockSpec((2, None, n_out, rb, cb), lambda p, c, r: (0, p, 0, r, c)),
        out_shape=jax.ShapeDtypeStruct((2, npair, n_out, rows, width), out_dtype),
        compiler_params=_cparams(3), name="hyena_stage_a_inv")(ar, ai, wmat, u5, g5, skip.reshape(1, width))


def stage_a_matrices(n_a):
    h = n_a // 2
    idx = jnp.arange(n_a, dtype=jnp.int32)
    ang = (-2.0 * math.pi / n_a) * ((idx[:, None] * idx[None, :]) % n_a).astype(F32)
    wr, wi = jnp.cos(ang), jnp.sin(ang)
    eye = jnp.eye(GRP, dtype=F32)

    def expand(blocks):
        return jnp.kron(jnp.block(blocks), eye).astype(BF16)

    fwd = expand([[wr[:, :h], -wi[:, :h]], [wi[:, :h], wr[:, :h]]])
    fwd_real = expand([[wr], [wi]])
    inv = expand([[wr[:h], wi[:h]], [-wi[:h], wr[:h]]])
    return fwd, fwd_real, inv


def dft_matrices(n_a):
    n = n_a * DFT_ROWS
    kb = jnp.arange(DFT_ROWS, dtype=jnp.int32)[None, :, None]
    b = jnp.arange(DFT_ROWS, dtype=jnp.int32)[None, None, :]
    ka = jnp.arange(n_a, dtype=jnp.int32)[:, None, None]
    ang = (-2.0 * math.pi / n) * ((b * (ka + n_a * kb)) % n).astype(F32)
    fr, fi = jnp.cos(ang), jnp.sin(ang)
    blk = jnp.concatenate([jnp.concatenate([fr, -fi], axis=2), jnp.concatenate([fi, fr], axis=2)], axis=1)
    return blk.astype(BF16), (jnp.swapaxes(blk, 1, 2) * (1.0 / n)).astype(BF16)


def hyena_branch(u5, width, tile0, n_tiles, fargs, skip, dft):
    length = n_tiles * DFT_ROWS
    n_a = 2 * n_tiles
    fmat, gmat, a_fwd, a_fwd_real, a_inv = dft
    hc, norm = hyena_conv_filters(length, *fargs)
    hc = hc.reshape(1, HY_ORDER, n_a, DFT_ROWS, width)
    hr, hi = hy_stage_b(*hy_stage_a(hc, a_fwd_real, 0, width, 0, n_a), fmat, norm=norm)
    x5, x_col0, x_tile0 = u5, 0, tile0
    out = None
    for order in range(HY_ORDER):
        ar, ai = hy_stage_a(x5, a_fwd, x_col0, width, x_tile0, n_tiles)
        ar, ai = hy_stage_b(ar, ai, fmat, gmat, hr, hi, order=order)
        out = hy_stage_a_inv(ar, ai, a_inv, x5, x_col0, x_tile0, u5, (order + 1) * width, tile0, skip[order],
                             BF16)
        x5, x_col0, x_tile0 = out, 0, 0
    return out


def kernel(x, c, ctx, c_ctx, w_mod, b_mod, g_mix, g_ffn, w_in, b_gate, w_br, w_out, da_lambda, da_subln_g,
           lru_conv_w, lru_conv_b, lru_wa, lru_ba, lru_wi, lru_bi, lru_lambda, hy_conv_w, hy_conv_b,
           hy_f_w1, hy_f_b1, hy_f_w2, hy_f_b2, hy_f_freq, hy_f_w3, hy_decay, hy_skip, ffn_w_gate, ffn_w_up,
           ffn_w_down, moe_router, moe_w_gate, moe_w_up, moe_w_down, g_final):
    n_batch, length, d = x.shape
    n_ctx = ctx.shape[1]
    depth = w_mod.shape[0]
    assert n_ctx == ROW_G and length % ROW_G == 0 and d % 128 == 0
    tb = n_ctx + length
    gpb = tb // ROW_G
    nt = n_batch * tb
    c_end = w_in.shape[2]
    cb_k, cb_v, cb_lx, cb_q, cb_ly, cb_hy, cb_g = 0, 1, 2, 3, 4, 5, 8
    assert c_end == 11 * d

    xs = jnp.concatenate([ctx, x], axis=1).reshape(nt, d)
    silu_rows = jnp.concatenate([jax.nn.silu(c), jax.nn.silu(c_ctx)[None, :],
                                 jnp.zeros((16 - n_batch - 1, d), F32)], axis=0)
    tables = rope_tables(length, n_ctx, ROPE_LANES)
    dft_ctx = dft_matrices(2) + stage_a_matrices(2)
    dft_lat = dft_matrices(2 * (gpb - 1)) + stage_a_matrices(2 * (gpb - 1))

    for li in range(depth):
        lam_init = 0.8 - 0.6 * math.exp(-0.3 * li)
        tab = mm_f32_bias(silu_rows, w_mod[li], b_mod[li], d)
        tab = tab.reshape(16, 6, d).transpose(1, 0, 2)

        h = norm_mod(xs, g_mix[li], tab, 0, 1, gpb, n_batch)
        p = in_proj(h, w_in[li].astype(BF16), tables, gpb, cb_k, cb_q)
        a_out = diff_attention(p, d, da_lambda[li], da_subln_g[li], lam_init, n_batch, gpb, cb_q, cb_k, cb_v)

        bw = d // LRU_BLOCKS
        r_out = None
        hf = None
        for direction in range(2):
            w_cat = jnp.concatenate([lru_wa[li, direction], lru_wi[li, direction]], axis=-1).astype(BF16)
            res = lru_pass(p, lru_conv_w[li], lru_conv_b[li], w_cat, lru_ba[li, direction],
                           lru_bi[li, direction], lru_lambda[li, direction], n_batch, gpb, cb_lx,
                           reverse=direction == 1, hf=hf, cb_y=cb_ly)
            if direction == 0:
                hf = res
            else:
                r_out = res
        del bw

        u = hyena_short_conv(p, hy_conv_w[li], hy_conv_b[li], n_batch, gpb, cb_hy, d)
        u5 = u.reshape(2, n_batch // 2, gpb, ROW_G, 3 * d)
        fargs = (hy_f_w1[li], hy_f_b1[li], hy_f_w2[li], hy_f_b2[li], hy_f_freq[li], hy_f_w3[li], hy_decay[li])
        y_c = hyena_branch(u5, d, 0, 1, fargs, hy_skip[li], dft_ctx)
        y_l = hyena_branch(u5, d, 1, gpb - 1, fargs, hy_skip[li], dft_lat)
        y_out = jnp.concatenate([y_c, y_l], axis=2).reshape(nt, d)

        xs = merge_branches(xs, a_out, r_out, y_out, p, b_gate[li], w_br[li].astype(BF16),
                            w_out[li].astype(BF16), tab, 2, gpb, n_batch, cb_g)

        jj = li // 2
        if li % 2 == 0:
            f = norm_mod(xs, g_ffn[li], tab, 3, 4, gpb, n_batch)
            hh = gate_up(f, ffn_w_gate[jj].astype(BF16), ffn_w_up[jj].astype(BF16))
            xs = down_residual(hh, ffn_w_down[jj].astype(BF16), xs, tab, 5, gpb, n_batch)
        else:
            wr = jnp.concatenate([moe_router[jj], jnp.zeros((d, 128 - N_EXPERTS), F32)], axis=1)
            f, logits = norm_mod(xs, g_ffn[li], tab, 3, 4, gpb, n_batch, w_router=wr)
            xs = moe_layer(f, logits, moe_w_gate[jj].astype(BF16), moe_w_up[jj].astype(BF16),
                           moe_w_down[jj].astype(BF16), xs, tab, 5, gpb, n_batch)

    return final_norm(xs.reshape(n_batch, tb, d), g_final, n_ctx)
```

```python
import functools
import math

import jax
import jax.numpy as jnp
from jax import lax
from jax.experimental import pallas as pl
from jax.experimental.pallas import tpu as pltpu

F32 = jnp.float32
BF16 = jnp.bfloat16

EPS = 1e-6
ROW_G = 256
GRID_W = 64
ROPE_BASE = 10000.0
DA_HEADS = 8
DA_HEAD_DIM = 64
DA_V_DIM = 128
LRU_BLOCKS = 8
LRU_C = 8.0
HY_ORDER = 2
HY_BANDS = 16
N_EXPERTS = 8
TOP_K = 2
MOE_BLOCK = 256
HALO = 16
VMEM_LIMIT = 56 * 1024 * 1024


def _cparams(n_axes):
    return pltpu.CompilerParams(dimension_semantics=("arbitrary",) * n_axes,
                                vmem_limit_bytes=VMEM_LIMIT)


def _pick(n, prefs):
    for p in prefs:
        if n % p == 0:
            return p
    raise ValueError(f"no tile in {prefs} divides {n}")


def _mod_row(tab_ref, k, group, groups_per_batch, n_batch):
    b = group // groups_per_batch
    row = jnp.where(group % groups_per_batch == 0, n_batch, b)
    return tab_ref[k, pl.ds(row, 1), :]


def _mm_f32_kernel(a_ref, w_ref, b_ref, o_ref):
    o_ref[...] = jnp.dot(a_ref[...], w_ref[...], preferred_element_type=F32) + b_ref[...]


def mm_f32_bias(a, w, b, tn):
    m, k = a.shape
    n = w.shape[1]
    return pl.pallas_call(
        _mm_f32_kernel, grid=(n // tn,),
        in_specs=[pl.BlockSpec((m, k), lambda j: (0, 0)),
                  pl.BlockSpec((k, tn), lambda j: (0, j)),
                  pl.BlockSpec((1, tn), lambda j: (0, j))],
        out_specs=pl.BlockSpec((m, tn), lambda j: (0, j)),
        out_shape=jax.ShapeDtypeStruct((m, n), F32),
        compiler_params=_cparams(1), name="mod_matmul")(a, w, b.reshape(1, n))


def _norm_mod_kernel(x_ref, g_ref, tab_ref, *rest, k_shift, k_scale, gpb, n_batch, router):
    if router:
        wr_ref, o_ref, lg_ref = rest
    else:
        (o_ref,) = rest
    nsub = x_ref.shape[0] // ROW_G
    for s in range(nsub):
        grp = pl.program_id(0) * nsub + s
        rows = pl.ds(s * ROW_G, ROW_G)
        xs = x_ref[rows, :]
        y = xs * lax.rsqrt(jnp.mean(xs * xs, axis=-1, keepdims=True) + EPS) * g_ref[...]
        shift = _mod_row(tab_ref, k_shift, grp, gpb, n_batch)
        scale = _mod_row(tab_ref, k_scale, grp, gpb, n_batch)
        h = y * (1.0 + scale) + shift
        o_ref[rows, :] = h.astype(o_ref.dtype)
        if router:
            lg_ref[rows, :] = jnp.dot(h, wr_ref[...], preferred_element_type=F32,
                                      precision=lax.Precision.HIGHEST)


def norm_mod(x, g, tab, k_shift, k_scale, gpb, n_batch, w_router=None):
    nt, d = x.shape
    tm = _pick(nt, (1024, 512, 256))
    router = w_router is not None
    kern = functools.partial(_norm_mod_kernel, k_shift=k_shift, k_scale=k_scale, gpb=gpb,
                             n_batch=n_batch, router=router)
    in_specs = [pl.BlockSpec((tm, d), lambda i: (i, 0)),
                pl.BlockSpec((1, d), lambda i: (0, 0)),
                pl.BlockSpec(tab.shape, lambda i: (0, 0, 0))]
    out_specs = [pl.BlockSpec((tm, d), lambda i: (i, 0))]
    out_shape = [jax.ShapeDtypeStruct((nt, d), BF16)]
    args = [x, g.reshape(1, d), tab]
    if router:
        in_specs.append(pl.BlockSpec(w_router.shape, lambda i: (0, 0)))
        out_specs.append(pl.BlockSpec((tm, w_router.shape[1]), lambda i: (i, 0)))
        out_shape.append(jax.ShapeDtypeStruct((nt, w_router.shape[1]), F32))
        args.append(w_router)
    out = pl.pallas_call(kern, grid=(nt // tm,), in_specs=in_specs, out_specs=out_specs,
                         out_shape=out_shape, compiler_params=_cparams(1), name="norm_mod")(*args)
    return out if router else out[0]


def _final_norm_kernel(x_ref, g_ref, o_ref):
    xs = x_ref[...]
    o_ref[...] = xs * lax.rsqrt(jnp.mean(xs * xs, axis=-1, keepdims=True) + EPS) * g_ref[...]


def final_norm(x3, g, n_ctx):
    b, tb, d = x3.shape
    length = tb - n_ctx
    tm = ROW_G
    off = n_ctx // tm
    return pl.pallas_call(
        _final_norm_kernel, grid=(b, length // tm),
        in_specs=[pl.BlockSpec((None, tm, d), lambda i, j: (i, j + off, 0)),
                  pl.BlockSpec((1, d), lambda i, j: (0, 0))],
        out_specs=pl.BlockSpec((None, tm, d), lambda i, j: (i, j, 0)),
        out_shape=jax.ShapeDtypeStruct((b, length, d), F32),
        compiler_params=_cparams(2), name="final_norm")(x3, g.reshape(1, d))


ROPE_LANES = 128


def _in_proj_kernel(a_ref, w_ref, cos_ref, sa_ref, sb_ref, o_ref, *, cb_k, cb_q, q_scale, gpb):
    n = pl.program_id(0)
    is_rot = jnp.logical_or(n == cb_k, n == cb_q)

    @pl.when(jnp.logical_not(is_rot))
    def _():
        o_ref[...] = jnp.dot(a_ref[...], w_ref[...], preferred_element_type=F32).astype(o_ref.dtype)

    @pl.when(is_rot)
    def _():
        scale = jnp.where(n == cb_q, q_scale, 1.0)
        nsub = a_ref.shape[0] // ROW_G
        for s in range(nsub):
            rows = slice(s * ROW_G, (s + 1) * ROW_G)
            acc = jnp.dot(a_ref[rows, :], w_ref[...], preferred_element_type=F32)
            j = (pl.program_id(1) * nsub + s) % gpb
            trow = pl.ds(pl.multiple_of(j * ROW_G, ROW_G), ROW_G)
            cos, sa, sb = cos_ref[trow, :] * scale, sa_ref[trow, :] * scale, sb_ref[trow, :] * scale
            for c in range(acc.shape[1] // ROPE_LANES):
                t = acc[:, c * ROPE_LANES:(c + 1) * ROPE_LANES]
                rot = (t * cos + pltpu.roll(t, ROPE_LANES - 16, 1) * sa + pltpu.roll(t, 16, 1) * sb)
                o_ref[rows, c * ROPE_LANES:(c + 1) * ROPE_LANES] = rot.astype(o_ref.dtype)


def in_proj(h, w, tables, gpb, cb_k, cb_q):
    m, k = h.shape
    n = w.shape[1]
    tm = _pick(m, (1024, 512, 256))
    tn = k
    cos, sa, sb = tables
    kern = functools.partial(_in_proj_kernel, cb_k=cb_k, cb_q=cb_q, gpb=gpb,
                             q_scale=math.log2(math.e) * DA_HEAD_DIM ** -0.5)
    tspec = pl.BlockSpec(cos.shape, lambda j, i: (0, 0))
    return pl.pallas_call(
        kern, grid=(n // tn, m // tm),
        in_specs=[pl.BlockSpec((tm, k), lambda j, i: (i, 0)),
                  pl.BlockSpec((k, tn), lambda j, i: (0, j)), tspec, tspec, tspec],
        out_specs=pl.BlockSpec((tm, tn), lambda j, i: (i, j)),
        out_shape=jax.ShapeDtypeStruct((m, n), BF16),
        compiler_params=_cparams(2), name="in_proj")(h, w, cos, sa, sb)


def rope_tables(length, n_ctx, width):
    rows = length // GRID_W
    row = jnp.repeat(jnp.arange(rows, dtype=F32), GRID_W)
    col = jnp.tile(jnp.arange(GRID_W, dtype=F32), rows)
    n_freq = DA_HEAD_DIM // 4
    inv = ROPE_BASE ** (-jnp.arange(n_freq, dtype=F32) / n_freq)
    ar = row[:, None] * inv
    ac = col[:, None] * inv
    ang = jnp.concatenate([ar, ar, ac, ac], axis=-1)
    ang = jnp.concatenate([jnp.zeros((n_ctx, DA_HEAD_DIM), F32), ang], axis=0)
    reps = width // DA_HEAD_DIM
    cos = jnp.tile(jnp.cos(ang), (1, reps))
    sin = jnp.tile(jnp.sin(ang), (1, reps))
    first = (jnp.arange(width) % (2 * n_freq)) < n_freq
    sa = jnp.where(first[None, :], -sin, 0.0)
    sb = jnp.where(first[None, :], 0.0, sin)
    return cos, sa, sb


def _attn_kernel(lamv_ref, g_ref, q_ref, k_ref, v_ref, o_ref, vaug_ref, m_ref, acc_ref,
                 *, lam_init, n_ctx, tk):
    j = pl.program_id(2)
    dv = v_ref.shape[1]

    @pl.when(j == 0)
    def _():
        vaug_ref[:, :dv] = v_ref[...]
        vaug_ref[:, dv:] = jnp.ones((vaug_ref.shape[0], vaug_ref.shape[1] - dv), vaug_ref.dtype)

    q = q_ref[...]
    lane = lax.broadcasted_iota(jnp.int32, q.shape, 1)
    zero = jnp.zeros_like(q)
    qq = jnp.concatenate([jnp.where(lane < DA_HEAD_DIM, q, zero),
                          jnp.where(lane >= DA_HEAD_DIM, q, zero)], axis=0)
    tq = q.shape[0]

    def scores(start, size):
        kc = k_ref[pl.ds(start, size), :]
        return lax.dot_general(qq, kc, (((1,), (1,)), ((), ())), preferred_element_type=F32)

    def update(s, start, size, first=False):
        m_new = jnp.max(s, axis=-1, keepdims=True)
        if first:
            m_new = jnp.broadcast_to(m_new, m_ref.shape)
        else:
            m_prev = m_ref[...]
            m_new = jnp.maximum(m_prev, m_new)
        p = jnp.exp2(s - jnp.concatenate([m_new] * (size // 128), axis=1))
        pv = jnp.dot(p.astype(BF16), vaug_ref[pl.ds(start, size), :], preferred_element_type=F32)
        if first:
            acc_ref[...] = pv
        else:
            alpha = jnp.exp2(m_prev - m_new)
            acc_ref[...] = jnp.concatenate([alpha] * (acc_ref.shape[1] // 128), axis=1) * acc_ref[...] + pv
        m_ref[...] = m_new

    s_ctx = scores(0, n_ctx)

    @pl.when(j == 0)
    def _():
        update(s_ctx, 0, n_ctx, first=True)

    @pl.when(j > 0)
    def _():
        starts = [n_ctx + i * tk for i in range((k_ref.shape[0] - n_ctx) // tk)]
        s_next = scores(starts[0], tk)
        update(s_ctx, 0, n_ctx, first=True)
        for i, start in enumerate(starts):
            s_cur = s_next
            if i + 1 < len(starts):
                s_next = scores(starts[i + 1], tk)
            update(s_cur, start, tk)

    lv = lamv_ref[...]
    lam = (jnp.exp(jnp.sum(lv[0:1] * lv[1:2], axis=-1, keepdims=True))
           - jnp.exp(jnp.sum(lv[2:3] * lv[3:4], axis=-1, keepdims=True)) + lam_init)
    o = acc_ref[:, :dv] / acc_ref[:, dv:]
    o = o[:tq] - lam * o[tq:]
    o = o * lax.rsqrt(jnp.mean(o * o, axis=-1, keepdims=True) + EPS) * g_ref[...]
    o_ref[...] = (o * (1.0 - lam_init)).astype(o_ref.dtype)


def diff_attention(p, width, lam_vecs, subln_g, lam_init, n_batch, gpb, cb_q, cb_k, cb_v):
    nt = p.shape[0]
    tb = gpb * ROW_G
    n_ctx = ROW_G
    tk = _pick(tb - n_ctx, (2048, 1024, 512, 256))
    kern = functools.partial(_attn_kernel, lam_init=lam_init, n_ctx=n_ctx, tk=tk)
    assert DA_V_DIM == 128
    hpb = width // DA_V_DIM
    return pl.pallas_call(
        kern, grid=(n_batch, DA_HEADS, gpb),
        in_specs=[pl.BlockSpec(lam_vecs.shape, lambda b, h, j: (0, 0)),
                  pl.BlockSpec((1, DA_V_DIM), lambda b, h, j: (0, 0)),
                  pl.BlockSpec((ROW_G, DA_V_DIM), lambda b, h, j: (b * gpb + j, cb_q * hpb + h)),
                  pl.BlockSpec((tb, DA_V_DIM), lambda b, h, j: (b, cb_k * hpb + h)),
                  pl.BlockSpec((tb, DA_V_DIM), lambda b, h, j: (b, cb_v * hpb + h))],
        out_specs=pl.BlockSpec((ROW_G, DA_V_DIM), lambda b, h, j: (b * gpb + j, h)),
        out_shape=jax.ShapeDtypeStruct((nt, width), BF16),
        scratch_shapes=[pltpu.VMEM((tb, 2 * DA_V_DIM), BF16), pltpu.VMEM((2 * ROW_G, 128), F32),
                        pltpu.VMEM((2 * ROW_G, 2 * DA_V_DIM), F32)],
        compiler_params=_cparams(3), name="diff_attn")(
            lam_vecs, subln_g.reshape(1, DA_V_DIM), p, p, p)


def _shift_matrix(n, d):
    row = lax.broadcasted_iota(jnp.int32, (n, n), 0)
    col = lax.broadcasted_iota(jnp.int32, (n, n), 1)
    return jnp.where(col == row + d, 1.0, 0.0).astype(BF16)


def _dwconv(main_ref, prev_ref, next_ref, w_ref, b_ref, left, has_prev, has_next):
    x = main_ref[...]
    acc = b_ref[...] + w_ref[left:left + 1, :] * x.astype(F32)
    prev8 = prev_ref[...].astype(F32)[HALO - 8:, :]
    next8 = next_ref[...].astype(F32)[:8, :]
    sub = lax.broadcasted_iota(jnp.int32, (8, 1), 0)
    first = jnp.zeros_like(prev8)
    last = jnp.zeros_like(next8)
    for j in range(w_ref.shape[0]):
        d = j - left
        if d == 0:
            continue
        wj = w_ref[j:j + 1, :]
        acc = acc + wj * jnp.dot(_shift_matrix(ROW_G, d), x, preferred_element_type=F32)
        if d < 0:
            mask = jnp.logical_and(sub < -d, has_prev)
            first = first + wj * jnp.where(mask, pltpu.roll(prev8, -d, 0), 0.0)
        else:
            mask = jnp.logical_and(sub >= 8 - d, has_next)
            last = last + wj * jnp.where(mask, pltpu.roll(next8, 8 - d, 0), 0.0)
    return jnp.concatenate([acc[0:8] + first, acc[8:ROW_G - 8], acc[ROW_G - 8:] + last], axis=0)


def _halo_specs(width, cb, gpb, tile_of, n_groups):
    per = ROW_G // HALO

    def main(b, s, *_):
        return (b * gpb + tile_of(s), cb)

    def prev(b, s, *_):
        return (jnp.maximum((b * gpb + tile_of(s)) * per - 1, 0), cb)

    def nxt(b, s, *_):
        return (jnp.minimum((b * gpb + tile_of(s) + 1) * per, n_groups * per - 1), cb)

    return [pl.BlockSpec((ROW_G, width), main), pl.BlockSpec((HALO, width), prev),
            pl.BlockSpec((HALO, width), nxt)]


def _seq_flags(j, gpb):
    return j >= 2, jnp.logical_and(j >= 1, j <= gpb - 2)


SCAN_G = 8


def _scan_groups(a, b, reverse):
    shape = a.shape
    a = a.reshape(shape[0] // SCAN_G, SCAN_G, shape[1])
    b = b.reshape(a.shape)
    pos = lax.broadcasted_iota(jnp.int32, a.shape, 1)
    s = 1
    while s < SCAN_G:
        keep = pos < SCAN_G - s if reverse else pos >= s
        shift = SCAN_G - s if reverse else s
        a_sh = jnp.where(keep, pltpu.roll(a, shift, 1), 1.0)
        b_sh = jnp.where(keep, pltpu.roll(b, shift, 1), 0.0)
        b = a * b_sh + b
        a = a * a_sh
        s *= 2
    return a.reshape(shape), b.reshape(shape)


def _gelu_tanh(x):
    return 0.5 * x * (1.0 + jnp.tanh(math.sqrt(2.0 / math.pi) * (x + 0.044715 * (x * x * x))))


def _lru_kernel(main_ref, prev_ref, next_ref, cw_ref, cb_ref, w_ref, ba_ref, bi_ref, lam_ref, *rest,
                gpb, reverse):
    if reverse:
        hf_ref, ly_ref, o_ref, carry_ref = rest
    else:
        o_ref, carry_ref = rest
    s = pl.program_id(1)
    j = jnp.where(s == 0, 0, gpb - s) if reverse else s
    has_prev, has_next = _seq_flags(j, gpb)
    xc = _dwconv(main_ref, prev_ref, next_ref, cw_ref, cb_ref, 2, has_prev, has_next)
    xcb = xc.astype(BF16)
    bw = w_ref.shape[1]
    r_parts, i_parts = [], []
    for n in range(w_ref.shape[0]):
        res = jnp.dot(xcb[:, n * bw:(n + 1) * bw], w_ref[n], preferred_element_type=F32)
        r_parts.append(res[:, :bw])
        i_parts.append(res[:, bw:])
    r = jax.nn.sigmoid(jnp.concatenate(r_parts, axis=1) + ba_ref[...])
    gate_i = jax.nn.sigmoid(jnp.concatenate(i_parts, axis=1) + bi_ref[...])
    nl = -lam_ref[...]
    softplus = jnp.maximum(nl, 0.0) + jnp.log(1.0 + jnp.exp(-jnp.abs(nl)))
    a = jnp.exp((-LRU_C) * r * softplus)
    bb = jnp.sqrt(1.0 - a * a) * (gate_i * xc)
    a_grp, h_grp = _scan_groups(a, bb, reverse)

    @pl.when(s == 0)
    def _():
        carry_ref[...] = jnp.zeros(carry_ref.shape, F32)

    h = carry_ref[...]
    n_grp = ROW_G // SCAN_G
    pieces = [None] * n_grp
    for g in (reversed(range(n_grp)) if reverse else range(n_grp)):
        rows = slice(g * SCAN_G, (g + 1) * SCAN_G)
        hg = a_grp[rows] * h + h_grp[rows]
        pieces[g] = hg
        h = hg[0:1, :] if reverse else hg[SCAN_G - 1:SCAN_G, :]
    carry_ref[...] = h
    h_all = jnp.concatenate(pieces, axis=0)
    if reverse:
        ly = ly_ref[...].astype(F32)
        o_ref[...] = (_gelu_tanh(ly) * (hf_ref[...] + h_all)).astype(o_ref.dtype)
    else:
        o_ref[...] = h_all


def lru_pass(p, conv_w, conv_b, w_cat, ba, bi, lam, n_batch, gpb, cb_x, reverse, hf=None, cb_y=None):
    nt = p.shape[0]
    width = conv_w.shape[1]
    tile_of = (lambda s: jnp.where(s == 0, 0, gpb - s)) if reverse else (lambda s: s)
    kern = functools.partial(_lru_kernel, gpb=gpb, reverse=reverse)
    const2 = lambda b, s: (0, 0)
    in_specs = _halo_specs(width, cb_x, gpb, tile_of, nt // ROW_G) + [
        pl.BlockSpec(conv_w.shape, const2), pl.BlockSpec((1, width), const2),
        pl.BlockSpec(w_cat.shape, lambda b, s: (0, 0, 0)),
        pl.BlockSpec((1, width), const2), pl.BlockSpec((1, width), const2),
        pl.BlockSpec((1, width), const2)]
    args = [p, p, p, conv_w, conv_b.reshape(1, width), w_cat, ba.reshape(1, width),
            bi.reshape(1, width), lam.reshape(1, width)]
    row_spec = lambda cb: pl.BlockSpec((ROW_G, width), lambda b, s: (b * gpb + tile_of(s), cb))
    if reverse:
        in_specs += [row_spec(0), row_spec(cb_y)]
        args += [hf, p]
        out_dtype = BF16
    else:
        out_dtype = F32
    return pl.pallas_call(
        kern, grid=(n_batch, gpb), in_specs=in_specs, out_specs=row_spec(0),
        out_shape=jax.ShapeDtypeStruct((nt, width), out_dtype),
        scratch_shapes=[pltpu.VMEM((1, width), F32)],
        compiler_params=_cparams(2), name="lru_bwd" if reverse else "lru_fwd")(*args)


def _short_conv_kernel(main_ref, prev_ref, next_ref, cw_ref, cb_ref, o_ref, *, gpb):
    j = pl.program_id(1)
    has_prev, has_next = _seq_flags(j, gpb)
    o_ref[...] = _dwconv(main_ref, prev_ref, next_ref, cw_ref, cb_ref, 1, has_prev, has_next).astype(o_ref.dtype)


def hyena_short_conv(p, conv_w, conv_b, n_batch, gpb, cb0, width):
    nt = p.shape[0]
    total = conv_w.shape[1]
    ncb = total // width
    base = _halo_specs(width, 0, gpb, lambda s: s, nt // ROW_G)

    def shifted(spec):
        f = spec.index_map
        return pl.BlockSpec(spec.block_shape, lambda b, j, c: (f(b, j)[0], cb0 + c))

    kern = functools.partial(_short_conv_kernel, gpb=gpb)
    return pl.pallas_call(
        kern, grid=(n_batch, gpb, ncb),
        in_specs=[shifted(sp) for sp in base] + [
            pl.BlockSpec((conv_w.shape[0], width), lambda b, j, c: (0, c)),
            pl.BlockSpec((1, width), lambda b, j, c: (0, c))],
        out_specs=pl.BlockSpec((ROW_G, width), lambda b, j, c: (b * gpb + j, c)),
        out_shape=jax.ShapeDtypeStruct((nt, total), BF16),
        compiler_params=_cparams(3), name="hyena_short_conv")(p, p, p, conv_w, conv_b.reshape(1, total))


def _merge_kernel(a_ref, r_ref, y_ref, g0_ref, g1_ref, g2_ref, bg_ref, wbr_ref, wout_ref, tab_ref, x_ref,
                  o_ref, *, k_gate, gpb, n_batch):
    width = a_ref.shape[1]
    m = None
    for k, (br_ref, gt_ref) in enumerate(((a_ref, g0_ref), (r_ref, g1_ref), (y_ref, g2_ref))):
        gate = jax.nn.sigmoid(gt_ref[...].astype(F32) + bg_ref[:, k * width:(k + 1) * width])
        term = gate * jnp.dot(br_ref[...], wbr_ref[k], preferred_element_type=F32)
        m = term if m is None else m + term
    out = jnp.dot(m.astype(BF16), wout_ref[...], preferred_element_type=F32)
    nsub = x_ref.shape[0] // ROW_G
    for s in range(nsub):
        rows = pl.ds(s * ROW_G, ROW_G)
        gmod = _mod_row(tab_ref, k_gate, pl.program_id(0) * nsub + s, gpb, n_batch)
        o_ref[rows, :] = x_ref[rows, :] + gmod * out[s * ROW_G:(s + 1) * ROW_G, :]


def merge_branches(x, a, r, y, p, b_gate, w_br, w_out, tab, k_gate, gpb, n_batch, cb_g):
    nt, d = x.shape
    tm = _pick(nt, (512, 256))
    kern = functools.partial(_merge_kernel, k_gate=k_gate, gpb=gpb, n_batch=n_batch)
    row = pl.BlockSpec((tm, d), lambda i: (i, 0))
    gspec = lambda k: pl.BlockSpec((tm, d), lambda i: (i, cb_g + k))
    return pl.pallas_call(
        kern, grid=(nt // tm,),
        in_specs=[row, row, row, gspec(0), gspec(1), gspec(2),
                  pl.BlockSpec((1, 3 * d), lambda i: (0, 0)),
                  pl.BlockSpec(w_br.shape, lambda i: (0, 0, 0)),
                  pl.BlockSpec(w_out.shape, lambda i: (0, 0)),
                  pl.BlockSpec(tab.shape, lambda i: (0, 0, 0)), row],
        out_specs=row, out_shape=jax.ShapeDtypeStruct((nt, d), F32),
        compiler_params=_cparams(1), name="merge")(a, r, y, p, p, p, b_gate.reshape(1, 3 * d), w_br, w_out, tab, x)


def _gate_up_kernel(*refs, expert):
    if expert:
        _, f_ref, wg_ref, wu_ref, o_ref = refs
    else:
        f_ref, wg_ref, wu_ref, o_ref = refs
    f = f_ref[...]
    g = jnp.dot(f, wg_ref[...], preferred_element_type=F32)
    u = jnp.dot(f, wu_ref[...], preferred_element_type=F32)
    o_ref[...] = (g * jax.nn.sigmoid(g) * u).astype(o_ref.dtype)


def gate_up(f, wg, wu, block_e=None):
    m, d = f.shape
    dff = wg.shape[-1]
    tn = _pick(dff, (1408, 1024, 512, 256, 128))
    if block_e is None:
        tm = _pick(m, (512, 256))
        grid_spec = pltpu.PrefetchScalarGridSpec(
            num_scalar_prefetch=0, grid=(dff // tn, m // tm),
            in_specs=[pl.BlockSpec((tm, d), lambda j, i: (i, 0)),
                      pl.BlockSpec((d, tn), lambda j, i: (0, j)),
                      pl.BlockSpec((d, tn), lambda j, i: (0, j))],
            out_specs=pl.BlockSpec((tm, tn), lambda j, i: (i, j)))
        args = (f, wg, wu)
    else:
        tm = MOE_BLOCK
        grid_spec = pltpu.PrefetchScalarGridSpec(
            num_scalar_prefetch=1, grid=(dff // tn, m // tm),
            in_specs=[pl.BlockSpec((tm, d), lambda j, i, be: (i, 0)),
                      pl.BlockSpec((None, d, tn), lambda j, i, be: (be[i], 0, j)),
                      pl.BlockSpec((None, d, tn), lambda j, i, be: (be[i], 0, j))],
            out_specs=pl.BlockSpec((tm, tn), lambda j, i, be: (i, j)))
        args = (block_e, f, wg, wu)
    return pl.pallas_call(
        functools.partial(_gate_up_kernel, expert=block_e is not None), grid_spec=grid_spec,
        out_shape=jax.ShapeDtypeStruct((m, dff), BF16),
        compiler_params=_cparams(2), name="gate_up")(*args)


def _down_res_kernel(h_ref, wd_ref, tab_ref, x_ref, o_ref, *, k_gate, gpb, n_batch):
    out = jnp.dot(h_ref[...], wd_ref[...], preferred_element_type=F32)
    nsub = x_ref.shape[0] // ROW_G
    for s in range(nsub):
        rows = pl.ds(s * ROW_G, ROW_G)
        gmod = _mod_row(tab_ref, k_gate, pl.program_id(0) * nsub + s, gpb, n_batch)
        o_ref[rows, :] = x_ref[rows, :] + gmod * out[s * ROW_G:(s + 1) * ROW_G, :]


def down_residual(h, wd, x, tab, k_gate, gpb, n_batch):
    nt, d = x.shape
    dff = h.shape[1]
    tm = _pick(nt, (512, 256))
    kern = functools.partial(_down_res_kernel, k_gate=k_gate, gpb=gpb, n_batch=n_batch)
    return pl.pallas_call(
        kern, grid=(nt // tm,),
        in_specs=[pl.BlockSpec((tm, dff), lambda i: (i, 0)),
                  pl.BlockSpec((dff, d), lambda i: (0, 0)),
                  pl.BlockSpec(tab.shape, lambda i: (0, 0, 0)),
                  pl.BlockSpec((tm, d), lambda i: (i, 0))],
        out_specs=pl.BlockSpec((tm, d), lambda i: (i, 0)),
        out_shape=jax.ShapeDtypeStruct((nt, d), F32),
        compiler_params=_cparams(1), name="down_residual")(h, wd, tab, x)


def _expert_down_kernel(be_ref, h_ref, wd_ref, o_ref):
    o_ref[...] = jnp.dot(h_ref[...], wd_ref[...], preferred_element_type=F32)


def expert_down(h, wd, block_e):
    m, dff = h.shape
    d = wd.shape[-1]
    grid_spec = pltpu.PrefetchScalarGridSpec(
        num_scalar_prefetch=1, grid=(m // MOE_BLOCK,),
        in_specs=[pl.BlockSpec((MOE_BLOCK, dff), lambda i, be: (i, 0)),
                  pl.BlockSpec((None, dff, d), lambda i, be: (be[i], 0, 0))],
        out_specs=pl.BlockSpec((MOE_BLOCK, d), lambda i, be: (i, 0)))
    return pl.pallas_call(_expert_down_kernel, grid_spec=grid_spec,
                          out_shape=jax.ShapeDtypeStruct((m, d), F32),
                          compiler_params=_cparams(1), name="expert_down")(block_e, h, wd)


def _combine_kernel(y0_ref, y1_ref, w0_ref, w1_ref, tab_ref, x_ref, o_ref, *, k_gate, gpb, n_batch):
    nsub = x_ref.shape[0] // ROW_G
    for s in range(nsub):
        rows = pl.ds(s * ROW_G, ROW_G)
        gmod = _mod_row(tab_ref, k_gate, pl.program_id(0) * nsub + s, gpb, n_batch)
        y = y0_ref[rows, :] * w0_ref[rows, :] + y1_ref[rows, :] * w1_ref[rows, :]
        o_ref[rows, :] = x_ref[rows, :] + gmod * y


def moe_combine(y0, y1, w0, w1, x, tab, k_gate, gpb, n_batch):
    nt, d = x.shape
    tm = _pick(nt, (512, 256))
    kern = functools.partial(_combine_kernel, k_gate=k_gate, gpb=gpb, n_batch=n_batch)
    row = pl.BlockSpec((tm, d), lambda i: (i, 0))
    wsp = pl.BlockSpec((tm, 1), lambda i: (i, 0))
    return pl.pallas_call(
        kern, grid=(nt // tm,),
        in_specs=[row, row, wsp, wsp, pl.BlockSpec(tab.shape, lambda i: (0, 0, 0)), row],
        out_specs=row, out_shape=jax.ShapeDtypeStruct((nt, d), F32),
        compiler_params=_cparams(1), name="moe_combine")(y0, y1, w0, w1, tab, x)


def moe_layer(f, logits, w_gate, w_up, w_down, x, tab, k_gate, gpb, n_batch):
    nt = f.shape[0]
    n_asg = nt * TOP_K
    top_v, top_i = lax.top_k(logits[:, :N_EXPERTS], TOP_K)
    top_w = jax.nn.softmax(top_v, axis=-1)
    e_flat = top_i.reshape(-1)
    onehot = (e_flat[:, None] == jnp.arange(N_EXPERTS, dtype=e_flat.dtype)[None, :]).astype(jnp.int32)
    csum = jnp.cumsum(onehot, axis=0)
    counts = csum[-1]
    rank = jnp.take_along_axis(csum, e_flat[:, None], axis=1)[:, 0] - 1
    padded = ((counts + MOE_BLOCK - 1) // MOE_BLOCK) * MOE_BLOCK
    pad_end = jnp.cumsum(padded)
    pad_start = pad_end - padded
    dest = (pad_start[e_flat] + rank).astype(jnp.int32)
    n_blocks = -(-n_asg // MOE_BLOCK) + N_EXPERTS
    n_slots = n_blocks * MOE_BLOCK
    block_e = jnp.minimum(jnp.searchsorted(pad_end, jnp.arange(n_blocks, dtype=jnp.int32) * MOE_BLOCK,
                                           side='right'), N_EXPERTS - 1).astype(jnp.int32)
    order = jnp.argsort(e_flat, stable=True).astype(jnp.int32)
    e_slot = jnp.repeat(block_e, MOE_BLOCK)
    within = jnp.arange(n_slots, dtype=jnp.int32) - pad_start[e_slot].astype(jnp.int32)
    valid = within < counts[e_slot]
    src = jnp.where(valid, (jnp.cumsum(counts) - counts)[e_slot].astype(jnp.int32) + within, 0)
    slot_tok = jnp.where(valid, order[src] // TOP_K, 0)
    xs = jnp.take(f, slot_tok, axis=0)
    h = gate_up(xs, w_gate, w_up, block_e)
    ys = expert_down(h, w_down, block_e)
    dest2 = dest.reshape(nt, TOP_K)
    y0 = jnp.take(ys, dest2[:, 0], axis=0)
    y1 = jnp.take(ys, dest2[:, 1], axis=0)
    return moe_combine(y0, y1, top_w[:, 0:1], top_w[:, 1:2], x, tab, k_gate, gpb, n_batch)


HY_PAD = 128


def _hy_filter_kernel(z_ref, t_ref, w1_ref, b1_ref, w2_ref, b2_ref, fr_ref, w3_ref, dec_ref, hc_ref, norm_ref,
                      *, length):
    i = pl.program_id(0)
    rows = z_ref.shape[0]
    width = dec_ref.shape[1]
    hp = lax.Precision.HIGHEST
    fr = fr_ref[...]
    hdn = jnp.sin(fr * (jnp.dot(z_ref[...], w1_ref[...], precision=hp, preferred_element_type=F32) + b1_ref[...]))
    hdn = jnp.sin(fr * (jnp.dot(hdn, w2_ref[...], precision=hp, preferred_element_type=F32) + b2_ref[...]))
    w3 = w3_ref[...]
    h_hi, w_hi = hdn.astype(BF16), w3.astype(BF16)
    h_lo = (hdn - h_hi.astype(F32)).astype(BF16)
    w_lo = (w3 - w_hi.astype(F32)).astype(BF16)
    filt = (jnp.dot(h_hi, w_hi, preferred_element_type=F32) + jnp.dot(h_hi, w_lo, preferred_element_type=F32)
            + jnp.dot(h_lo, w_hi, preferred_element_type=F32))
    t = t_ref[...]
    r = i * rows + lax.broadcasted_iota(jnp.int32, (rows, 1), 0)
    first_half = r < length

    @pl.when(i == 0)
    def _():
        norm_ref[...] = jnp.full(norm_ref.shape, EPS, F32)

    for o in range(HY_ORDER):
        win = jnp.exp(-t * jnp.abs(dec_ref[o:o + 1, :]))
        fwd = filt[:, (2 * o) * width:(2 * o + 1) * width] * win
        bwd = filt[:, (2 * o + 1) * width:(2 * o + 2) * width] * win
        mass = jnp.where(first_half, jnp.abs(fwd) + jnp.abs(bwd), 0.0)
        norm_ref[o:o + 1, :] += jnp.sum(mass, axis=0, keepdims=True)
        hc = jnp.where(first_half, fwd, bwd) + jnp.where(r == 0, bwd, 0.0)
        hc_ref[o] = jnp.where(r == length, 0.0, hc)


def hyena_conv_filters(length, w1, b1, w2, b2, freq, w3, decay):
    n = 2 * length
    width = decay.shape[-1]
    r = jnp.arange(n, dtype=jnp.int32)
    tap = jnp.where(r < length, r, jnp.where(r == length, 0, n - r))
    t = jnp.linspace(0.0, 1.0, length, dtype=F32)[tap][:, None]
    w = 2.0 * math.pi * tap.astype(F32)[:, None] / length
    bands = jnp.linspace(1e-4, HY_BANDS - 1, HY_BANDS, dtype=F32)
    z = jnp.concatenate([t, jnp.cos(bands * w), -jnp.sin(bands * w)], axis=-1)
    assert max(w1.shape) <= HY_PAD

    def pad_to(x, shape):
        return jnp.pad(x, [(0, s - d) for s, d in zip(shape, x.shape)])

    args = (pad_to(z, (n, HY_PAD)), t, pad_to(w1, (HY_PAD, HY_PAD)), pad_to(b1[None, :], (1, HY_PAD)),
            pad_to(w2, (HY_PAD, HY_PAD)), pad_to(b2[None, :], (1, HY_PAD)), pad_to(freq[None, :], (1, HY_PAD)),
            pad_to(w3, (HY_PAD, w3.shape[1])), decay)
    const = lambda i: (0, 0)
    return pl.pallas_call(
        functools.partial(_hy_filter_kernel, length=length), grid=(n // DFT_ROWS,),
        in_specs=[pl.BlockSpec((DFT_ROWS, HY_PAD), lambda i: (i, 0)), pl.BlockSpec((DFT_ROWS, 1), lambda i: (i, 0)),
                  pl.BlockSpec((HY_PAD, HY_PAD), const), pl.BlockSpec((1, HY_PAD), const),
                  pl.BlockSpec((HY_PAD, HY_PAD), const), pl.BlockSpec((1, HY_PAD), const),
                  pl.BlockSpec((1, HY_PAD), const), pl.BlockSpec((HY_PAD, w3.shape[1]), const),
                  pl.BlockSpec(decay.shape, const)],
        out_specs=[pl.BlockSpec((HY_ORDER, DFT_ROWS, width), lambda i: (0, i, 0)),
                   pl.BlockSpec((HY_ORDER, width), const)],
        out_shape=[jax.ShapeDtypeStruct((HY_ORDER, n, width), F32),
                   jax.ShapeDtypeStruct((HY_ORDER, width), F32)],
        compiler_params=_cparams(1), name="hyena_filter")(*args)


DFT_ROWS = ROW_G
GRP = 8
HY_LANES = 256
HY_ROWS = 16


def _group_chunks(rows, width):
    return [(slice(r, r + HY_ROWS), slice(c, c + HY_LANES))
            for r in range(0, rows, HY_ROWS) for c in range(0, width, HY_LANES)]


def _grouped_matmul(w, pieces):
    pieces = [p.astype(F32) for p in pieces]
    outs = []
    for r in range(0, HY_ROWS, GRP):
        rhs = jnp.concatenate([p[r:r + GRP] for p in pieces], axis=0).astype(BF16)
        outs.append(jnp.dot(w, rhs, preferred_element_type=F32))
    return outs


def _piece(outs, idx):
    return jnp.concatenate([o[idx * GRP:(idx + 1) * GRP] for o in outs], axis=0)


def _hy_stage_a_kernel(x_ref, w_ref, ar_ref, ai_ref, *, tile0, n_in):
    planes = x_ref.shape[0]
    n_a = ar_ref.shape[0]
    for g, c in _group_chunks(ar_ref.shape[1], ar_ref.shape[2]):
        outs = _grouped_matmul(w_ref[...], [x_ref[q, tile0 + a, g, c] for q in range(planes) for a in range(n_in)])
        for k in range(n_a):
            ar_ref[k, g, c] = _piece(outs, k).astype(ar_ref.dtype)
            ai_ref[k, g, c] = _piece(outs, n_a + k).astype(ai_ref.dtype)


def hy_stage_a(x5, wmat, col0, width, tile0, n_in):
    planes, npair, ntile, rows, _ = x5.shape
    n_a = wmat.shape[0] // (2 * GRP)
    assert wmat.shape[1] == planes * n_in * GRP
    rb = HY_ROWS
    kern = functools.partial(_hy_stage_a_kernel, tile0=tile0, n_in=n_in)
    ospec = pl.BlockSpec((None, n_a, rb, width), lambda p, r: (p, 0, r, 0))
    return pl.pallas_call(
        kern, grid=(npair, rows // rb),
        in_specs=[pl.BlockSpec((planes, None, ntile, rb, width), lambda p, r: (0, p, 0, r, col0 // width)),
                  pl.BlockSpec(wmat.shape, lambda p, r: (0, 0))],
        out_specs=[ospec, ospec],
        out_shape=[jax.ShapeDtypeStruct((npair, n_a, rows, width), BF16)] * 2,
        compiler_params=_cparams(2), name="hyena_stage_a")(x5, wmat)


def _hy_stage_b_kernel(ar_ref, ai_ref, f_ref, *rest, spectrum_only):
    half = ar_ref.shape[0]
    a = jnp.concatenate([ar_ref[...], ai_ref[...]], axis=0)
    x = jnp.dot(f_ref[...], a, preferred_element_type=F32)
    if spectrum_only:
        norm_ref, or_ref, oi_ref = rest
        inv = 1.0 / norm_ref[...]
        or_ref[...] = x[:half] * inv
        oi_ref[...] = x[half:] * inv
        return
    g_ref, hr_ref, hi_ref, or_ref, oi_ref = rest
    xr, xi = x[:half], x[half:]
    hr, hi = hr_ref[...], hi_ref[...]
    y = jnp.concatenate([xr * hr - xi * hi, xr * hi + xi * hr], axis=0).astype(BF16)
    z = jnp.dot(g_ref[...], y, preferred_element_type=F32)
    or_ref[...] = z[:half].astype(or_ref.dtype)
    oi_ref[...] = z[half:].astype(oi_ref.dtype)


def hy_stage_b(ar, ai, fmat, gmat=None, hr=None, hi=None, order=None, norm=None):
    npair, n_a, rows, width = ar.shape
    spectrum_only = gmat is None
    aspec = pl.BlockSpec((None, None, rows, width), lambda k, p: (p, k, 0, 0))
    mspec = pl.BlockSpec((None, 2 * rows, 2 * rows), lambda k, p: (k, 0, 0))
    in_specs = [aspec, aspec, mspec]
    args = [ar, ai, fmat]
    if spectrum_only:
        in_specs.append(pl.BlockSpec((None, 1, width), lambda k, p: (p, 0, 0)))
        args.append(norm.reshape(npair, 1, width))
    else:
        hspec = pl.BlockSpec((None, None, rows, width), lambda k, p: (order, k, 0, 0))
        in_specs += [mspec, hspec, hspec]
        args += [gmat, hr, hi]
    out_dtype = F32 if spectrum_only else BF16
    return pl.pallas_call(
        functools.partial(_hy_stage_b_kernel, spectrum_only=spectrum_only), grid=(n_a, npair),
        in_specs=in_specs, out_specs=[aspec, aspec],
        out_shape=[jax.ShapeDtypeStruct(ar.shape, out_dtype)] * 2,
        compiler_params=_cparams(2), name="hyena_stage_b")(*args)


def _hy_stage_a_inv_kernel(ar_ref, ai_ref, w_ref, u_ref, g_ref, skip_ref, o_ref, *, tile0_u, tile0_g):
    n_a = ar_ref.shape[0]
    n_out = o_ref.shape[1]
    skip = skip_ref[...]
    for g, c in _group_chunks(ar_ref.shape[1], ar_ref.shape[2]):
        outs = _grouped_matmul(w_ref[...], [ar_ref[k, g, c] for k in range(n_a)]
                               + [ai_ref[k, g, c] for k in range(n_a)])
        for part in range(2):
            for a in range(n_out):
                conv = (_piece(outs, part * n_out + a)
                        + skip[:, c] * u_ref[part, tile0_u + a, g, c].astype(F32))
                gate = g_ref[part, tile0_g + a, g, c].astype(F32)
                o_ref[part, a, g, c] = (gate * conv).astype(o_ref.dtype)


def hy_stage_a_inv(ar, ai, wmat, u5, u_col0, tile0_u, g5, g_col0, tile0_g, skip, out_dtype):
    npair, n_a, rows, width = ar.shape
    n_out = n_a // 2
    assert wmat.shape == (2 * n_out * GRP, 2 * n_a * GRP)
    rb = HY_ROWS
    kern = functools.partial(_hy_stage_a_inv_kernel, tile0_u=tile0_u, tile0_g=tile0_g)
    aspec = pl.BlockSpec((None, n_a, rb, width), lambda p, r: (p, 0, r, 0))
    return pl.pallas_call(
        kern, grid=(npair, rows // rb),
        in_specs=[aspec, aspec, pl.BlockSpec(wmat.shape, lambda p, r: (0, 0)),
                  pl.BlockSpec((2, None, u5.shape[2], rb, width), lambda p, r: (0, p, 0, r, u_col0 // width)),
                  pl.BlockSpec((2, None, g5.shape[2], rb, width), lambda p, r: (0, p, 0, r, g_col0 // width)),
                  pl.BlockSpec((1, width), lambda p, r: (0, 0))],
        out_specs=pl.BlockSpec((2, None, n_out, rb, width), lambda p, r: (0, p, 0, r, 0)),
        out_shape=jax.ShapeDtypeStruct((2, npair, n_out, rows, width), out_dtype),
        compiler_params=_cparams(2), name="hyena_stage_a_inv")(ar, ai, wmat, u5, g5, skip.reshape(1, width))


def stage_a_matrices(n_a):
    h = n_a // 2
    idx = jnp.arange(n_a, dtype=jnp.int32)
    ang = (-2.0 * math.pi / n_a) * ((idx[:, None] * idx[None, :]) % n_a).astype(F32)
    wr, wi = jnp.cos(ang), jnp.sin(ang)
    eye = jnp.eye(GRP, dtype=F32)

    def expand(blocks):
        return jnp.kron(jnp.block(blocks), eye).astype(BF16)

    fwd = expand([[wr[:, :h], -wi[:, :h]], [wi[:, :h], wr[:, :h]]])
    fwd_real = expand([[wr], [wi]])
    inv = expand([[wr[:h], wi[:h]], [-wi[:h], wr[:h]]])
    return fwd, fwd_real, inv


def dft_matrices(n_a):
    n = n_a * DFT_ROWS
    kb = jnp.arange(DFT_ROWS, dtype=jnp.int32)[None, :, None]
    b = jnp.arange(DFT_ROWS, dtype=jnp.int32)[None, None, :]
    ka = jnp.arange(n_a, dtype=jnp.int32)[:, None, None]
    ang = (-2.0 * math.pi / n) * ((b * (ka + n_a * kb)) % n).astype(F32)
    fr, fi = jnp.cos(ang), jnp.sin(ang)
    blk = jnp.concatenate([jnp.concatenate([fr, -fi], axis=2), jnp.concatenate([fi, fr], axis=2)], axis=1)
    return blk.astype(BF16), (jnp.swapaxes(blk, 1, 2) * (1.0 / n)).astype(BF16)


def hyena_branch(u5, width, tile0, n_tiles, fargs, skip, dft):
    length = n_tiles * DFT_ROWS
    n_a = 2 * n_tiles
    fmat, gmat, a_fwd, a_fwd_real, a_inv = dft
    hc, norm = hyena_conv_filters(length, *fargs)
    hc = hc.reshape(1, HY_ORDER, n_a, DFT_ROWS, width)
    hr, hi = hy_stage_b(*hy_stage_a(hc, a_fwd_real, 0, width, 0, n_a), fmat, norm=norm)
    x5, x_col0, x_tile0 = u5, 0, tile0
    out = None
    for order in range(HY_ORDER):
        ar, ai = hy_stage_a(x5, a_fwd, x_col0, width, x_tile0, n_tiles)
        ar, ai = hy_stage_b(ar, ai, fmat, gmat, hr, hi, order=order)
        out = hy_stage_a_inv(ar, ai, a_inv, x5, x_col0, x_tile0, u5, (order + 1) * width, tile0, skip[order],
                             BF16)
        x5, x_col0, x_tile0 = out, 0, 0
    return out


def kernel(x, c, ctx, c_ctx, w_mod, b_mod, g_mix, g_ffn, w_in, b_gate, w_br, w_out, da_lambda, da_subln_g,
           lru_conv_w, lru_conv_b, lru_wa, lru_ba, lru_wi, lru_bi, lru_lambda, hy_conv_w, hy_conv_b,
           hy_f_w1, hy_f_b1, hy_f_w2, hy_f_b2, hy_f_freq, hy_f_w3, hy_decay, hy_skip, ffn_w_gate, ffn_w_up,
           ffn_w_down, moe_router, moe_w_gate, moe_w_up, moe_w_down, g_final):
    n_batch, length, d = x.shape
    n_ctx = ctx.shape[1]
    depth = w_mod.shape[0]
    assert n_ctx == ROW_G and length % ROW_G == 0 and d % 128 == 0
    tb = n_ctx + length
    gpb = tb // ROW_G
    nt = n_batch * tb
    c_end = w_in.shape[2]
    cb_k, cb_v, cb_lx, cb_q, cb_ly, cb_hy, cb_g = 0, 1, 2, 3, 4, 5, 8
    assert c_end == 11 * d

    xs = jnp.concatenate([ctx, x], axis=1).reshape(nt, d)
    silu_rows = jnp.concatenate([jax.nn.silu(c), jax.nn.silu(c_ctx)[None, :],
                                 jnp.zeros((16 - n_batch - 1, d), F32)], axis=0)
    tables = rope_tables(length, n_ctx, ROPE_LANES)
    dft_ctx = dft_matrices(2) + stage_a_matrices(2)
    dft_lat = dft_matrices(2 * (gpb - 1)) + stage_a_matrices(2 * (gpb - 1))

    for li in range(depth):
        lam_init = 0.8 - 0.6 * math.exp(-0.3 * li)
        tab = mm_f32_bias(silu_rows, w_mod[li], b_mod[li], d)
        tab = tab.reshape(16, 6, d).transpose(1, 0, 2)

        h = norm_mod(xs, g_mix[li], tab, 0, 1, gpb, n_batch)
        p = in_proj(h, w_in[li].astype(BF16), tables, gpb, cb_k, cb_q)
        a_out = diff_attention(p, d, da_lambda[li], da_subln_g[li], lam_init, n_batch, gpb, cb_q, cb_k, cb_v)

        bw = d // LRU_BLOCKS
        r_out = None
        hf = None
        for direction in range(2):
            w_cat = jnp.concatenate([lru_wa[li, direction], lru_wi[li, direction]], axis=-1).astype(BF16)
            res = lru_pass(p, lru_conv_w[li], lru_conv_b[li], w_cat, lru_ba[li, direction],
                           lru_bi[li, direction], lru_lambda[li, direction], n_batch, gpb, cb_lx,
                           reverse=direction == 1, hf=hf, cb_y=cb_ly)
            if direction == 0:
                hf = res
            else:
                r_out = res
        del bw

        u = hyena_short_conv(p, hy_conv_w[li], hy_conv_b[li], n_batch, gpb, cb_hy, d)
        u5 = u.reshape(2, n_batch // 2, gpb, ROW_G, 3 * d)
        fargs = (hy_f_w1[li], hy_f_b1[li], hy_f_w2[li], hy_f_b2[li], hy_f_freq[li], hy_f_w3[li], hy_decay[li])
        y_c = hyena_branch(u5, d, 0, 1, fargs, hy_skip[li], dft_ctx)
        y_l = hyena_branch(u5, d, 1, gpb - 1, fargs, hy_skip[li], dft_lat)
        y_out = jnp.concatenate([y_c, y_l], axis=2).reshape(nt, d)

        xs = merge_branches(xs, a_out, r_out, y_out, p, b_gate[li], w_br[li].astype(BF16),
                            w_out[li].astype(BF16), tab, 2, gpb, n_batch, cb_g)

        jj = li // 2
        if li % 2 == 0:
            f = norm_mod(xs, g_ffn[li], tab, 3, 4, gpb, n_batch)
            hh = gate_up(f, ffn_w_gate[jj].astype(BF16), ffn_w_up[jj].astype(BF16))
            xs = down_residual(hh, ffn_w_down[jj].astype(BF16), xs, tab, 5, gpb, n_batch)
        else:
            wr = jnp.concatenate([moe_router[jj], jnp.zeros((d, 128 - N_EXPERTS), F32)], axis=1)
            f, logits = norm_mod(xs, g_ffn[li], tab, 3, 4, gpb, n_batch, w_router=wr)
            xs = moe_layer(f, logits, moe_w_gate[jj].astype(BF16), moe_w_up[jj].astype(BF16),
                           moe_w_down[jj].astype(BF16), xs, tab, 5, gpb, n_batch)

    return final_norm(xs.reshape(n_batch, tb, d), g_final, n_ctx)
```

```python
import functools
import math

import jax
import jax.numpy as jnp
from jax import lax
from jax.experimental import pallas as pl
from jax.experimental.pallas import tpu as pltpu

F32 = jnp.float32
BF16 = jnp.bfloat16

EPS = 1e-6
ROW_G = 256
GRID_W = 64
ROPE_BASE = 10000.0
DA_HEADS = 8
DA_HEAD_DIM = 64
DA_V_DIM = 128
LRU_BLOCKS = 8
LRU_C = 8.0
HY_ORDER = 2
HY_BANDS = 16
N_EXPERTS = 8
TOP_K = 2
MOE_BLOCK = 256
HALO = 16
VMEM_LIMIT = 56 * 1024 * 1024


def _cparams(n_axes):
    return pltpu.CompilerParams(dimension_semantics=("arbitrary",) * n_axes,
                                vmem_limit_bytes=VMEM_LIMIT)


def _pick(n, prefs):
    for p in prefs:
        if n % p == 0:
            return p
    raise ValueError(f"no tile in {prefs} divides {n}")


def _mod_row(tab_ref, k, group, groups_per_batch, n_batch):
    b = group // groups_per_batch
    row = jnp.where(group % groups_per_batch == 0, n_batch, b)
    return tab_ref[k, pl.ds(row, 1), :]


def _mm_f32_kernel(a_ref, w_ref, b_ref, o_ref):
    o_ref[...] = jnp.dot(a_ref[...], w_ref[...], preferred_element_type=F32) + b_ref[...]


def mm_f32_bias(a, w, b, tn):
    m, k = a.shape
    n = w.shape[1]
    return pl.pallas_call(
        _mm_f32_kernel, grid=(n // tn,),
        in_specs=[pl.BlockSpec((m, k), lambda j: (0, 0)),
                  pl.BlockSpec((k, tn), lambda j: (0, j)),
                  pl.BlockSpec((1, tn), lambda j: (0, j))],
        out_specs=pl.BlockSpec((m, tn), lambda j: (0, j)),
        out_shape=jax.ShapeDtypeStruct((m, n), F32),
        compiler_params=_cparams(1), name="mod_matmul")(a, w, b.reshape(1, n))


def _norm_mod_kernel(x_ref, g_ref, tab_ref, *rest, k_shift, k_scale, gpb, n_batch, router):
    if router:
        wr_ref, o_ref, lg_ref = rest
    else:
        (o_ref,) = rest
    nsub = x_ref.shape[0] // ROW_G
    for s in range(nsub):
        grp = pl.program_id(0) * nsub + s
        rows = pl.ds(s * ROW_G, ROW_G)
        xs = x_ref[rows, :]
        y = xs * lax.rsqrt(jnp.mean(xs * xs, axis=-1, keepdims=True) + EPS) * g_ref[...]
        shift = _mod_row(tab_ref, k_shift, grp, gpb, n_batch)
        scale = _mod_row(tab_ref, k_scale, grp, gpb, n_batch)
        h = y * (1.0 + scale) + shift
        o_ref[rows, :] = h.astype(o_ref.dtype)
        if router:
            lg_ref[rows, :] = jnp.dot(h, wr_ref[...], preferred_element_type=F32,
                                      precision=lax.Precision.HIGHEST)


def norm_mod(x, g, tab, k_shift, k_scale, gpb, n_batch, w_router=None):
    nt, d = x.shape
    tm = _pick(nt, (1024, 512, 256))
    router = w_router is not None
    kern = functools.partial(_norm_mod_kernel, k_shift=k_shift, k_scale=k_scale, gpb=gpb,
                             n_batch=n_batch, router=router)
    in_specs = [pl.BlockSpec((tm, d), lambda i: (i, 0)),
                pl.BlockSpec((1, d), lambda i: (0, 0)),
                pl.BlockSpec(tab.shape, lambda i: (0, 0, 0))]
    out_specs = [pl.BlockSpec((tm, d), lambda i: (i, 0))]
    out_shape = [jax.ShapeDtypeStruct((nt, d), BF16)]
    args = [x, g.reshape(1, d), tab]
    if router:
        in_specs.append(pl.BlockSpec(w_router.shape, lambda i: (0, 0)))
        out_specs.append(pl.BlockSpec((tm, w_router.shape[1]), lambda i: (i, 0)))
        out_shape.append(jax.ShapeDtypeStruct((nt, w_router.shape[1]), F32))
        args.append(w_router)
    out = pl.pallas_call(kern, grid=(nt // tm,), in_specs=in_specs, out_specs=out_specs,
                         out_shape=out_shape, compiler_params=_cparams(1), name="norm_mod")(*args)
    return out if router else out[0]


def _final_norm_kernel(x_ref, g_ref, o_ref):
    xs = x_ref[...]
    o_ref[...] = xs * lax.rsqrt(jnp.mean(xs * xs, axis=-1, keepdims=True) + EPS) * g_ref[...]


def final_norm(x3, g, n_ctx):
    b, tb, d = x3.shape
    length = tb - n_ctx
    tm = ROW_G
    off = n_ctx // tm
    return pl.pallas_call(
        _final_norm_kernel, grid=(b, length // tm),
        in_specs=[pl.BlockSpec((None, tm, d), lambda i, j: (i, j + off, 0)),
                  pl.BlockSpec((1, d), lambda i, j: (0, 0))],
        out_specs=pl.BlockSpec((None, tm, d), lambda i, j: (i, j, 0)),
        out_shape=jax.ShapeDtypeStruct((b, length, d), F32),
        compiler_params=_cparams(2), name="final_norm")(x3, g.reshape(1, d))


ROPE_LANES = 128


def _in_proj_kernel(a_ref, w_ref, cos_ref, sa_ref, sb_ref, o_ref, *, cb_k, cb_q, q_scale, gpb):
    n = pl.program_id(0)
    is_rot = jnp.logical_or(n == cb_k, n == cb_q)

    @pl.when(jnp.logical_not(is_rot))
    def _():
        o_ref[...] = jnp.dot(a_ref[...], w_ref[...], preferred_element_type=F32).astype(o_ref.dtype)

    @pl.when(is_rot)
    def _():
        scale = jnp.where(n == cb_q, q_scale, 1.0)
        nsub = a_ref.shape[0] // ROW_G
        for s in range(nsub):
            rows = slice(s * ROW_G, (s + 1) * ROW_G)
            acc = jnp.dot(a_ref[rows, :], w_ref[...], preferred_element_type=F32)
            j = (pl.program_id(1) * nsub + s) % gpb
            trow = pl.ds(pl.multiple_of(j * ROW_G, ROW_G), ROW_G)
            cos, sa, sb = cos_ref[trow, :] * scale, sa_ref[trow, :] * scale, sb_ref[trow, :] * scale
            for c in range(acc.shape[1] // ROPE_LANES):
                t = acc[:, c * ROPE_LANES:(c + 1) * ROPE_LANES]
                rot = (t * cos + pltpu.roll(t, ROPE_LANES - 16, 1) * sa + pltpu.roll(t, 16, 1) * sb)
                o_ref[rows, c * ROPE_LANES:(c + 1) * ROPE_LANES] = rot.astype(o_ref.dtype)


def in_proj(h, w, tables, gpb, cb_k, cb_q):
    m, k = h.shape
    n = w.shape[1]
    tm = _pick(m, (1024, 512, 256))
    tn = k
    cos, sa, sb = tables
    kern = functools.partial(_in_proj_kernel, cb_k=cb_k, cb_q=cb_q, gpb=gpb,
                             q_scale=math.log2(math.e) * DA_HEAD_DIM ** -0.5)
    tspec = pl.BlockSpec(cos.shape, lambda j, i: (0, 0))
    return pl.pallas_call(
        kern, grid=(n // tn, m // tm),
        in_specs=[pl.BlockSpec((tm, k), lambda j, i: (i, 0)),
                  pl.BlockSpec((k, tn), lambda j, i: (0, j)), tspec, tspec, tspec],
        out_specs=pl.BlockSpec((tm, tn), lambda j, i: (i, j)),
        out_shape=jax.ShapeDtypeStruct((m, n), BF16),
        compiler_params=_cparams(2), name="in_proj")(h, w, cos, sa, sb)


def rope_tables(length, n_ctx, width):
    rows = length // GRID_W
    row = jnp.repeat(jnp.arange(rows, dtype=F32), GRID_W)
    col = jnp.tile(jnp.arange(GRID_W, dtype=F32), rows)
    n_freq = DA_HEAD_DIM // 4
    inv = ROPE_BASE ** (-jnp.arange(n_freq, dtype=F32) / n_freq)
    ar = row[:, None] * inv
    ac = col[:, None] * inv
    ang = jnp.concatenate([ar, ar, ac, ac], axis=-1)
    ang = jnp.concatenate([jnp.zeros((n_ctx, DA_HEAD_DIM), F32), ang], axis=0)
    reps = width // DA_HEAD_DIM
    cos = jnp.tile(jnp.cos(ang), (1, reps))
    sin = jnp.tile(jnp.sin(ang), (1, reps))
    first = (jnp.arange(width) % (2 * n_freq)) < n_freq
    sa = jnp.where(first[None, :], -sin, 0.0)
    sb = jnp.where(first[None, :], 0.0, sin)
    return cos, sa, sb


def _attn_kernel(lamv_ref, g_ref, q_ref, k_ref, v_ref, o_ref, vaug_ref, m_ref, acc_ref,
                 *, lam_init, n_ctx, tk):
    j = pl.program_id(2)
    dv = v_ref.shape[1]

    @pl.when(j == 0)
    def _():
        vaug_ref[:, :dv] = v_ref[...]
        vaug_ref[:, dv:] = jnp.ones((vaug_ref.shape[0], vaug_ref.shape[1] - dv), vaug_ref.dtype)

    q = q_ref[...]
    lane = lax.broadcasted_iota(jnp.int32, q.shape, 1)
    zero = jnp.zeros_like(q)
    qq = jnp.concatenate([jnp.where(lane < DA_HEAD_DIM, q, zero),
                          jnp.where(lane >= DA_HEAD_DIM, q, zero)], axis=0)
    tq = q.shape[0]

    def scores(start, size):
        kc = k_ref[pl.ds(start, size), :]
        return lax.dot_general(qq, kc, (((1,), (1,)), ((), ())), preferred_element_type=F32)

    def update(s, start, size, first=False):
        m_new = jnp.max(s, axis=-1, keepdims=True)
        if first:
            m_new = jnp.broadcast_to(m_new, m_ref.shape)
        else:
            m_prev = m_ref[...]
            m_new = jnp.maximum(m_prev, m_new)
        p = jnp.exp2(s - jnp.concatenate([m_new] * (size // 128), axis=1))
        pv = jnp.dot(p.astype(BF16), vaug_ref[pl.ds(start, size), :], preferred_element_type=F32)
        if first:
            acc_ref[...] = pv
        else:
            alpha = jnp.exp2(m_prev - m_new)
            acc_ref[...] = jnp.concatenate([alpha] * (acc_ref.shape[1] // 128), axis=1) * acc_ref[...] + pv
        m_ref[...] = m_new

    s_ctx = scores(0, n_ctx)

    @pl.when(j == 0)
    def _():
        update(s_ctx, 0, n_ctx, first=True)

    @pl.when(j > 0)
    def _():
        starts = [n_ctx + i * tk for i in range((k_ref.shape[0] - n_ctx) // tk)]
        s_next = scores(starts[0], tk)
        update(s_ctx, 0, n_ctx, first=True)
        for i, start in enumerate(starts):
            s_cur = s_next
            if i + 1 < len(starts):
                s_next = scores(starts[i + 1], tk)
            update(s_cur, start, tk)

    lv = lamv_ref[...]
    lam = (jnp.exp(jnp.sum(lv[0:1] * lv[1:2], axis=-1, keepdims=True))
           - jnp.exp(jnp.sum(lv[2:3] * lv[3:4], axis=-1, keepdims=True)) + lam_init)
    o = acc_ref[:, :dv] / acc_ref[:, dv:]
    o = o[:tq] - lam * o[tq:]
    o = o * lax.rsqrt(jnp.mean(o * o, axis=-1, keepdims=True) + EPS) * g_ref[...]
    o_ref[...] = (o * (1.0 - lam_init)).astype(o_ref.dtype)


def diff_attention(p, width, lam_vecs, subln_g, lam_init, n_batch, gpb, cb_q, cb_k, cb_v):
    nt = p.shape[0]
    tb = gpb * ROW_G
    n_ctx = ROW_G
    tk = _pick(tb - n_ctx, (2048, 1024, 512, 256))
    kern = functools.partial(_attn_kernel, lam_init=lam_init, n_ctx=n_ctx, tk=tk)
    assert DA_V_DIM == 128
    hpb = width // DA_V_DIM
    return pl.pallas_call(
        kern, grid=(n_batch, DA_HEADS, gpb),
        in_specs=[pl.BlockSpec(lam_vecs.shape, lambda b, h, j: (0, 0)),
                  pl.BlockSpec((1, DA_V_DIM), lambda b, h, j: (0, 0)),
                  pl.BlockSpec((ROW_G, DA_V_DIM), lambda b, h, j: (b * gpb + j, cb_q * hpb + h)),
                  pl.BlockSpec((tb, DA_V_DIM), lambda b, h, j: (b, cb_k * hpb + h)),
                  pl.BlockSpec((tb, DA_V_DIM), lambda b, h, j: (b, cb_v * hpb + h))],
        out_specs=pl.BlockSpec((ROW_G, DA_V_DIM), lambda b, h, j: (b * gpb + j, h)),
        out_shape=jax.ShapeDtypeStruct((nt, width), BF16),
        scratch_shapes=[pltpu.VMEM((tb, 2 * DA_V_DIM), BF16), pltpu.VMEM((2 * ROW_G, 128), F32),
                        pltpu.VMEM((2 * ROW_G, 2 * DA_V_DIM), F32)],
        compiler_params=_cparams(3), name="diff_attn")(
            lam_vecs, subln_g.reshape(1, DA_V_DIM), p, p, p)


def _shift_matrix(n, d):
    row = lax.broadcasted_iota(jnp.int32, (n, n), 0)
    col = lax.broadcasted_iota(jnp.int32, (n, n), 1)
    return jnp.where(col == row + d, 1.0, 0.0).astype(BF16)


def _dwconv(main_ref, prev_ref, next_ref, w_ref, b_ref, left, has_prev, has_next):
    x = main_ref[...]
    acc = b_ref[...] + w_ref[left:left + 1, :] * x.astype(F32)
    prev8 = prev_ref[...].astype(F32)[HALO - 8:, :]
    next8 = next_ref[...].astype(F32)[:8, :]
    sub = lax.broadcasted_iota(jnp.int32, (8, 1), 0)
    first = jnp.zeros_like(prev8)
    last = jnp.zeros_like(next8)
    for j in range(w_ref.shape[0]):
        d = j - left
        if d == 0:
            continue
        wj = w_ref[j:j + 1, :]
        acc = acc + wj * jnp.dot(_shift_matrix(ROW_G, d), x, preferred_element_type=F32)
        if d < 0:
            mask = jnp.logical_and(sub < -d, has_prev)
            first = first + wj * jnp.where(mask, pltpu.roll(prev8, -d, 0), 0.0)
        else:
            mask = jnp.logical_and(sub >= 8 - d, has_next)
            last = last + wj * jnp.where(mask, pltpu.roll(next8, 8 - d, 0), 0.0)
    return jnp.concatenate([acc[0:8] + first, acc[8:ROW_G - 8], acc[ROW_G - 8:] + last], axis=0)


def _halo_specs(width, cb, gpb, tile_of, n_groups):
    per = ROW_G // HALO

    def main(b, s, *_):
        return (b * gpb + tile_of(s), cb)

    def prev(b, s, *_):
        return (jnp.maximum((b * gpb + tile_of(s)) * per - 1, 0), cb)

    def nxt(b, s, *_):
        return (jnp.minimum((b * gpb + tile_of(s) + 1) * per, n_groups * per - 1), cb)

    return [pl.BlockSpec((ROW_G, width), main), pl.BlockSpec((HALO, width), prev),
            pl.BlockSpec((HALO, width), nxt)]


def _seq_flags(j, gpb):
    return j >= 2, jnp.logical_and(j >= 1, j <= gpb - 2)


SCAN_G = 8


def _scan_groups(a, b, reverse):
    shape = a.shape
    a = a.reshape(shape[0] // SCAN_G, SCAN_G, shape[1])
    b = b.reshape(a.shape)
    pos = lax.broadcasted_iota(jnp.int32, a.shape, 1)
    s = 1
    while s < SCAN_G:
        keep = pos < SCAN_G - s if reverse else pos >= s
        shift = SCAN_G - s if reverse else s
        a_sh = jnp.where(keep, pltpu.roll(a, shift, 1), 1.0)
        b_sh = jnp.where(keep, pltpu.roll(b, shift, 1), 0.0)
        b = a * b_sh + b
        a = a * a_sh
        s *= 2
    return a.reshape(shape), b.reshape(shape)


def _gelu_tanh(x):
    return 0.5 * x * (1.0 + jnp.tanh(math.sqrt(2.0 / math.pi) * (x + 0.044715 * (x * x * x))))


def _lru_kernel(main_ref, prev_ref, next_ref, cw_ref, cb_ref, w_ref, ba_ref, bi_ref, lam_ref, *rest,
                gpb, reverse):
    if reverse:
        hf_ref, ly_ref, o_ref, carry_ref = rest
    else:
        o_ref, carry_ref = rest
    s = pl.program_id(1)
    j = jnp.where(s == 0, 0, gpb - s) if reverse else s
    has_prev, has_next = _seq_flags(j, gpb)
    xc = _dwconv(main_ref, prev_ref, next_ref, cw_ref, cb_ref, 2, has_prev, has_next)
    xcb = xc.astype(BF16)
    bw = w_ref.shape[1]
    r_parts, i_parts = [], []
    for n in range(w_ref.shape[0]):
        res = jnp.dot(xcb[:, n * bw:(n + 1) * bw], w_ref[n], preferred_element_type=F32)
        r_parts.append(res[:, :bw])
        i_parts.append(res[:, bw:])
    r = jax.nn.sigmoid(jnp.concatenate(r_parts, axis=1) + ba_ref[...])
    gate_i = jax.nn.sigmoid(jnp.concatenate(i_parts, axis=1) + bi_ref[...])
    nl = -lam_ref[...]
    softplus = jnp.maximum(nl, 0.0) + jnp.log(1.0 + jnp.exp(-jnp.abs(nl)))
    a = jnp.exp((-LRU_C) * r * softplus)
    bb = jnp.sqrt(1.0 - a * a) * (gate_i * xc)
    a_grp, h_grp = _scan_groups(a, bb, reverse)

    @pl.when(s == 0)
    def _():
        carry_ref[...] = jnp.zeros(carry_ref.shape, F32)

    h = carry_ref[...]
    n_grp = ROW_G // SCAN_G
    pieces = [None] * n_grp
    for g in (reversed(range(n_grp)) if reverse else range(n_grp)):
        rows = slice(g * SCAN_G, (g + 1) * SCAN_G)
        hg = a_grp[rows] * h + h_grp[rows]
        pieces[g] = hg
        h = hg[0:1, :] if reverse else hg[SCAN_G - 1:SCAN_G, :]
    carry_ref[...] = h
    h_all = jnp.concatenate(pieces, axis=0)
    if reverse:
        ly = ly_ref[...].astype(F32)
        o_ref[...] = (_gelu_tanh(ly) * (hf_ref[...] + h_all)).astype(o_ref.dtype)
    else:
        o_ref[...] = h_all


def lru_pass(p, conv_w, conv_b, w_cat, ba, bi, lam, n_batch, gpb, cb_x, reverse, hf=None, cb_y=None):
    nt = p.shape[0]
    width = conv_w.shape[1]
    tile_of = (lambda s: jnp.where(s == 0, 0, gpb - s)) if reverse else (lambda s: s)
    kern = functools.partial(_lru_kernel, gpb=gpb, reverse=reverse)
    const2 = lambda b, s: (0, 0)
    in_specs = _halo_specs(width, cb_x, gpb, tile_of, nt // ROW_G) + [
        pl.BlockSpec(conv_w.shape, const2), pl.BlockSpec((1, width), const2),
        pl.BlockSpec(w_cat.shape, lambda b, s: (0, 0, 0)),
        pl.BlockSpec((1, width), const2), pl.BlockSpec((1, width), const2),
        pl.BlockSpec((1, width), const2)]
    args = [p, p, p, conv_w, conv_b.reshape(1, width), w_cat, ba.reshape(1, width),
            bi.reshape(1, width), lam.reshape(1, width)]
    row_spec = lambda cb: pl.BlockSpec((ROW_G, width), lambda b, s: (b * gpb + tile_of(s), cb))
    if reverse:
        in_specs += [row_spec(0), row_spec(cb_y)]
        args += [hf, p]
        out_dtype = BF16
    else:
        out_dtype = F32
    return pl.pallas_call(
        kern, grid=(n_batch, gpb), in_specs=in_specs, out_specs=row_spec(0),
        out_shape=jax.ShapeDtypeStruct((nt, width), out_dtype),
        scratch_shapes=[pltpu.VMEM((1, width), F32)],
        compiler_params=_cparams(2), name="lru_bwd" if reverse else "lru_fwd")(*args)


def _short_conv_kernel(main_ref, prev_ref, next_ref, cw_ref, cb_ref, o_ref, *, gpb):
    j = pl.program_id(1)
    has_prev, has_next = _seq_flags(j, gpb)
    o_ref[...] = _dwconv(main_ref, prev_ref, next_ref, cw_ref, cb_ref, 1, has_prev, has_next).astype(o_ref.dtype)


def hyena_short_conv(p, conv_w, conv_b, n_batch, gpb, cb0, width):
    nt = p.shape[0]
    total = conv_w.shape[1]
    ncb = total // width
    base = _halo_specs(width, 0, gpb, lambda s: s, nt // ROW_G)

    def shifted(spec):
        f = spec.index_map
        return pl.BlockSpec(spec.block_shape, lambda b, j, c: (f(b, j)[0], cb0 + c))

    kern = functools.partial(_short_conv_kernel, gpb=gpb)
    return pl.pallas_call(
        kern, grid=(n_batch, gpb, ncb),
        in_specs=[shifted(sp) for sp in base] + [
            pl.BlockSpec((conv_w.shape[0], width), lambda b, j, c: (0, c)),
            pl.BlockSpec((1, width), lambda b, j, c: (0, c))],
        out_specs=pl.BlockSpec((ROW_G, width), lambda b, j, c: (b * gpb + j, c)),
        out_shape=jax.ShapeDtypeStruct((nt, total), BF16),
        compiler_params=_cparams(3), name="hyena_short_conv")(p, p, p, conv_w, conv_b.reshape(1, total))


def _merge_kernel(a_ref, r_ref, y_ref, g0_ref, g1_ref, g2_ref, bg_ref, wbr_ref, wout_ref, tab_ref, x_ref,
                  o_ref, *, k_gate, gpb, n_batch):
    width = a_ref.shape[1]
    m = None
    for k, (br_ref, gt_ref) in enumerate(((a_ref, g0_ref), (r_ref, g1_ref), (y_ref, g2_ref))):
        gate = jax.nn.sigmoid(gt_ref[...].astype(F32) + bg_ref[:, k * width:(k + 1) * width])
        term = gate * jnp.dot(br_ref[...], wbr_ref[k], preferred_element_type=F32)
        m = term if m is None else m + term
    out = jnp.dot(m.astype(BF16), wout_ref[...], preferred_element_type=F32)
    nsub = x_ref.shape[0] // ROW_G
    for s in range(nsub):
        rows = pl.ds(s * ROW_G, ROW_G)
        gmod = _mod_row(tab_ref, k_gate, pl.program_id(0) * nsub + s, gpb, n_batch)
        o_ref[rows, :] = x_ref[rows, :] + gmod * out[s * ROW_G:(s + 1) * ROW_G, :]


def merge_branches(x, a, r, y, p, b_gate, w_br, w_out, tab, k_gate, gpb, n_batch, cb_g):
    nt, d = x.shape
    tm = _pick(nt, (512, 256))
    kern = functools.partial(_merge_kernel, k_gate=k_gate, gpb=gpb, n_batch=n_batch)
    row = pl.BlockSpec((tm, d), lambda i: (i, 0))
    gspec = lambda k: pl.BlockSpec((tm, d), lambda i: (i, cb_g + k))
    return pl.pallas_call(
        kern, grid=(nt // tm,),
        in_specs=[row, row, row, gspec(0), gspec(1), gspec(2),
                  pl.BlockSpec((1, 3 * d), lambda i: (0, 0)),
                  pl.BlockSpec(w_br.shape, lambda i: (0, 0, 0)),
                  pl.BlockSpec(w_out.shape, lambda i: (0, 0)),
                  pl.BlockSpec(tab.shape, lambda i: (0, 0, 0)), row],
        out_specs=row, out_shape=jax.ShapeDtypeStruct((nt, d), F32),
        compiler_params=_cparams(1), name="merge")(a, r, y, p, p, p, b_gate.reshape(1, 3 * d), w_br, w_out, tab, x)


def _gate_up_kernel(*refs, expert):
    if expert:
        _, f_ref, wg_ref, wu_ref, o_ref = refs
    else:
        f_ref, wg_ref, wu_ref, o_ref = refs
    f = f_ref[...]
    g = jnp.dot(f, wg_ref[...], preferred_element_type=F32)
    u = jnp.dot(f, wu_ref[...], preferred_element_type=F32)
    o_ref[...] = (g * jax.nn.sigmoid(g) * u).astype(o_ref.dtype)


def gate_up(f, wg, wu, block_e=None):
    m, d = f.shape
    dff = wg.shape[-1]
    tn = _pick(dff, (1408, 1024, 512, 256, 128))
    if block_e is None:
        tm = _pick(m, (512, 256))
        grid_spec = pltpu.PrefetchScalarGridSpec(
            num_scalar_prefetch=0, grid=(dff // tn, m // tm),
            in_specs=[pl.BlockSpec((tm, d), lambda j, i: (i, 0)),
                      pl.BlockSpec((d, tn), lambda j, i: (0, j)),
                      pl.BlockSpec((d, tn), lambda j, i: (0, j))],
            out_specs=pl.BlockSpec((tm, tn), lambda j, i: (i, j)))
        args = (f, wg, wu)
    else:
        tm = MOE_BLOCK
        grid_spec = pltpu.PrefetchScalarGridSpec(
            num_scalar_prefetch=1, grid=(dff // tn, m // tm),
            in_specs=[pl.BlockSpec((tm, d), lambda j, i, be: (i, 0)),
                      pl.BlockSpec((None, d, tn), lambda j, i, be: (be[i], 0, j)),
                      pl.BlockSpec((None, d, tn), lambda j, i, be: (be[i], 0, j))],
            out_specs=pl.BlockSpec((tm, tn), lambda j, i, be: (i, j)))
        args = (block_e, f, wg, wu)
    return pl.pallas_call(
        functools.partial(_gate_up_kernel, expert=block_e is not None), grid_spec=grid_spec,
        out_shape=jax.ShapeDtypeStruct((m, dff), BF16),
        compiler_params=_cparams(2), name="gate_up")(*args)


def _down_res_kernel(h_ref, wd_ref, tab_ref, x_ref, o_ref, *, k_gate, gpb, n_batch):
    out = jnp.dot(h_ref[...], wd_ref[...], preferred_element_type=F32)
    nsub = x_ref.shape[0] // ROW_G
    for s in range(nsub):
        rows = pl.ds(s * ROW_G, ROW_G)
        gmod = _mod_row(tab_ref, k_gate, pl.program_id(0) * nsub + s, gpb, n_batch)
        o_ref[rows, :] = x_ref[rows, :] + gmod * out[s * ROW_G:(s + 1) * ROW_G, :]


def down_residual(h, wd, x, tab, k_gate, gpb, n_batch):
    nt, d = x.shape
    dff = h.shape[1]
    tm = _pick(nt, (512, 256))
    kern = functools.partial(_down_res_kernel, k_gate=k_gate, gpb=gpb, n_batch=n_batch)
    return pl.pallas_call(
        kern, grid=(nt // tm,),
        in_specs=[pl.BlockSpec((tm, dff), lambda i: (i, 0)),
                  pl.BlockSpec((dff, d), lambda i: (0, 0)),
                  pl.BlockSpec(tab.shape, lambda i: (0, 0, 0)),
                  pl.BlockSpec((tm, d), lambda i: (i, 0))],
        out_specs=pl.BlockSpec((tm, d), lambda i: (i, 0)),
        out_shape=jax.ShapeDtypeStruct((nt, d), F32),
        compiler_params=_cparams(1), name="down_residual")(h, wd, tab, x)


def _expert_down_kernel(be_ref, h_ref, wd_ref, o_ref):
    o_ref[...] = jnp.dot(h_ref[...], wd_ref[...], preferred_element_type=F32).astype(o_ref.dtype)


def expert_down(h, wd, block_e):
    m, dff = h.shape
    d = wd.shape[-1]
    grid_spec = pltpu.PrefetchScalarGridSpec(
        num_scalar_prefetch=1, grid=(m // MOE_BLOCK,),
        in_specs=[pl.BlockSpec((MOE_BLOCK, dff), lambda i, be: (i, 0)),
                  pl.BlockSpec((None, dff, d), lambda i, be: (be[i], 0, 0))],
        out_specs=pl.BlockSpec((MOE_BLOCK, d), lambda i, be: (i, 0)))
    return pl.pallas_call(_expert_down_kernel, grid_spec=grid_spec,
                          out_shape=jax.ShapeDtypeStruct((m, d), BF16),
                          compiler_params=_cparams(1), name="expert_down")(block_e, h, wd)


def _combine_kernel(y0_ref, y1_ref, w0_ref, w1_ref, tab_ref, x_ref, o_ref, *, k_gate, gpb, n_batch):
    nsub = x_ref.shape[0] // ROW_G
    for s in range(nsub):
        rows = pl.ds(s * ROW_G, ROW_G)
        gmod = _mod_row(tab_ref, k_gate, pl.program_id(0) * nsub + s, gpb, n_batch)
        y = (y0_ref[rows, :].astype(F32) * w0_ref[rows, :]
             + y1_ref[rows, :].astype(F32) * w1_ref[rows, :])
        o_ref[rows, :] = x_ref[rows, :] + gmod * y


def moe_combine(y0, y1, w0, w1, x, tab, k_gate, gpb, n_batch):
    nt, d = x.shape
    tm = _pick(nt, (512, 256))
    kern = functools.partial(_combine_kernel, k_gate=k_gate, gpb=gpb, n_batch=n_batch)
    row = pl.BlockSpec((tm, d), lambda i: (i, 0))
    wsp = pl.BlockSpec((tm, 1), lambda i: (i, 0))
    return pl.pallas_call(
        kern, grid=(nt // tm,),
        in_specs=[row, row, wsp, wsp, pl.BlockSpec(tab.shape, lambda i: (0, 0, 0)), row],
        out_specs=row, out_shape=jax.ShapeDtypeStruct((nt, d), F32),
        compiler_params=_cparams(1), name="moe_combine")(y0, y1, w0, w1, tab, x)


def moe_layer(f, logits, w_gate, w_up, w_down, x, tab, k_gate, gpb, n_batch):
    nt = f.shape[0]
    n_asg = nt * TOP_K
    top_v, top_i = lax.top_k(logits[:, :N_EXPERTS], TOP_K)
    top_w = jax.nn.softmax(top_v, axis=-1)
    e_flat = top_i.reshape(-1)
    onehot = (e_flat[:, None] == jnp.arange(N_EXPERTS, dtype=e_flat.dtype)[None, :]).astype(jnp.int32)
    csum = jnp.cumsum(onehot, axis=0)
    counts = csum[-1]
    rank = jnp.take_along_axis(csum, e_flat[:, None], axis=1)[:, 0] - 1
    padded = ((counts + MOE_BLOCK - 1) // MOE_BLOCK) * MOE_BLOCK
    pad_end = jnp.cumsum(padded)
    pad_start = pad_end - padded
    dest = (pad_start[e_flat] + rank).astype(jnp.int32)
    n_blocks = -(-n_asg // MOE_BLOCK) + N_EXPERTS
    n_slots = n_blocks * MOE_BLOCK
    block_e = jnp.minimum(jnp.searchsorted(pad_end, jnp.arange(n_blocks, dtype=jnp.int32) * MOE_BLOCK,
                                           side='right'), N_EXPERTS - 1).astype(jnp.int32)
    order = jnp.argsort(e_flat, stable=True).astype(jnp.int32)
    e_slot = jnp.repeat(block_e, MOE_BLOCK)
    within = jnp.arange(n_slots, dtype=jnp.int32) - pad_start[e_slot].astype(jnp.int32)
    valid = within < counts[e_slot]
    src = jnp.where(valid, (jnp.cumsum(counts) - counts)[e_slot].astype(jnp.int32) + within, 0)
    slot_tok = jnp.where(valid, order[src] // TOP_K, 0)
    xs = f.at[slot_tok].get(mode='promise_in_bounds')
    h = gate_up(xs, w_gate, w_up, block_e)
    ys = expert_down(h, w_down, block_e)
    dest2 = dest.reshape(nt, TOP_K)
    y0 = ys.at[dest2[:, 0]].get(mode='promise_in_bounds')
    y1 = ys.at[dest2[:, 1]].get(mode='promise_in_bounds')
    return moe_combine(y0, y1, top_w[:, 0:1], top_w[:, 1:2], x, tab, k_gate, gpb, n_batch)


HY_PAD = 128


def _hy_filter_kernel(z_ref, t_ref, w1_ref, b1_ref, w2_ref, b2_ref, fr_ref, w3_ref, dec_ref, hc_ref, norm_ref,
                      *, length):
    i = pl.program_id(0)
    rows = z_ref.shape[0]
    width = dec_ref.shape[1]
    hp = lax.Precision.HIGHEST
    fr = fr_ref[...]
    hdn = jnp.sin(fr * (jnp.dot(z_ref[...], w1_ref[...], precision=hp, preferred_element_type=F32) + b1_ref[...]))
    hdn = jnp.sin(fr * (jnp.dot(hdn, w2_ref[...], precision=hp, preferred_element_type=F32) + b2_ref[...]))
    w3 = w3_ref[...]
    h_hi, w_hi = hdn.astype(BF16), w3.astype(BF16)
    h_lo = (hdn - h_hi.astype(F32)).astype(BF16)
    w_lo = (w3 - w_hi.astype(F32)).astype(BF16)
    filt = (jnp.dot(h_hi, w_hi, preferred_element_type=F32) + jnp.dot(h_hi, w_lo, preferred_element_type=F32)
            + jnp.dot(h_lo, w_hi, preferred_element_type=F32))
    t = t_ref[...]
    r = i * rows + lax.broadcasted_iota(jnp.int32, (rows, 1), 0)
    first_half = r < length

    @pl.when(i == 0)
    def _():
        norm_ref[...] = jnp.full(norm_ref.shape, EPS, F32)

    for o in range(HY_ORDER):
        win = jnp.exp(-t * jnp.abs(dec_ref[o:o + 1, :]))
        fwd = filt[:, (2 * o) * width:(2 * o + 1) * width] * win
        bwd = filt[:, (2 * o + 1) * width:(2 * o + 2) * width] * win
        mass = jnp.where(first_half, jnp.abs(fwd) + jnp.abs(bwd), 0.0)
        norm_ref[o:o + 1, :] += jnp.sum(mass, axis=0, keepdims=True)
        hc = jnp.where(first_half, fwd, bwd) + jnp.where(r == 0, bwd, 0.0)
        hc_ref[o] = jnp.where(r == length, 0.0, hc)


def hyena_conv_filters(length, w1, b1, w2, b2, freq, w3, decay):
    n = 2 * length
    width = decay.shape[-1]
    r = jnp.arange(n, dtype=jnp.int32)
    tap = jnp.where(r < length, r, jnp.where(r == length, 0, n - r))
    t = jnp.linspace(0.0, 1.0, length, dtype=F32)[tap][:, None]
    w = 2.0 * math.pi * tap.astype(F32)[:, None] / length
    bands = jnp.linspace(1e-4, HY_BANDS - 1, HY_BANDS, dtype=F32)
    z = jnp.concatenate([t, jnp.cos(bands * w), -jnp.sin(bands * w)], axis=-1)
    assert max(w1.shape) <= HY_PAD

    def pad_to(x, shape):
        return jnp.pad(x, [(0, s - d) for s, d in zip(shape, x.shape)])

    args = (pad_to(z, (n, HY_PAD)), t, pad_to(w1, (HY_PAD, HY_PAD)), pad_to(b1[None, :], (1, HY_PAD)),
            pad_to(w2, (HY_PAD, HY_PAD)), pad_to(b2[None, :], (1, HY_PAD)), pad_to(freq[None, :], (1, HY_PAD)),
            pad_to(w3, (HY_PAD, w3.shape[1])), decay)
    const = lambda i: (0, 0)
    return pl.pallas_call(
        functools.partial(_hy_filter_kernel, length=length), grid=(n // DFT_ROWS,),
        in_specs=[pl.BlockSpec((DFT_ROWS, HY_PAD), lambda i: (i, 0)), pl.BlockSpec((DFT_ROWS, 1), lambda i: (i, 0)),
                  pl.BlockSpec((HY_PAD, HY_PAD), const), pl.BlockSpec((1, HY_PAD), const),
                  pl.BlockSpec((HY_PAD, HY_PAD), const), pl.BlockSpec((1, HY_PAD), const),
                  pl.BlockSpec((1, HY_PAD), const), pl.BlockSpec((HY_PAD, w3.shape[1]), const),
                  pl.BlockSpec(decay.shape, const)],
        out_specs=[pl.BlockSpec((HY_ORDER, DFT_ROWS, width), lambda i: (0, i, 0)),
                   pl.BlockSpec((HY_ORDER, width), const)],
        out_shape=[jax.ShapeDtypeStruct((HY_ORDER, n, width), F32),
                   jax.ShapeDtypeStruct((HY_ORDER, width), F32)],
        compiler_params=_cparams(1), name="hyena_filter")(*args)


DFT_ROWS = ROW_G
GRP = 8
HY_LANES = 256
HY_ROWS = 16
HY_STEP_ROWS = 64


def _group_chunks(rows, width):
    return [(slice(r, r + HY_ROWS), slice(c, c + HY_LANES))
            for r in range(0, rows, HY_ROWS) for c in range(0, width, HY_LANES)]


def _grouped_matmul(w, pieces):
    pieces = [p.astype(F32) for p in pieces]
    outs = []
    for r in range(0, HY_ROWS, GRP):
        rhs = jnp.concatenate([p[r:r + GRP] for p in pieces], axis=0).astype(BF16)
        outs.append(jnp.dot(w, rhs, preferred_element_type=F32))
    return outs


def _piece(outs, idx):
    return jnp.concatenate([o[idx * GRP:(idx + 1) * GRP] for o in outs], axis=0)


def _hy_stage_a_kernel(x_ref, w_ref, ar_ref, ai_ref, *, tile0, n_in):
    planes = x_ref.shape[0]
    n_a = ar_ref.shape[0]
    for g, c in _group_chunks(ar_ref.shape[1], ar_ref.shape[2]):
        outs = _grouped_matmul(w_ref[...], [x_ref[q, tile0 + a, g, c] for q in range(planes) for a in range(n_in)])
        for k in range(n_a):
            ar_ref[k, g, c] = _piece(outs, k).astype(ar_ref.dtype)
            ai_ref[k, g, c] = _piece(outs, n_a + k).astype(ai_ref.dtype)


def hy_stage_a(x5, wmat, col0, width, tile0, n_in):
    planes, npair, ntile, rows, _ = x5.shape
    n_a = wmat.shape[0] // (2 * GRP)
    assert wmat.shape[1] == planes * n_in * GRP
    rb = HY_STEP_ROWS
    kern = functools.partial(_hy_stage_a_kernel, tile0=tile0, n_in=n_in)
    ospec = pl.BlockSpec((None, n_a, rb, width), lambda p, r: (p, 0, r, 0))
    return pl.pallas_call(
        kern, grid=(npair, rows // rb),
        in_specs=[pl.BlockSpec((planes, None, ntile, rb, width), lambda p, r: (0, p, 0, r, col0 // width)),
                  pl.BlockSpec(wmat.shape, lambda p, r: (0, 0))],
        out_specs=[ospec, ospec],
        out_shape=[jax.ShapeDtypeStruct((npair, n_a, rows, width), BF16)] * 2,
        compiler_params=_cparams(2), name="hyena_stage_a")(x5, wmat)


def _hy_stage_b_kernel(ar_ref, ai_ref, f_ref, *rest, spectrum_only):
    half = ar_ref.shape[0]
    a = jnp.concatenate([ar_ref[...], ai_ref[...]], axis=0)
    x = jnp.dot(f_ref[...], a, preferred_element_type=F32)
    if spectrum_only:
        norm_ref, or_ref, oi_ref = rest
        inv = 1.0 / norm_ref[...]
        or_ref[...] = x[:half] * inv
        oi_ref[...] = x[half:] * inv
        return
    g_ref, hr_ref, hi_ref, or_ref, oi_ref = rest
    xr, xi = x[:half], x[half:]
    hr, hi = hr_ref[...], hi_ref[...]
    y = jnp.concatenate([xr * hr - xi * hi, xr * hi + xi * hr], axis=0).astype(BF16)
    z = jnp.dot(g_ref[...], y, preferred_element_type=F32)
    or_ref[...] = z[:half].astype(or_ref.dtype)
    oi_ref[...] = z[half:].astype(oi_ref.dtype)


def hy_stage_b(ar, ai, fmat, gmat=None, hr=None, hi=None, order=None, norm=None):
    npair, n_a, rows, width = ar.shape
    spectrum_only = gmat is None
    aspec = pl.BlockSpec((None, None, rows, width), lambda k, p: (p, k, 0, 0))
    mspec = pl.BlockSpec((None, 2 * rows, 2 * rows), lambda k, p: (k, 0, 0))
    in_specs = [aspec, aspec, mspec]
    args = [ar, ai, fmat]
    if spectrum_only:
        in_specs.append(pl.BlockSpec((None, 1, width), lambda k, p: (p, 0, 0)))
        args.append(norm.reshape(npair, 1, width))
    else:
        hspec = pl.BlockSpec((None, None, rows, width), lambda k, p: (order, k, 0, 0))
        in_specs += [mspec, hspec, hspec]
        args += [gmat, hr, hi]
    out_dtype = F32 if spectrum_only else BF16
    return pl.pallas_call(
        functools.partial(_hy_stage_b_kernel, spectrum_only=spectrum_only), grid=(n_a, npair),
        in_specs=in_specs, out_specs=[aspec, aspec],
        out_shape=[jax.ShapeDtypeStruct(ar.shape, out_dtype)] * 2,
        compiler_params=_cparams(2), name="hyena_stage_b")(*args)


def _hy_stage_a_inv_kernel(ar_ref, ai_ref, w_ref, u_ref, g_ref, skip_ref, o_ref, *, tile0_u, tile0_g):
    n_a = ar_ref.shape[0]
    n_out = o_ref.shape[1]
    skip = skip_ref[...]
    for g, c in _group_chunks(ar_ref.shape[1], ar_ref.shape[2]):
        outs = _grouped_matmul(w_ref[...], [ar_ref[k, g, c] for k in range(n_a)]
                               + [ai_ref[k, g, c] for k in range(n_a)])
        for part in range(2):
            for a in range(n_out):
                conv = (_piece(outs, part * n_out + a)
                        + skip[:, c] * u_ref[part, tile0_u + a, g, c].astype(F32))
                gate = g_ref[part, tile0_g + a, g, c].astype(F32)
                o_ref[part, a, g, c] = (gate * conv).astype(o_ref.dtype)


def hy_stage_a_inv(ar, ai, wmat, u5, u_col0, tile0_u, g5, g_col0, tile0_g, skip, out_dtype):
    npair, n_a, rows, width = ar.shape
    n_out = n_a // 2
    assert wmat.shape == (2 * n_out * GRP, 2 * n_a * GRP)
    rb = HY_STEP_ROWS
    kern = functools.partial(_hy_stage_a_inv_kernel, tile0_u=tile0_u, tile0_g=tile0_g)
    aspec = pl.BlockSpec((None, n_a, rb, width), lambda p, r: (p, 0, r, 0))
    return pl.pallas_call(
        kern, grid=(npair, rows // rb),
        in_specs=[aspec, aspec, pl.BlockSpec(wmat.shape, lambda p, r: (0, 0)),
                  pl.BlockSpec((2, None, u5.shape[2], rb, width), lambda p, r: (0, p, 0, r, u_col0 // width)),
                  pl.BlockSpec((2, None, g5.shape[2], rb, width), lambda p, r: (0, p, 0, r, g_col0 // width)),
                  pl.BlockSpec((1, width), lambda p, r: (0, 0))],
        out_specs=pl.BlockSpec((2, None, n_out, rb, width), lambda p, r: (0, p, 0, r, 0)),
        out_shape=jax.ShapeDtypeStruct((2, npair, n_out, rows, width), out_dtype),
        compiler_params=_cparams(2), name="hyena_stage_a_inv")(ar, ai, wmat, u5, g5, skip.reshape(1, width))


def stage_a_matrices(n_a):
    h = n_a // 2
    idx = jnp.arange(n_a, dtype=jnp.int32)
    ang = (-2.0 * math.pi / n_a) * ((idx[:, None] * idx[None, :]) % n_a).astype(F32)
    wr, wi = jnp.cos(ang), jnp.sin(ang)
    eye = jnp.eye(GRP, dtype=F32)

    def expand(blocks):
        return jnp.kron(jnp.block(blocks), eye).astype(BF16)

    fwd = expand([[wr[:, :h], -wi[:, :h]], [wi[:, :h], wr[:, :h]]])
    fwd_real = expand([[wr], [wi]])
    inv = expand([[wr[:h], wi[:h]], [-wi[:h], wr[:h]]])
    return fwd, fwd_real, inv


def dft_matrices(n_a):
    n = n_a * DFT_ROWS
    kb = jnp.arange(DFT_ROWS, dtype=jnp.int32)[None, :, None]
    b = jnp.arange(DFT_ROWS, dtype=jnp.int32)[None, None, :]
    ka = jnp.arange(n_a, dtype=jnp.int32)[:, None, None]
    ang = (-2.0 * math.pi / n) * ((b * (ka + n_a * kb)) % n).astype(F32)
    fr, fi = jnp.cos(ang), jnp.sin(ang)
    blk = jnp.concatenate([jnp.concatenate([fr, -fi], axis=2), jnp.concatenate([fi, fr], axis=2)], axis=1)
    return blk.astype(BF16), (jnp.swapaxes(blk, 1, 2) * (1.0 / n)).astype(BF16)


def hyena_branch(u5, width, tile0, n_tiles, fargs, skip, dft):
    length = n_tiles * DFT_ROWS
    n_a = 2 * n_tiles
    fmat, gmat, a_fwd, a_fwd_real, a_inv = dft
    hc, norm = hyena_conv_filters(length, *fargs)
    hc = hc.reshape(1, HY_ORDER, n_a, DFT_ROWS, width)
    hr, hi = hy_stage_b(*hy_stage_a(hc, a_fwd_real, 0, width, 0, n_a), fmat, norm=norm)
    x5, x_col0, x_tile0 = u5, 0, tile0
    out = None
    for order in range(HY_ORDER):
        ar, ai = hy_stage_a(x5, a_fwd, x_col0, width, x_tile0, n_tiles)
        ar, ai = hy_stage_b(ar, ai, fmat, gmat, hr, hi, order=order)
        out = hy_stage_a_inv(ar, ai, a_inv, x5, x_col0, x_tile0, u5, (order + 1) * width, tile0, skip[order],
                             BF16)
        x5, x_col0, x_tile0 = out, 0, 0
    return out


def kernel(x, c, ctx, c_ctx, w_mod, b_mod, g_mix, g_ffn, w_in, b_gate, w_br, w_out, da_lambda, da_subln_g,
           lru_conv_w, lru_conv_b, lru_wa, lru_ba, lru_wi, lru_bi, lru_lambda, hy_conv_w, hy_conv_b,
           hy_f_w1, hy_f_b1, hy_f_w2, hy_f_b2, hy_f_freq, hy_f_w3, hy_decay, hy_skip, ffn_w_gate, ffn_w_up,
           ffn_w_down, moe_router, moe_w_gate, moe_w_up, moe_w_down, g_final):
    n_batch, length, d = x.shape
    n_ctx = ctx.shape[1]
    depth = w_mod.shape[0]
    assert n_ctx == ROW_G and length % ROW_G == 0 and d % 128 == 0
    tb = n_ctx + length
    gpb = tb // ROW_G
    nt = n_batch * tb
    c_end = w_in.shape[2]
    cb_k, cb_v, cb_lx, cb_q, cb_ly, cb_hy, cb_g = 0, 1, 2, 3, 4, 5, 8
    assert c_end == 11 * d

    xs = jnp.concatenate([ctx, x], axis=1).reshape(nt, d)
    silu_rows = jnp.concatenate([jax.nn.silu(c), jax.nn.silu(c_ctx)[None, :],
                                 jnp.zeros((16 - n_batch - 1, d), F32)], axis=0)
    tables = rope_tables(length, n_ctx, ROPE_LANES)
    dft_ctx = dft_matrices(2) + stage_a_matrices(2)
    dft_lat = dft_matrices(2 * (gpb - 1)) + stage_a_matrices(2 * (gpb - 1))

    for li in range(depth):
        lam_init = 0.8 - 0.6 * math.exp(-0.3 * li)
        tab = mm_f32_bias(silu_rows, w_mod[li], b_mod[li], d)
        tab = tab.reshape(16, 6, d).transpose(1, 0, 2)

        h = norm_mod(xs, g_mix[li], tab, 0, 1, gpb, n_batch)
        p = in_proj(h, w_in[li].astype(BF16), tables, gpb, cb_k, cb_q)
        a_out = diff_attention(p, d, da_lambda[li], da_subln_g[li], lam_init, n_batch, gpb, cb_q, cb_k, cb_v)

        bw = d // LRU_BLOCKS
        r_out = None
        hf = None
        for direction in range(2):
            w_cat = jnp.concatenate([lru_wa[li, direction], lru_wi[li, direction]], axis=-1).astype(BF16)
            res = lru_pass(p, lru_conv_w[li], lru_conv_b[li], w_cat, lru_ba[li, direction],
                           lru_bi[li, direction], lru_lambda[li, direction], n_batch, gpb, cb_lx,
                           reverse=direction == 1, hf=hf, cb_y=cb_ly)
            if direction == 0:
                hf = res
            else:
                r_out = res
        del bw

        u = hyena_short_conv(p, hy_conv_w[li], hy_conv_b[li], n_batch, gpb, cb_hy, d)
        u5 = u.reshape(2, n_batch // 2, gpb, ROW_G, 3 * d)
        fargs = (hy_f_w1[li], hy_f_b1[li], hy_f_w2[li], hy_f_b2[li], hy_f_freq[li], hy_f_w3[li], hy_decay[li])
        y_c = hyena_branch(u5, d, 0, 1, fargs, hy_skip[li], dft_ctx)
        y_l = hyena_branch(u5, d, 1, gpb - 1, fargs, hy_skip[li], dft_lat)
        y_out = jnp.concatenate([y_c, y_l], axis=2).reshape(nt, d)

        xs = merge_branches(xs, a_out, r_out, y_out, p, b_gate[li], w_br[li].astype(BF16),
                            w_out[li].astype(BF16), tab, 2, gpb, n_batch, cb_g)

        jj = li // 2
        if li % 2 == 0:
            f = norm_mod(xs, g_ffn[li], tab, 3, 4, gpb, n_batch)
            hh = gate_up(f, ffn_w_gate[jj].astype(BF16), ffn_w_up[jj].astype(BF16))
            xs = down_residual(hh, ffn_w_down[jj].astype(BF16), xs, tab, 5, gpb, n_batch)
        else:
            wr = jnp.concatenate([moe_router[jj], jnp.zeros((d, 128 - N_EXPERTS), F32)], axis=1)
            f, logits = norm_mod(xs, g_ffn[li], tab, 3, 4, gpb, n_batch, w_router=wr)
            xs = moe_layer(f, logits, moe_w_gate[jj].astype(BF16), moe_w_up[jj].astype(BF16),
                           moe_w_down[jj].astype(BF16), xs, tab, 5, gpb, n_batch)

    return final_norm(xs.reshape(n_batch, tb, d), g_final, n_ctx)
```

```python
import functools
import math

import jax
import jax.numpy as jnp
from jax import lax
from jax.experimental import pallas as pl
from jax.experimental.pallas import tpu as pltpu

F32 = jnp.float32
BF16 = jnp.bfloat16

EPS = 1e-6
ROW_G = 256
GRID_W = 64
ROPE_BASE = 10000.0
DA_HEADS = 8
DA_HEAD_DIM = 64
DA_V_DIM = 128
LRU_BLOCKS = 8
LRU_C = 8.0
HY_ORDER = 2
HY_BANDS = 16
N_EXPERTS = 8
TOP_K = 2
MOE_BLOCK = 256
HALO = 16
VMEM_LIMIT = 56 * 1024 * 1024


def _cparams(n_axes):
    return pltpu.CompilerParams(dimension_semantics=("arbitrary",) * n_axes,
                                vmem_limit_bytes=VMEM_LIMIT)


def _pick(n, prefs):
    for p in prefs:
        if n % p == 0:
            return p
    raise ValueError(f"no tile in {prefs} divides {n}")


def _mod_row(tab_ref, k, group, groups_per_batch, n_batch):
    b = group // groups_per_batch
    row = jnp.where(group % groups_per_batch == 0, n_batch, b)
    return tab_ref[k, pl.ds(row, 1), :]


def _mm_f32_kernel(a_ref, w_ref, b_ref, o_ref):
    o_ref[...] = jnp.dot(a_ref[...], w_ref[...], preferred_element_type=F32) + b_ref[...]


def mm_f32_bias(a, w, b, tn):
    m, k = a.shape
    n = w.shape[1]
    return pl.pallas_call(
        _mm_f32_kernel, grid=(n // tn,),
        in_specs=[pl.BlockSpec((m, k), lambda j: (0, 0)),
                  pl.BlockSpec((k, tn), lambda j: (0, j)),
                  pl.BlockSpec((1, tn), lambda j: (0, j))],
        out_specs=pl.BlockSpec((m, tn), lambda j: (0, j)),
        out_shape=jax.ShapeDtypeStruct((m, n), F32),
        compiler_params=_cparams(1), name="mod_matmul")(a, w, b.reshape(1, n))


def _norm_mod_kernel(x_ref, g_ref, tab_ref, *rest, k_shift, k_scale, gpb, n_batch, router):
    if router:
        wr_ref, o_ref, lg_ref = rest
    else:
        (o_ref,) = rest
    nsub = x_ref.shape[0] // ROW_G
    for s in range(nsub):
        grp = pl.program_id(0) * nsub + s
        rows = pl.ds(s * ROW_G, ROW_G)
        xs = x_ref[rows, :]
        y = xs * lax.rsqrt(jnp.mean(xs * xs, axis=-1, keepdims=True) + EPS) * g_ref[...]
        shift = _mod_row(tab_ref, k_shift, grp, gpb, n_batch)
        scale = _mod_row(tab_ref, k_scale, grp, gpb, n_batch)
        h = y * (1.0 + scale) + shift
        o_ref[rows, :] = h.astype(o_ref.dtype)
        if router:
            lg_ref[rows, :] = jnp.dot(h, wr_ref[...], preferred_element_type=F32,
                                      precision=lax.Precision.HIGHEST)


def norm_mod(x, g, tab, k_shift, k_scale, gpb, n_batch, w_router=None):
    nt, d = x.shape
    tm = _pick(nt, (1024, 512, 256))
    router = w_router is not None
    kern = functools.partial(_norm_mod_kernel, k_shift=k_shift, k_scale=k_scale, gpb=gpb,
                             n_batch=n_batch, router=router)
    in_specs = [pl.BlockSpec((tm, d), lambda i: (i, 0)),
                pl.BlockSpec((1, d), lambda i: (0, 0)),
                pl.BlockSpec(tab.shape, lambda i: (0, 0, 0))]
    out_specs = [pl.BlockSpec((tm, d), lambda i: (i, 0))]
    out_shape = [jax.ShapeDtypeStruct((nt, d), BF16)]
    args = [x, g.reshape(1, d), tab]
    if router:
        in_specs.append(pl.BlockSpec(w_router.shape, lambda i: (0, 0)))
        out_specs.append(pl.BlockSpec((tm, w_router.shape[1]), lambda i: (i, 0)))
        out_shape.append(jax.ShapeDtypeStruct((nt, w_router.shape[1]), F32))
        args.append(w_router)
    out = pl.pallas_call(kern, grid=(nt // tm,), in_specs=in_specs, out_specs=out_specs,
                         out_shape=out_shape, compiler_params=_cparams(1), name="norm_mod")(*args)
    return out if router else out[0]


def _final_norm_kernel(x_ref, g_ref, o_ref):
    xs = x_ref[...]
    o_ref[...] = xs * lax.rsqrt(jnp.mean(xs * xs, axis=-1, keepdims=True) + EPS) * g_ref[...]


def final_norm(x3, g, n_ctx):
    b, tb, d = x3.shape
    length = tb - n_ctx
    tm = ROW_G
    off = n_ctx // tm
    return pl.pallas_call(
        _final_norm_kernel, grid=(b, length // tm),
        in_specs=[pl.BlockSpec((None, tm, d), lambda i, j: (i, j + off, 0)),
                  pl.BlockSpec((1, d), lambda i, j: (0, 0))],
        out_specs=pl.BlockSpec((None, tm, d), lambda i, j: (i, j, 0)),
        out_shape=jax.ShapeDtypeStruct((b, length, d), F32),
        compiler_params=_cparams(2), name="final_norm")(x3, g.reshape(1, d))


ROPE_LANES = 128


def _in_proj_kernel(a_ref, w_ref, cos_ref, sa_ref, sb_ref, o_ref, *, cb_k, cb_q, q_scale, gpb):
    n = pl.program_id(0)
    is_rot = jnp.logical_or(n == cb_k, n == cb_q)

    @pl.when(jnp.logical_not(is_rot))
    def _():
        o_ref[...] = jnp.dot(a_ref[...], w_ref[...], preferred_element_type=F32).astype(o_ref.dtype)

    @pl.when(is_rot)
    def _():
        scale = jnp.where(n == cb_q, q_scale, 1.0)
        nsub = a_ref.shape[0] // ROW_G
        for s in range(nsub):
            rows = slice(s * ROW_G, (s + 1) * ROW_G)
            acc = jnp.dot(a_ref[rows, :], w_ref[...], preferred_element_type=F32)
            j = (pl.program_id(1) * nsub + s) % gpb
            trow = pl.ds(pl.multiple_of(j * ROW_G, ROW_G), ROW_G)
            cos, sa, sb = cos_ref[trow, :] * scale, sa_ref[trow, :] * scale, sb_ref[trow, :] * scale
            for c in range(acc.shape[1] // ROPE_LANES):
                t = acc[:, c * ROPE_LANES:(c + 1) * ROPE_LANES]
                rot = (t * cos + pltpu.roll(t, ROPE_LANES - 16, 1) * sa + pltpu.roll(t, 16, 1) * sb)
                o_ref[rows, c * ROPE_LANES:(c + 1) * ROPE_LANES] = rot.astype(o_ref.dtype)


def in_proj(h, w, tables, gpb, cb_k, cb_q):
    m, k = h.shape
    n = w.shape[1]
    tm = _pick(m, (1024, 512, 256))
    tn = k
    cos, sa, sb = tables
    kern = functools.partial(_in_proj_kernel, cb_k=cb_k, cb_q=cb_q, gpb=gpb,
                             q_scale=math.log2(math.e) * DA_HEAD_DIM ** -0.5)
    tspec = pl.BlockSpec(cos.shape, lambda j, i: (0, 0))
    return pl.pallas_call(
        kern, grid=(n // tn, m // tm),
        in_specs=[pl.BlockSpec((tm, k), lambda j, i: (i, 0)),
                  pl.BlockSpec((k, tn), lambda j, i: (0, j)), tspec, tspec, tspec],
        out_specs=pl.BlockSpec((tm, tn), lambda j, i: (i, j)),
        out_shape=jax.ShapeDtypeStruct((m, n), BF16),
        compiler_params=_cparams(2), name="in_proj")(h, w, cos, sa, sb)


def rope_tables(length, n_ctx, width):
    rows = length // GRID_W
    row = jnp.repeat(jnp.arange(rows, dtype=F32), GRID_W)
    col = jnp.tile(jnp.arange(GRID_W, dtype=F32), rows)
    n_freq = DA_HEAD_DIM // 4
    inv = ROPE_BASE ** (-jnp.arange(n_freq, dtype=F32) / n_freq)
    ar = row[:, None] * inv
    ac = col[:, None] * inv
    ang = jnp.concatenate([ar, ar, ac, ac], axis=-1)
    ang = jnp.concatenate([jnp.zeros((n_ctx, DA_HEAD_DIM), F32), ang], axis=0)
    reps = width // DA_HEAD_DIM
    cos = jnp.tile(jnp.cos(ang), (1, reps))
    sin = jnp.tile(jnp.sin(ang), (1, reps))
    first = (jnp.arange(width) % (2 * n_freq)) < n_freq
    sa = jnp.where(first[None, :], -sin, 0.0)
    sb = jnp.where(first[None, :], 0.0, sin)
    return cos, sa, sb


def _attn_kernel(lamv_ref, g_ref, q_ref, k_ref, v_ref, o_ref, vaug_ref, m_ref, acc_ref,
                 *, lam_init, n_ctx, tk):
    j = pl.program_id(2)
    dv = v_ref.shape[1]

    @pl.when(j == 0)
    def _():
        vaug_ref[:, :dv] = v_ref[...]
        vaug_ref[:, dv:] = jnp.ones((vaug_ref.shape[0], vaug_ref.shape[1] - dv), vaug_ref.dtype)

    q = q_ref[...]
    lane = lax.broadcasted_iota(jnp.int32, q.shape, 1)
    zero = jnp.zeros_like(q)
    qq = jnp.concatenate([jnp.where(lane < DA_HEAD_DIM, q, zero),
                          jnp.where(lane >= DA_HEAD_DIM, q, zero)], axis=0)
    tq = q.shape[0]

    def scores(start, size):
        kc = k_ref[pl.ds(start, size), :]
        return lax.dot_general(qq, kc, (((1,), (1,)), ((), ())), preferred_element_type=F32)

    def update(s, start, size, first=False):
        m_new = jnp.max(s, axis=-1, keepdims=True)
        if first:
            m_new = jnp.broadcast_to(m_new, m_ref.shape)
        else:
            m_prev = m_ref[...]
            m_new = jnp.maximum(m_prev, m_new)
        p = jnp.exp2(s - jnp.concatenate([m_new] * (size // 128), axis=1))
        pv = jnp.dot(p.astype(BF16), vaug_ref[pl.ds(start, size), :], preferred_element_type=F32)
        if first:
            acc_ref[...] = pv
        else:
            alpha = jnp.exp2(m_prev - m_new)
            acc_ref[...] = jnp.concatenate([alpha] * (acc_ref.shape[1] // 128), axis=1) * acc_ref[...] + pv
        m_ref[...] = m_new

    s_ctx = scores(0, n_ctx)

    @pl.when(j == 0)
    def _():
        update(s_ctx, 0, n_ctx, first=True)

    @pl.when(j > 0)
    def _():
        starts = [n_ctx + i * tk for i in range((k_ref.shape[0] - n_ctx) // tk)]
        s_next = scores(starts[0], tk)
        update(s_ctx, 0, n_ctx, first=True)
        for i, start in enumerate(starts):
            s_cur = s_next
            if i + 1 < len(starts):
                s_next = scores(starts[i + 1], tk)
            update(s_cur, start, tk)

    lv = lamv_ref[...]
    lam = (jnp.exp(jnp.sum(lv[0:1] * lv[1:2], axis=-1, keepdims=True))
           - jnp.exp(jnp.sum(lv[2:3] * lv[3:4], axis=-1, keepdims=True)) + lam_init)
    o = acc_ref[:, :dv] / acc_ref[:, dv:]
    o = o[:tq] - lam * o[tq:]
    o = o * lax.rsqrt(jnp.mean(o * o, axis=-1, keepdims=True) + EPS) * g_ref[...]
    o_ref[...] = (o * (1.0 - lam_init)).astype(o_ref.dtype)


def diff_attention(p, width, lam_vecs, subln_g, lam_init, n_batch, gpb, cb_q, cb_k, cb_v):
    nt = p.shape[0]
    tb = gpb * ROW_G
    n_ctx = ROW_G
    tk = _pick(tb - n_ctx, (2048, 1024, 512, 256))
    kern = functools.partial(_attn_kernel, lam_init=lam_init, n_ctx=n_ctx, tk=tk)
    assert DA_V_DIM == 128
    hpb = width // DA_V_DIM
    return pl.pallas_call(
        kern, grid=(n_batch, DA_HEADS, gpb),
        in_specs=[pl.BlockSpec(lam_vecs.shape, lambda b, h, j: (0, 0)),
                  pl.BlockSpec((1, DA_V_DIM), lambda b, h, j: (0, 0)),
                  pl.BlockSpec((ROW_G, DA_V_DIM), lambda b, h, j: (b * gpb + j, cb_q * hpb + h)),
                  pl.BlockSpec((tb, DA_V_DIM), lambda b, h, j: (b, cb_k * hpb + h)),
                  pl.BlockSpec((tb, DA_V_DIM), lambda b, h, j: (b, cb_v * hpb + h))],
        out_specs=pl.BlockSpec((ROW_G, DA_V_DIM), lambda b, h, j: (b * gpb + j, h)),
        out_shape=jax.ShapeDtypeStruct((nt, width), BF16),
        scratch_shapes=[pltpu.VMEM((tb, 2 * DA_V_DIM), BF16), pltpu.VMEM((2 * ROW_G, 128), F32),
                        pltpu.VMEM((2 * ROW_G, 2 * DA_V_DIM), F32)],
        compiler_params=_cparams(3), name="diff_attn")(
            lam_vecs, subln_g.reshape(1, DA_V_DIM), p, p, p)


def _shift_matrix(n, d):
    row = lax.broadcasted_iota(jnp.int32, (n, n), 0)
    col = lax.broadcasted_iota(jnp.int32, (n, n), 1)
    return jnp.where(col == row + d, 1.0, 0.0).astype(BF16)


def _dwconv(main_ref, prev_ref, next_ref, w_ref, b_ref, left, has_prev, has_next):
    x = main_ref[...]
    acc = b_ref[...] + w_ref[left:left + 1, :] * x.astype(F32)
    prev8 = prev_ref[...].astype(F32)[HALO - 8:, :]
    next8 = next_ref[...].astype(F32)[:8, :]
    sub = lax.broadcasted_iota(jnp.int32, (8, 1), 0)
    first = jnp.zeros_like(prev8)
    last = jnp.zeros_like(next8)
    for j in range(w_ref.shape[0]):
        d = j - left
        if d == 0:
            continue
        wj = w_ref[j:j + 1, :]
        acc = acc + wj * jnp.dot(_shift_matrix(ROW_G, d), x, preferred_element_type=F32)
        if d < 0:
            mask = jnp.logical_and(sub < -d, has_prev)
            first = first + wj * jnp.where(mask, pltpu.roll(prev8, -d, 0), 0.0)
        else:
            mask = jnp.logical_and(sub >= 8 - d, has_next)
            last = last + wj * jnp.where(mask, pltpu.roll(next8, 8 - d, 0), 0.0)
    return jnp.concatenate([acc[0:8] + first, acc[8:ROW_G - 8], acc[ROW_G - 8:] + last], axis=0)


def _halo_specs(width, cb, gpb, tile_of, n_groups):
    per = ROW_G // HALO

    def main(b, s, *_):
        return (b * gpb + tile_of(s), cb)

    def prev(b, s, *_):
        return (jnp.maximum((b * gpb + tile_of(s)) * per - 1, 0), cb)

    def nxt(b, s, *_):
        return (jnp.minimum((b * gpb + tile_of(s) + 1) * per, n_groups * per - 1), cb)

    return [pl.BlockSpec((ROW_G, width), main), pl.BlockSpec((HALO, width), prev),
            pl.BlockSpec((HALO, width), nxt)]


def _seq_flags(j, gpb):
    return j >= 2, jnp.logical_and(j >= 1, j <= gpb - 2)


SCAN_G = 8


def _scan_groups(a, b, reverse):
    shape = a.shape
    a = a.reshape(shape[0] // SCAN_G, SCAN_G, shape[1])
    b = b.reshape(a.shape)
    pos = lax.broadcasted_iota(jnp.int32, a.shape, 1)
    s = 1
    while s < SCAN_G:
        keep = pos < SCAN_G - s if reverse else pos >= s
        shift = SCAN_G - s if reverse else s
        a_sh = jnp.where(keep, pltpu.roll(a, shift, 1), 1.0)
        b_sh = jnp.where(keep, pltpu.roll(b, shift, 1), 0.0)
        b = a * b_sh + b
        a = a * a_sh
        s *= 2
    return a.reshape(shape), b.reshape(shape)


def _gelu_tanh(x):
    return 0.5 * x * (1.0 + jnp.tanh(math.sqrt(2.0 / math.pi) * (x + 0.044715 * (x * x * x))))


def _lru_kernel(main_ref, prev_ref, next_ref, cw_ref, cb_ref, w_ref, ba_ref, bi_ref, lam_ref, *rest,
                gpb, reverse):
    if reverse:
        hf_ref, ly_ref, o_ref, carry_ref = rest
    else:
        o_ref, carry_ref = rest
    s = pl.program_id(1)
    j = jnp.where(s == 0, 0, gpb - s) if reverse else s
    has_prev, has_next = _seq_flags(j, gpb)
    xc = _dwconv(main_ref, prev_ref, next_ref, cw_ref, cb_ref, 2, has_prev, has_next)
    xcb = xc.astype(BF16)
    bw = w_ref.shape[1]
    r_parts, i_parts = [], []
    for n in range(w_ref.shape[0]):
        res = jnp.dot(xcb[:, n * bw:(n + 1) * bw], w_ref[n], preferred_element_type=F32)
        r_parts.append(res[:, :bw])
        i_parts.append(res[:, bw:])
    r = jax.nn.sigmoid(jnp.concatenate(r_parts, axis=1) + ba_ref[...])
    gate_i = jax.nn.sigmoid(jnp.concatenate(i_parts, axis=1) + bi_ref[...])
    nl = -lam_ref[...]
    softplus = jnp.maximum(nl, 0.0) + jnp.log(1.0 + jnp.exp(-jnp.abs(nl)))
    a = jnp.exp((-LRU_C) * r * softplus)
    bb = jnp.sqrt(1.0 - a * a) * (gate_i * xc)
    a_grp, h_grp = _scan_groups(a, bb, reverse)

    @pl.when(s == 0)
    def _():
        carry_ref[...] = jnp.zeros(carry_ref.shape, F32)

    h = carry_ref[...]
    n_grp = ROW_G // SCAN_G
    pieces = [None] * n_grp
    for g in (reversed(range(n_grp)) if reverse else range(n_grp)):
        rows = slice(g * SCAN_G, (g + 1) * SCAN_G)
        hg = a_grp[rows] * h + h_grp[rows]
        pieces[g] = hg
        h = hg[0:1, :] if reverse else hg[SCAN_G - 1:SCAN_G, :]
    carry_ref[...] = h
    h_all = jnp.concatenate(pieces, axis=0)
    if reverse:
        ly = ly_ref[...].astype(F32)
        o_ref[...] = (_gelu_tanh(ly) * (hf_ref[...].astype(F32) + h_all)).astype(o_ref.dtype)
    else:
        o_ref[...] = h_all.astype(o_ref.dtype)


def lru_pass(p, conv_w, conv_b, w_cat, ba, bi, lam, n_batch, gpb, cb_x, reverse, hf=None, cb_y=None):
    nt = p.shape[0]
    width = conv_w.shape[1]
    tile_of = (lambda s: jnp.where(s == 0, 0, gpb - s)) if reverse else (lambda s: s)
    kern = functools.partial(_lru_kernel, gpb=gpb, reverse=reverse)
    const2 = lambda b, s: (0, 0)
    in_specs = _halo_specs(width, cb_x, gpb, tile_of, nt // ROW_G) + [
        pl.BlockSpec(conv_w.shape, const2), pl.BlockSpec((1, width), const2),
        pl.BlockSpec(w_cat.shape, lambda b, s: (0, 0, 0)),
        pl.BlockSpec((1, width), const2), pl.BlockSpec((1, width), const2),
        pl.BlockSpec((1, width), const2)]
    args = [p, p, p, conv_w, conv_b.reshape(1, width), w_cat, ba.reshape(1, width),
            bi.reshape(1, width), lam.reshape(1, width)]
    row_spec = lambda cb: pl.BlockSpec((ROW_G, width), lambda b, s: (b * gpb + tile_of(s), cb))
    if reverse:
        in_specs += [row_spec(0), row_spec(cb_y)]
        args += [hf, p]
    return pl.pallas_call(
        kern, grid=(n_batch, gpb), in_specs=in_specs, out_specs=row_spec(0),
        out_shape=jax.ShapeDtypeStruct((nt, width), BF16),
        scratch_shapes=[pltpu.VMEM((1, width), F32)],
        compiler_params=_cparams(2), name="lru_bwd" if reverse else "lru_fwd")(*args)


def _short_conv_kernel(*refs, gpb, ncb):
    cw_ref, cb_ref, o_ref = refs[3 * ncb:]
    j = pl.program_id(1)
    has_prev, has_next = _seq_flags(j, gpb)
    width = o_ref.shape[1] // ncb
    for c in range(ncb):
        main_ref, prev_ref, next_ref = refs[3 * c:3 * c + 3]
        cols = pl.ds(c * width, width)
        o_ref[:, cols] = _dwconv(main_ref, prev_ref, next_ref, cw_ref.at[:, cols], cb_ref.at[:, cols], 1,
                                 has_prev, has_next).astype(o_ref.dtype)


def hyena_short_conv(p, conv_w, conv_b, n_batch, gpb, cb0, width):
    nt = p.shape[0]
    total = conv_w.shape[1]
    ncb = total // width
    halos = []
    for c in range(ncb):
        halos += _halo_specs(width, cb0 + c, gpb, lambda s: s, nt // ROW_G)
    kern = functools.partial(_short_conv_kernel, gpb=gpb, ncb=ncb)
    return pl.pallas_call(
        kern, grid=(n_batch, gpb),
        in_specs=halos + [pl.BlockSpec(conv_w.shape, lambda b, j: (0, 0)),
                          pl.BlockSpec((1, total), lambda b, j: (0, 0))],
        out_specs=pl.BlockSpec((ROW_G, total), lambda b, j: (b * gpb + j, 0)),
        out_shape=jax.ShapeDtypeStruct((nt, total), BF16),
        compiler_params=_cparams(2), name="hyena_short_conv")(*([p] * (3 * ncb)), conv_w,
                                                              conv_b.reshape(1, total))


def _merge_kernel(a_ref, r_ref, y_ref, g0_ref, g1_ref, g2_ref, bg_ref, wbr_ref, wout_ref, tab_ref, x_ref,
                  o_ref, *, k_gate, gpb, n_batch):
    width = a_ref.shape[1]
    m = None
    for k, (br_ref, gt_ref) in enumerate(((a_ref, g0_ref), (r_ref, g1_ref), (y_ref, g2_ref))):
        gate = jax.nn.sigmoid(gt_ref[...].astype(F32) + bg_ref[:, k * width:(k + 1) * width])
        term = gate * jnp.dot(br_ref[...], wbr_ref[k], preferred_element_type=F32)
        m = term if m is None else m + term
    out = jnp.dot(m.astype(BF16), wout_ref[...], preferred_element_type=F32)
    nsub = x_ref.shape[0] // ROW_G
    for s in range(nsub):
        rows = pl.ds(s * ROW_G, ROW_G)
        gmod = _mod_row(tab_ref, k_gate, pl.program_id(0) * nsub + s, gpb, n_batch)
        o_ref[rows, :] = x_ref[rows, :] + gmod * out[s * ROW_G:(s + 1) * ROW_G, :]


def merge_branches(x, a, r, y, p, b_gate, w_br, w_out, tab, k_gate, gpb, n_batch, cb_g):
    nt, d = x.shape
    tm = _pick(nt, (512, 256))
    kern = functools.partial(_merge_kernel, k_gate=k_gate, gpb=gpb, n_batch=n_batch)
    row = pl.BlockSpec((tm, d), lambda i: (i, 0))
    gspec = lambda k: pl.BlockSpec((tm, d), lambda i: (i, cb_g + k))
    return pl.pallas_call(
        kern, grid=(nt // tm,),
        in_specs=[row, row, row, gspec(0), gspec(1), gspec(2),
                  pl.BlockSpec((1, 3 * d), lambda i: (0, 0)),
                  pl.BlockSpec(w_br.shape, lambda i: (0, 0, 0)),
                  pl.BlockSpec(w_out.shape, lambda i: (0, 0)),
                  pl.BlockSpec(tab.shape, lambda i: (0, 0, 0)), row],
        out_specs=row, out_shape=jax.ShapeDtypeStruct((nt, d), F32),
        compiler_params=_cparams(1), name="merge")(a, r, y, p, p, p, b_gate.reshape(1, 3 * d), w_br, w_out, tab, x)


def _gate_up_kernel(*refs, expert):
    if expert:
        _, f_ref, wg_ref, wu_ref, o_ref = refs
    else:
        f_ref, wg_ref, wu_ref, o_ref = refs
    f = f_ref[...]
    g = jnp.dot(f, wg_ref[...], preferred_element_type=F32)
    u = jnp.dot(f, wu_ref[...], preferred_element_type=F32)
    o_ref[...] = (g * jax.nn.sigmoid(g) * u).astype(o_ref.dtype)


def gate_up(f, wg, wu, block_e=None):
    m, d = f.shape
    dff = wg.shape[-1]
    tn = _pick(dff, (1408, 1024, 512, 256, 128))
    if block_e is None:
        tm = _pick(m, (512, 256))
        grid_spec = pltpu.PrefetchScalarGridSpec(
            num_scalar_prefetch=0, grid=(dff // tn, m // tm),
            in_specs=[pl.BlockSpec((tm, d), lambda j, i: (i, 0)),
                      pl.BlockSpec((d, tn), lambda j, i: (0, j)),
                      pl.BlockSpec((d, tn), lambda j, i: (0, j))],
            out_specs=pl.BlockSpec((tm, tn), lambda j, i: (i, j)))
        args = (f, wg, wu)
    else:
        tm = MOE_BLOCK
        grid_spec = pltpu.PrefetchScalarGridSpec(
            num_scalar_prefetch=1, grid=(dff // tn, m // tm),
            in_specs=[pl.BlockSpec((tm, d), lambda j, i, be: (i, 0)),
                      pl.BlockSpec((None, d, tn), lambda j, i, be: (be[i], 0, j)),
                      pl.BlockSpec((None, d, tn), lambda j, i, be: (be[i], 0, j))],
            out_specs=pl.BlockSpec((tm, tn), lambda j, i, be: (i, j)))
        args = (block_e, f, wg, wu)
    return pl.pallas_call(
        functools.partial(_gate_up_kernel, expert=block_e is not None), grid_spec=grid_spec,
        out_shape=jax.ShapeDtypeStruct((m, dff), BF16),
        compiler_params=_cparams(2), name="gate_up")(*args)


def _down_res_kernel(h_ref, wd_ref, tab_ref, x_ref, o_ref, *, k_gate, gpb, n_batch):
    out = jnp.dot(h_ref[...], wd_ref[...], preferred_element_type=F32)
    nsub = x_ref.shape[0] // ROW_G
    for s in range(nsub):
        rows = pl.ds(s * ROW_G, ROW_G)
        gmod = _mod_row(tab_ref, k_gate, pl.program_id(0) * nsub + s, gpb, n_batch)
        o_ref[rows, :] = x_ref[rows, :] + gmod * out[s * ROW_G:(s + 1) * ROW_G, :]


def down_residual(h, wd, x, tab, k_gate, gpb, n_batch):
    nt, d = x.shape
    dff = h.shape[1]
    tm = _pick(nt, (512, 256))
    kern = functools.partial(_down_res_kernel, k_gate=k_gate, gpb=gpb, n_batch=n_batch)
    return pl.pallas_call(
        kern, grid=(nt // tm,),
        in_specs=[pl.BlockSpec((tm, dff), lambda i: (i, 0)),
                  pl.BlockSpec((dff, d), lambda i: (0, 0)),
                  pl.BlockSpec(tab.shape, lambda i: (0, 0, 0)),
                  pl.BlockSpec((tm, d), lambda i: (i, 0))],
        out_specs=pl.BlockSpec((tm, d), lambda i: (i, 0)),
        out_shape=jax.ShapeDtypeStruct((nt, d), F32),
        compiler_params=_cparams(1), name="down_residual")(h, wd, tab, x)


def _expert_down_kernel(be_ref, h_ref, wd_ref, o_ref):
    o_ref[...] = jnp.dot(h_ref[...], wd_ref[...], preferred_element_type=F32).astype(o_ref.dtype)


def expert_down(h, wd, block_e):
    m, dff = h.shape
    d = wd.shape[-1]
    grid_spec = pltpu.PrefetchScalarGridSpec(
        num_scalar_prefetch=1, grid=(m // MOE_BLOCK,),
        in_specs=[pl.BlockSpec((MOE_BLOCK, dff), lambda i, be: (i, 0)),
                  pl.BlockSpec((None, dff, d), lambda i, be: (be[i], 0, 0))],
        out_specs=pl.BlockSpec((MOE_BLOCK, d), lambda i, be: (i, 0)))
    return pl.pallas_call(_expert_down_kernel, grid_spec=grid_spec,
                          out_shape=jax.ShapeDtypeStruct((m, d), BF16),
                          compiler_params=_cparams(1), name="expert_down")(block_e, h, wd)


def _combine_kernel(y0_ref, y1_ref, w0_ref, w1_ref, tab_ref, x_ref, o_ref, *, k_gate, gpb, n_batch):
    nsub = x_ref.shape[0] // ROW_G
    for s in range(nsub):
        rows = pl.ds(s * ROW_G, ROW_G)
        gmod = _mod_row(tab_ref, k_gate, pl.program_id(0) * nsub + s, gpb, n_batch)
        y = (y0_ref[rows, :].astype(F32) * w0_ref[rows, :]
             + y1_ref[rows, :].astype(F32) * w1_ref[rows, :])
        o_ref[rows, :] = x_ref[rows, :] + gmod * y


def moe_combine(y0, y1, w0, w1, x, tab, k_gate, gpb, n_batch):
    nt, d = x.shape
    tm = _pick(nt, (512, 256))
    kern = functools.partial(_combine_kernel, k_gate=k_gate, gpb=gpb, n_batch=n_batch)
    row = pl.BlockSpec((tm, d), lambda i: (i, 0))
    wsp = pl.BlockSpec((tm, 1), lambda i: (i, 0))
    return pl.pallas_call(
        kern, grid=(nt // tm,),
        in_specs=[row, row, wsp, wsp, pl.BlockSpec(tab.shape, lambda i: (0, 0, 0)), row],
        out_specs=row, out_shape=jax.ShapeDtypeStruct((nt, d), F32),
        compiler_params=_cparams(1), name="moe_combine")(y0, y1, w0, w1, tab, x)


def moe_layer(f, logits, w_gate, w_up, w_down, x, tab, k_gate, gpb, n_batch):
    nt = f.shape[0]
    n_asg = nt * TOP_K
    top_v, top_i = lax.top_k(logits[:, :N_EXPERTS], TOP_K)
    top_w = jax.nn.softmax(top_v, axis=-1)
    e_flat = top_i.reshape(-1)
    onehot = (e_flat[:, None] == jnp.arange(N_EXPERTS, dtype=e_flat.dtype)[None, :]).astype(jnp.int32)
    csum = jnp.cumsum(onehot, axis=0)
    counts = csum[-1]
    rank = jnp.take_along_axis(csum, e_flat[:, None], axis=1)[:, 0] - 1
    padded = ((counts + MOE_BLOCK - 1) // MOE_BLOCK) * MOE_BLOCK
    pad_end = jnp.cumsum(padded)
    pad_start = pad_end - padded
    dest = (pad_start[e_flat] + rank).astype(jnp.int32)
    n_blocks = -(-n_asg // MOE_BLOCK) + N_EXPERTS
    n_slots = n_blocks * MOE_BLOCK
    block_e = jnp.minimum(jnp.searchsorted(pad_end, jnp.arange(n_blocks, dtype=jnp.int32) * MOE_BLOCK,
                                           side='right'), N_EXPERTS - 1).astype(jnp.int32)
    order = jnp.argsort(e_flat, stable=True).astype(jnp.int32)
    e_slot = jnp.repeat(block_e, MOE_BLOCK)
    within = jnp.arange(n_slots, dtype=jnp.int32) - pad_start[e_slot].astype(jnp.int32)
    valid = within < counts[e_slot]
    src = jnp.where(valid, (jnp.cumsum(counts) - counts)[e_slot].astype(jnp.int32) + within, 0)
    slot_tok = jnp.where(valid, order[src] // TOP_K, 0)
    xs = f.at[slot_tok].get(mode='promise_in_bounds')
    h = gate_up(xs, w_gate, w_up, block_e)
    ys = expert_down(h, w_down, block_e)
    dest2 = dest.reshape(nt, TOP_K)
    y0 = ys.at[dest2[:, 0]].get(mode='promise_in_bounds')
    y1 = ys.at[dest2[:, 1]].get(mode='promise_in_bounds')
    return moe_combine(y0, y1, top_w[:, 0:1], top_w[:, 1:2], x, tab, k_gate, gpb, n_batch)


HY_PAD = 128


def _hy_filter_kernel(z_ref, t_ref, w1_ref, b1_ref, w2_ref, b2_ref, fr_ref, w3_ref, dec_ref, hc_ref, norm_ref,
                      *, length):
    i = pl.program_id(0)
    rows = z_ref.shape[0]
    width = dec_ref.shape[1]
    hp = lax.Precision.HIGHEST
    fr = fr_ref[...]
    hdn = jnp.sin(fr * (jnp.dot(z_ref[...], w1_ref[...], precision=hp, preferred_element_type=F32) + b1_ref[...]))
    hdn = jnp.sin(fr * (jnp.dot(hdn, w2_ref[...], precision=hp, preferred_element_type=F32) + b2_ref[...]))
    w3 = w3_ref[...]
    h_hi, w_hi = hdn.astype(BF16), w3.astype(BF16)
    h_lo = (hdn - h_hi.astype(F32)).astype(BF16)
    w_lo = (w3 - w_hi.astype(F32)).astype(BF16)
    filt = (jnp.dot(h_hi, w_hi, preferred_element_type=F32) + jnp.dot(h_hi, w_lo, preferred_element_type=F32)
            + jnp.dot(h_lo, w_hi, preferred_element_type=F32))
    t = t_ref[...]
    r = i * rows + lax.broadcasted_iota(jnp.int32, (rows, 1), 0)
    first_half = r < length

    @pl.when(i == 0)
    def _():
        norm_ref[...] = jnp.full(norm_ref.shape, EPS, F32)

    for o in range(HY_ORDER):
        win = jnp.exp(-t * jnp.abs(dec_ref[o:o + 1, :]))
        fwd = filt[:, (2 * o) * width:(2 * o + 1) * width] * win
        bwd = filt[:, (2 * o + 1) * width:(2 * o + 2) * width] * win
        mass = jnp.where(first_half, jnp.abs(fwd) + jnp.abs(bwd), 0.0)
        norm_ref[o:o + 1, :] += jnp.sum(mass, axis=0, keepdims=True)
        hc = jnp.where(first_half, fwd, bwd) + jnp.where(r == 0, bwd, 0.0)
        hc_ref[o] = jnp.where(r == length, 0.0, hc)


def hyena_conv_filters(length, w1, b1, w2, b2, freq, w3, decay):
    n = 2 * length
    width = decay.shape[-1]
    r = jnp.arange(n, dtype=jnp.int32)
    tap = jnp.where(r < length, r, jnp.where(r == length, 0, n - r))
    t = jnp.linspace(0.0, 1.0, length, dtype=F32)[tap][:, None]
    w = 2.0 * math.pi * tap.astype(F32)[:, None] / length
    bands = jnp.linspace(1e-4, HY_BANDS - 1, HY_BANDS, dtype=F32)
    z = jnp.concatenate([t, jnp.cos(bands * w), -jnp.sin(bands * w)], axis=-1)
    assert max(w1.shape) <= HY_PAD

    def pad_to(x, shape):
        return jnp.pad(x, [(0, s - d) for s, d in zip(shape, x.shape)])

    args = (pad_to(z, (n, HY_PAD)), t, pad_to(w1, (HY_PAD, HY_PAD)), pad_to(b1[None, :], (1, HY_PAD)),
            pad_to(w2, (HY_PAD, HY_PAD)), pad_to(b2[None, :], (1, HY_PAD)), pad_to(freq[None, :], (1, HY_PAD)),
            pad_to(w3, (HY_PAD, w3.shape[1])), decay)
    const = lambda i: (0, 0)
    return pl.pallas_call(
        functools.partial(_hy_filter_kernel, length=length), grid=(n // DFT_ROWS,),
        in_specs=[pl.BlockSpec((DFT_ROWS, HY_PAD), lambda i: (i, 0)), pl.BlockSpec((DFT_ROWS, 1), lambda i: (i, 0)),
                  pl.BlockSpec((HY_PAD, HY_PAD), const), pl.BlockSpec((1, HY_PAD), const),
                  pl.BlockSpec((HY_PAD, HY_PAD), const), pl.BlockSpec((1, HY_PAD), const),
                  pl.BlockSpec((1, HY_PAD), const), pl.BlockSpec((HY_PAD, w3.shape[1]), const),
                  pl.BlockSpec(decay.shape, const)],
        out_specs=[pl.BlockSpec((HY_ORDER, DFT_ROWS, width), lambda i: (0, i, 0)),
                   pl.BlockSpec((HY_ORDER, width), const)],
        out_shape=[jax.ShapeDtypeStruct((HY_ORDER, n, width), F32),
                   jax.ShapeDtypeStruct((HY_ORDER, width), F32)],
        compiler_params=_cparams(1), name="hyena_filter")(*args)


DFT_ROWS = ROW_G
GRP = 8
HY_LANES = 256
HY_ROWS = 16
HY_STEP_ROWS = 64


def _group_chunks(rows, width):
    return [(slice(r, r + HY_ROWS), slice(c, c + HY_LANES))
            for r in range(0, rows, HY_ROWS) for c in range(0, width, HY_LANES)]


def _grouped_matmul(w, pieces):
    pieces = [p.astype(F32) for p in pieces]
    outs = []
    for r in range(0, HY_ROWS, GRP):
        rhs = jnp.concatenate([p[r:r + GRP] for p in pieces], axis=0).astype(BF16)
        outs.append(jnp.dot(w, rhs, preferred_element_type=F32))
    return outs


def _piece(outs, idx):
    return jnp.concatenate([o[idx * GRP:(idx + 1) * GRP] for o in outs], axis=0)


def _hy_stage_a_kernel(x_ref, w_ref, ar_ref, ai_ref, *, tile0, n_in):
    planes = x_ref.shape[0]
    n_a = ar_ref.shape[0]
    for g, c in _group_chunks(ar_ref.shape[1], ar_ref.shape[2]):
        outs = _grouped_matmul(w_ref[...], [x_ref[q, tile0 + a, g, c] for q in range(planes) for a in range(n_in)])
        for k in range(n_a):
            ar_ref[k, g, c] = _piece(outs, k).astype(ar_ref.dtype)
            ai_ref[k, g, c] = _piece(outs, n_a + k).astype(ai_ref.dtype)


def hy_stage_a(x5, wmat, col0, width, tile0, n_in):
    planes, npair, ntile, rows, _ = x5.shape
    n_a = wmat.shape[0] // (2 * GRP)
    assert wmat.shape[1] == planes * n_in * GRP
    rb = HY_STEP_ROWS
    kern = functools.partial(_hy_stage_a_kernel, tile0=tile0, n_in=n_in)
    ospec = pl.BlockSpec((None, n_a, rb, width), lambda p, r: (p, 0, r, 0))
    return pl.pallas_call(
        kern, grid=(npair, rows // rb),
        in_specs=[pl.BlockSpec((planes, None, ntile, rb, width), lambda p, r: (0, p, 0, r, col0 // width)),
                  pl.BlockSpec(wmat.shape, lambda p, r: (0, 0))],
        out_specs=[ospec, ospec],
        out_shape=[jax.ShapeDtypeStruct((npair, n_a, rows, width), BF16)] * 2,
        compiler_params=_cparams(2), name="hyena_stage_a")(x5, wmat)


def _hy_stage_b_kernel(ar_ref, ai_ref, f_ref, *rest, spectrum_only):
    half = ar_ref.shape[0]
    a = jnp.concatenate([ar_ref[...], ai_ref[...]], axis=0)
    x = jnp.dot(f_ref[...], a, preferred_element_type=F32)
    if spectrum_only:
        norm_ref, or_ref, oi_ref = rest
        inv = 1.0 / norm_ref[...]
        or_ref[...] = x[:half] * inv
        oi_ref[...] = x[half:] * inv
        return
    g_ref, hr_ref, hi_ref, or_ref, oi_ref = rest
    xr, xi = x[:half], x[half:]
    hr, hi = hr_ref[...], hi_ref[...]
    y = jnp.concatenate([xr * hr - xi * hi, xr * hi + xi * hr], axis=0).astype(BF16)
    z = jnp.dot(g_ref[...], y, preferred_element_type=F32)
    or_ref[...] = z[:half].astype(or_ref.dtype)
    oi_ref[...] = z[half:].astype(oi_ref.dtype)


def hy_stage_b(ar, ai, fmat, gmat=None, hr=None, hi=None, order=None, norm=None):
    npair, n_a, rows, width = ar.shape
    spectrum_only = gmat is None
    aspec = pl.BlockSpec((None, None, rows, width), lambda k, p: (p, k, 0, 0))
    mspec = pl.BlockSpec((None, 2 * rows, 2 * rows), lambda k, p: (k, 0, 0))
    in_specs = [aspec, aspec, mspec]
    args = [ar, ai, fmat]
    if spectrum_only:
        in_specs.append(pl.BlockSpec((None, 1, width), lambda k, p: (p, 0, 0)))
        args.append(norm.reshape(npair, 1, width))
    else:
        hspec = pl.BlockSpec((None, None, rows, width), lambda k, p: (order, k, 0, 0))
        in_specs += [mspec, hspec, hspec]
        args += [gmat, hr, hi]
    out_dtype = F32 if spectrum_only else BF16
    return pl.pallas_call(
        functools.partial(_hy_stage_b_kernel, spectrum_only=spectrum_only), grid=(n_a, npair),
        in_specs=in_specs, out_specs=[aspec, aspec],
        out_shape=[jax.ShapeDtypeStruct(ar.shape, out_dtype)] * 2,
        compiler_params=_cparams(2), name="hyena_stage_b")(*args)


def _hy_stage_a_inv_kernel(ar_ref, ai_ref, w_ref, u_ref, g_ref, skip_ref, o_ref, *, tile0_u, tile0_g):
    n_a = ar_ref.shape[0]
    n_out = o_ref.shape[1]
    skip = skip_ref[...]
    for g, c in _group_chunks(ar_ref.shape[1], ar_ref.shape[2]):
        outs = _grouped_matmul(w_ref[...], [ar_ref[k, g, c] for k in range(n_a)]
                               + [ai_ref[k, g, c] for k in range(n_a)])
        for part in range(2):
            for a in range(n_out):
                conv = (_piece(outs, part * n_out + a)
                        + skip[:, c] * u_ref[part, tile0_u + a, g, c].astype(F32))
                gate = g_ref[part, tile0_g + a, g, c].astype(F32)
                o_ref[part, a, g, c] = (gate * conv).astype(o_ref.dtype)


def hy_stage_a_inv(ar, ai, wmat, u5, u_col0, tile0_u, g5, g_col0, tile0_g, skip, out_dtype):
    npair, n_a, rows, width = ar.shape
    n_out = n_a // 2
    assert wmat.shape == (2 * n_out * GRP, 2 * n_a * GRP)
    rb = HY_STEP_ROWS
    kern = functools.partial(_hy_stage_a_inv_kernel, tile0_u=tile0_u, tile0_g=tile0_g)
    aspec = pl.BlockSpec((None, n_a, rb, width), lambda p, r: (p, 0, r, 0))
    return pl.pallas_call(
        kern, grid=(npair, rows // rb),
        in_specs=[aspec, aspec, pl.BlockSpec(wmat.shape, lambda p, r: (0, 0)),
                  pl.BlockSpec((2, None, u5.shape[2], rb, width), lambda p, r: (0, p, 0, r, u_col0 // width)),
                  pl.BlockSpec((2, None, g5.shape[2], rb, width), lambda p, r: (0, p, 0, r, g_col0 // width)),
                  pl.BlockSpec((1, width), lambda p, r: (0, 0))],
        out_specs=pl.BlockSpec((2, None, n_out, rb, width), lambda p, r: (0, p, 0, r, 0)),
        out_shape=jax.ShapeDtypeStruct((2, npair, n_out, rows, width), out_dtype),
        compiler_params=_cparams(2), name="hyena_stage_a_inv")(ar, ai, wmat, u5, g5, skip.reshape(1, width))


def stage_a_matrices(n_a):
    h = n_a // 2
    idx = jnp.arange(n_a, dtype=jnp.int32)
    ang = (-2.0 * math.pi / n_a) * ((idx[:, None] * idx[None, :]) % n_a).astype(F32)
    wr, wi = jnp.cos(ang), jnp.sin(ang)
    eye = jnp.eye(GRP, dtype=F32)

    def expand(blocks):
        return jnp.kron(jnp.block(blocks), eye).astype(BF16)

    fwd = expand([[wr[:, :h], -wi[:, :h]], [wi[:, :h], wr[:, :h]]])
    fwd_real = expand([[wr], [wi]])
    inv = expand([[wr[:h], wi[:h]], [-wi[:h], wr[:h]]])
    return fwd, fwd_real, inv


def dft_matrices(n_a):
    n = n_a * DFT_ROWS
    kb = jnp.arange(DFT_ROWS, dtype=jnp.int32)[None, :, None]
    b = jnp.arange(DFT_ROWS, dtype=jnp.int32)[None, None, :]
    ka = jnp.arange(n_a, dtype=jnp.int32)[:, None, None]
    ang = (-2.0 * math.pi / n) * ((b * (ka + n_a * kb)) % n).astype(F32)
    fr, fi = jnp.cos(ang), jnp.sin(ang)
    blk = jnp.concatenate([jnp.concatenate([fr, -fi], axis=2), jnp.concatenate([fi, fr], axis=2)], axis=1)
    return blk.astype(BF16), (jnp.swapaxes(blk, 1, 2) * (1.0 / n)).astype(BF16)


def hyena_branch(u5, width, tile0, n_tiles, fargs, skip, dft):
    length = n_tiles * DFT_ROWS
    n_a = 2 * n_tiles
    fmat, gmat, a_fwd, a_fwd_real, a_inv = dft
    hc, norm = hyena_conv_filters(length, *fargs)
    hc = hc.reshape(1, HY_ORDER, n_a, DFT_ROWS, width)
    hr, hi = hy_stage_b(*hy_stage_a(hc, a_fwd_real, 0, width, 0, n_a), fmat, norm=norm)
    x5, x_col0, x_tile0 = u5, 0, tile0
    out = None
    for order in range(HY_ORDER):
        ar, ai = hy_stage_a(x5, a_fwd, x_col0, width, x_tile0, n_tiles)
        ar, ai = hy_stage_b(ar, ai, fmat, gmat, hr, hi, order=order)
        out = hy_stage_a_inv(ar, ai, a_inv, x5, x_col0, x_tile0, u5, (order + 1) * width, tile0, skip[order],
                             BF16)
        x5, x_col0, x_tile0 = out, 0, 0
    return out


def kernel(x, c, ctx, c_ctx, w_mod, b_mod, g_mix, g_ffn, w_in, b_gate, w_br, w_out, da_lambda, da_subln_g,
           lru_conv_w, lru_conv_b, lru_wa, lru_ba, lru_wi, lru_bi, lru_lambda, hy_conv_w, hy_conv_b,
           hy_f_w1, hy_f_b1, hy_f_w2, hy_f_b2, hy_f_freq, hy_f_w3, hy_decay, hy_skip, ffn_w_gate, ffn_w_up,
           ffn_w_down, moe_router, moe_w_gate, moe_w_up, moe_w_down, g_final):
    n_batch, length, d = x.shape
    n_ctx = ctx.shape[1]
    depth = w_mod.shape[0]
    assert n_ctx == ROW_G and length % ROW_G == 0 and d % 128 == 0
    tb = n_ctx + length
    gpb = tb // ROW_G
    nt = n_batch * tb
    c_end = w_in.shape[2]
    cb_k, cb_v, cb_lx, cb_q, cb_ly, cb_hy, cb_g = 0, 1, 2, 3, 4, 5, 8
    assert c_end == 11 * d

    xs = jnp.concatenate([ctx, x], axis=1).reshape(nt, d)
    silu_rows = jnp.concatenate([jax.nn.silu(c), jax.nn.silu(c_ctx)[None, :],
                                 jnp.zeros((16 - n_batch - 1, d), F32)], axis=0)
    tables = rope_tables(length, n_ctx, ROPE_LANES)
    dft_ctx = dft_matrices(2) + stage_a_matrices(2)
    dft_lat = dft_matrices(2 * (gpb - 1)) + stage_a_matrices(2 * (gpb - 1))

    for li in range(depth):
        lam_init = 0.8 - 0.6 * math.exp(-0.3 * li)
        tab = mm_f32_bias(silu_rows, w_mod[li], b_mod[li], d)
        tab = tab.reshape(16, 6, d).transpose(1, 0, 2)

        h = norm_mod(xs, g_mix[li], tab, 0, 1, gpb, n_batch)
        p = in_proj(h, w_in[li].astype(BF16), tables, gpb, cb_k, cb_q)
        a_out = diff_attention(p, d, da_lambda[li], da_subln_g[li], lam_init, n_batch, gpb, cb_q, cb_k, cb_v)

        bw = d // LRU_BLOCKS
        r_out = None
        hf = None
        for direction in range(2):
            w_cat = jnp.concatenate([lru_wa[li, direction], lru_wi[li, direction]], axis=-1).astype(BF16)
            res = lru_pass(p, lru_conv_w[li], lru_conv_b[li], w_cat, lru_ba[li, direction],
                           lru_bi[li, direction], lru_lambda[li, direction], n_batch, gpb, cb_lx,
                           reverse=direction == 1, hf=hf, cb_y=cb_ly)
            if direction == 0:
                hf = res
            else:
                r_out = res
        del bw

        u = hyena_short_conv(p, hy_conv_w[li], hy_conv_b[li], n_batch, gpb, cb_hy, d)
        u5 = u.reshape(2, n_batch // 2, gpb, ROW_G, 3 * d)
        fargs = (hy_f_w1[li], hy_f_b1[li], hy_f_w2[li], hy_f_b2[li], hy_f_freq[li], hy_f_w3[li], hy_decay[li])
        y_c = hyena_branch(u5, d, 0, 1, fargs, hy_skip[li], dft_ctx)
        y_l = hyena_branch(u5, d, 1, gpb - 1, fargs, hy_skip[li], dft_lat)
        y_out = jnp.concatenate([y_c, y_l], axis=2).reshape(nt, d)

        xs = merge_branches(xs, a_out, r_out, y_out, p, b_gate[li], w_br[li].astype(BF16),
                            w_out[li].astype(BF16), tab, 2, gpb, n_batch, cb_g)

        jj = li // 2
        if li % 2 == 0:
            f = norm_mod(xs, g_ffn[li], tab, 3, 4, gpb, n_batch)
            hh = gate_up(f, ffn_w_gate[jj].astype(BF16), ffn_w_up[jj].astype(BF16))
            xs = down_residual(hh, ffn_w_down[jj].astype(BF16), xs, tab, 5, gpb, n_batch)
        else:
            wr = jnp.concatenate([moe_router[jj], jnp.zeros((d, 128 - N_EXPERTS), F32)], axis=1)
            f, logits = norm_mod(xs, g_ffn[li], tab, 3, 4, gpb, n_batch, w_router=wr)
            xs = moe_layer(f, logits, moe_w_gate[jj].astype(BF16), moe_w_up[jj].astype(BF16),
                           moe_w_down[jj].astype(BF16), xs, tab, 5, gpb, n_batch)

    return final_norm(xs.reshape(n_batch, tb, d), g_final, n_ctx)
```

```python
import functools
import math

import jax
import jax.numpy as jnp
from jax import lax
from jax.experimental import pallas as pl
from jax.experimental.pallas import tpu as pltpu

F32 = jnp.float32
BF16 = jnp.bfloat16

EPS = 1e-6
ROW_G = 256
GRID_W = 64
ROPE_BASE = 10000.0
DA_HEADS = 8
DA_HEAD_DIM = 64
DA_V_DIM = 128
LRU_BLOCKS = 8
LRU_C = 8.0
HY_ORDER = 2
HY_BANDS = 16
N_EXPERTS = 8
TOP_K = 2
MOE_BLOCK = 256
HALO = 16
VMEM_LIMIT = 56 * 1024 * 1024


def _cparams(n_axes):
    return pltpu.CompilerParams(dimension_semantics=("arbitrary",) * n_axes,
                                vmem_limit_bytes=VMEM_LIMIT)


def _pick(n, prefs):
    for p in prefs:
        if n % p == 0:
            return p
    raise ValueError(f"no tile in {prefs} divides {n}")


def _mod_row(tab_ref, k, group, groups_per_batch, n_batch):
    b = group // groups_per_batch
    row = jnp.where(group % groups_per_batch == 0, n_batch, b)
    return tab_ref[k, pl.ds(row, 1), :]


def _mm_f32_kernel(a_ref, w_ref, b_ref, o_ref):
    o_ref[...] = jnp.dot(a_ref[...], w_ref[...], preferred_element_type=F32) + b_ref[...]


def mm_f32_bias(a, w, b, tn):
    m, k = a.shape
    n = w.shape[1]
    return pl.pallas_call(
        _mm_f32_kernel, grid=(n // tn,),
        in_specs=[pl.BlockSpec((m, k), lambda j: (0, 0)),
                  pl.BlockSpec((k, tn), lambda j: (0, j)),
                  pl.BlockSpec((1, tn), lambda j: (0, j))],
        out_specs=pl.BlockSpec((m, tn), lambda j: (0, j)),
        out_shape=jax.ShapeDtypeStruct((m, n), F32),
        compiler_params=_cparams(1), name="mod_matmul")(a, w, b.reshape(1, n))


def _norm_mod_kernel(x_ref, g_ref, tab_ref, *rest, k_shift, k_scale, gpb, n_batch, router):
    if router:
        wr_ref, o_ref, lg_ref = rest
    else:
        (o_ref,) = rest
    nsub = x_ref.shape[0] // ROW_G
    for s in range(nsub):
        grp = pl.program_id(0) * nsub + s
        rows = pl.ds(s * ROW_G, ROW_G)
        xs = x_ref[rows, :]
        y = xs * lax.rsqrt(jnp.mean(xs * xs, axis=-1, keepdims=True) + EPS) * g_ref[...]
        shift = _mod_row(tab_ref, k_shift, grp, gpb, n_batch)
        scale = _mod_row(tab_ref, k_scale, grp, gpb, n_batch)
        h = y * (1.0 + scale) + shift
        o_ref[rows, :] = h.astype(o_ref.dtype)
        if router:
            lg_ref[rows, :] = jnp.dot(h, wr_ref[...], preferred_element_type=F32,
                                      precision=lax.Precision.HIGHEST)


def norm_mod(x, g, tab, k_shift, k_scale, gpb, n_batch, w_router=None):
    nt, d = x.shape
    tm = _pick(nt, (1024, 512, 256))
    router = w_router is not None
    kern = functools.partial(_norm_mod_kernel, k_shift=k_shift, k_scale=k_scale, gpb=gpb,
                             n_batch=n_batch, router=router)
    in_specs = [pl.BlockSpec((tm, d), lambda i: (i, 0)),
                pl.BlockSpec((1, d), lambda i: (0, 0)),
                pl.BlockSpec(tab.shape, lambda i: (0, 0, 0))]
    out_specs = [pl.BlockSpec((tm, d), lambda i: (i, 0))]
    out_shape = [jax.ShapeDtypeStruct((nt, d), BF16)]
    args = [x, g.reshape(1, d), tab]
    if router:
        in_specs.append(pl.BlockSpec(w_router.shape, lambda i: (0, 0)))
        out_specs.append(pl.BlockSpec((tm, w_router.shape[1]), lambda i: (i, 0)))
        out_shape.append(jax.ShapeDtypeStruct((nt, w_router.shape[1]), F32))
        args.append(w_router)
    out = pl.pallas_call(kern, grid=(nt // tm,), in_specs=in_specs, out_specs=out_specs,
                         out_shape=out_shape, compiler_params=_cparams(1), name="norm_mod")(*args)
    return out if router else out[0]


def _final_norm_kernel(x_ref, g_ref, o_ref):
    xs = x_ref[...]
    o_ref[...] = xs * lax.rsqrt(jnp.mean(xs * xs, axis=-1, keepdims=True) + EPS) * g_ref[...]


def final_norm(x3, g, n_ctx):
    b, tb, d = x3.shape
    length = tb - n_ctx
    tm = ROW_G
    off = n_ctx // tm
    return pl.pallas_call(
        _final_norm_kernel, grid=(b, length // tm),
        in_specs=[pl.BlockSpec((None, tm, d), lambda i, j: (i, j + off, 0)),
                  pl.BlockSpec((1, d), lambda i, j: (0, 0))],
        out_specs=pl.BlockSpec((None, tm, d), lambda i, j: (i, j, 0)),
        out_shape=jax.ShapeDtypeStruct((b, length, d), F32),
        compiler_params=_cparams(2), name="final_norm")(x3, g.reshape(1, d))


ROPE_LANES = 128


def _in_proj_kernel(a_ref, w_ref, cos_ref, sa_ref, sb_ref, o_ref, *, cb_k, cb_q, q_scale, gpb):
    n = pl.program_id(0)
    is_rot = jnp.logical_or(n == cb_k, n == cb_q)

    @pl.when(jnp.logical_not(is_rot))
    def _():
        o_ref[...] = jnp.dot(a_ref[...], w_ref[...], preferred_element_type=F32).astype(o_ref.dtype)

    @pl.when(is_rot)
    def _():
        scale = jnp.where(n == cb_q, q_scale, 1.0)
        nsub = a_ref.shape[0] // ROW_G
        for s in range(nsub):
            rows = slice(s * ROW_G, (s + 1) * ROW_G)
            acc = jnp.dot(a_ref[rows, :], w_ref[...], preferred_element_type=F32)
            j = (pl.program_id(1) * nsub + s) % gpb
            trow = pl.ds(pl.multiple_of(j * ROW_G, ROW_G), ROW_G)
            cos, sa, sb = cos_ref[trow, :] * scale, sa_ref[trow, :] * scale, sb_ref[trow, :] * scale
            for c in range(acc.shape[1] // ROPE_LANES):
                t = acc[:, c * ROPE_LANES:(c + 1) * ROPE_LANES]
                rot = (t * cos + pltpu.roll(t, ROPE_LANES - 16, 1) * sa + pltpu.roll(t, 16, 1) * sb)
                o_ref[rows, c * ROPE_LANES:(c + 1) * ROPE_LANES] = rot.astype(o_ref.dtype)


def in_proj(h, w, tables, gpb, cb_k, cb_q):
    m, k = h.shape
    n = w.shape[1]
    tm = _pick(m, (1024, 512, 256))
    tn = k
    cos, sa, sb = tables
    kern = functools.partial(_in_proj_kernel, cb_k=cb_k, cb_q=cb_q, gpb=gpb,
                             q_scale=math.log2(math.e) * DA_HEAD_DIM ** -0.5)
    tspec = pl.BlockSpec(cos.shape, lambda j, i: (0, 0))
    return pl.pallas_call(
        kern, grid=(n // tn, m // tm),
        in_specs=[pl.BlockSpec((tm, k), lambda j, i: (i, 0)),
                  pl.BlockSpec((k, tn), lambda j, i: (0, j)), tspec, tspec, tspec],
        out_specs=pl.BlockSpec((tm, tn), lambda j, i: (i, j)),
        out_shape=jax.ShapeDtypeStruct((m, n), BF16),
        compiler_params=_cparams(2), name="in_proj")(h, w, cos, sa, sb)


def rope_tables(length, n_ctx, width):
    rows = length // GRID_W
    row = jnp.repeat(jnp.arange(rows, dtype=F32), GRID_W)
    col = jnp.tile(jnp.arange(GRID_W, dtype=F32), rows)
    n_freq = DA_HEAD_DIM // 4
    inv = ROPE_BASE ** (-jnp.arange(n_freq, dtype=F32) / n_freq)
    ar = row[:, None] * inv
    ac = col[:, None] * inv
    ang = jnp.concatenate([ar, ar, ac, ac], axis=-1)
    ang = jnp.concatenate([jnp.zeros((n_ctx, DA_HEAD_DIM), F32), ang], axis=0)
    reps = width // DA_HEAD_DIM
    cos = jnp.tile(jnp.cos(ang), (1, reps))
    sin = jnp.tile(jnp.sin(ang), (1, reps))
    first = (jnp.arange(width) % (2 * n_freq)) < n_freq
    sa = jnp.where(first[None, :], -sin, 0.0)
    sb = jnp.where(first[None, :], 0.0, sin)
    return cos, sa, sb


def _attn_kernel(lamv_ref, g_ref, q_ref, k_ref, v_ref, o_ref, vaug_ref, m_ref, acc_ref,
                 *, lam_init, n_ctx, tk):
    j = pl.program_id(2)
    dv = v_ref.shape[1]

    @pl.when(j == 0)
    def _():
        vaug_ref[:, :dv] = v_ref[...]
        vaug_ref[:, dv:] = jnp.ones((vaug_ref.shape[0], vaug_ref.shape[1] - dv), vaug_ref.dtype)

    q = q_ref[...]
    lane = lax.broadcasted_iota(jnp.int32, q.shape, 1)
    zero = jnp.zeros_like(q)
    qq = jnp.concatenate([jnp.where(lane < DA_HEAD_DIM, q, zero),
                          jnp.where(lane >= DA_HEAD_DIM, q, zero)], axis=0)
    tq = q.shape[0]

    def scores(start, size):
        kc = k_ref[pl.ds(start, size), :]
        return lax.dot_general(qq, kc, (((1,), (1,)), ((), ())), preferred_element_type=F32)

    def update(s, start, size, first=False):
        m_new = jnp.max(s, axis=-1, keepdims=True)
        if first:
            m_new = jnp.broadcast_to(m_new, m_ref.shape)
        else:
            m_prev = m_ref[...]
            m_new = jnp.maximum(m_prev, m_new)
        p = jnp.exp2(s - jnp.concatenate([m_new] * (size // 128), axis=1))
        pv = jnp.dot(p.astype(BF16), vaug_ref[pl.ds(start, size), :], preferred_element_type=F32)
        if first:
            acc_ref[...] = pv
        else:
            alpha = jnp.exp2(m_prev - m_new)
            acc_ref[...] = jnp.concatenate([alpha] * (acc_ref.shape[1] // 128), axis=1) * acc_ref[...] + pv
        m_ref[...] = m_new

    s_ctx = scores(0, n_ctx)

    @pl.when(j == 0)
    def _():
        update(s_ctx, 0, n_ctx, first=True)

    @pl.when(j > 0)
    def _():
        starts = [n_ctx + i * tk for i in range((k_ref.shape[0] - n_ctx) // tk)]
        s_next = scores(starts[0], tk)
        update(s_ctx, 0, n_ctx, first=True)
        for i, start in enumerate(starts):
            s_cur = s_next
            if i + 1 < len(starts):
                s_next = scores(starts[i + 1], tk)
            update(s_cur, start, tk)

    lv = lamv_ref[...]
    lam = (jnp.exp(jnp.sum(lv[0:1] * lv[1:2], axis=-1, keepdims=True))
           - jnp.exp(jnp.sum(lv[2:3] * lv[3:4], axis=-1, keepdims=True)) + lam_init)
    o = acc_ref[:, :dv] / acc_ref[:, dv:]
    o = o[:tq] - lam * o[tq:]
    o = o * lax.rsqrt(jnp.mean(o * o, axis=-1, keepdims=True) + EPS) * g_ref[...]
    o_ref[...] = (o * (1.0 - lam_init)).astype(o_ref.dtype)


def diff_attention(p, width, lam_vecs, subln_g, lam_init, n_batch, gpb, cb_q, cb_k, cb_v):
    nt = p.shape[0]
    tb = gpb * ROW_G
    n_ctx = ROW_G
    tk = _pick(tb - n_ctx, (2048, 1024, 512, 256))
    kern = functools.partial(_attn_kernel, lam_init=lam_init, n_ctx=n_ctx, tk=tk)
    assert DA_V_DIM == 128
    hpb = width // DA_V_DIM
    return pl.pallas_call(
        kern, grid=(n_batch, DA_HEADS, gpb),
        in_specs=[pl.BlockSpec(lam_vecs.shape, lambda b, h, j: (0, 0)),
                  pl.BlockSpec((1, DA_V_DIM), lambda b, h, j: (0, 0)),
                  pl.BlockSpec((ROW_G, DA_V_DIM), lambda b, h, j: (b * gpb + j, cb_q * hpb + h)),
                  pl.BlockSpec((tb, DA_V_DIM), lambda b, h, j: (b, cb_k * hpb + h)),
                  pl.BlockSpec((tb, DA_V_DIM), lambda b, h, j: (b, cb_v * hpb + h))],
        out_specs=pl.BlockSpec((ROW_G, DA_V_DIM), lambda b, h, j: (b * gpb + j, h)),
        out_shape=jax.ShapeDtypeStruct((nt, width), BF16),
        scratch_shapes=[pltpu.VMEM((tb, 2 * DA_V_DIM), BF16), pltpu.VMEM((2 * ROW_G, 128), F32),
                        pltpu.VMEM((2 * ROW_G, 2 * DA_V_DIM), F32)],
        compiler_params=_cparams(3), name="diff_attn")(
            lam_vecs, subln_g.reshape(1, DA_V_DIM), p, p, p)


def _shift_matrix(n, d):
    row = lax.broadcasted_iota(jnp.int32, (n, n), 0)
    col = lax.broadcasted_iota(jnp.int32, (n, n), 1)
    return jnp.where(col == row + d, 1.0, 0.0).astype(BF16)


def _dwconv(main_ref, prev_ref, next_ref, w_ref, b_ref, left, has_prev, has_next):
    x = main_ref[...]
    acc = b_ref[...] + w_ref[left:left + 1, :] * x.astype(F32)
    prev8 = prev_ref[...].astype(F32)[HALO - 8:, :]
    next8 = next_ref[...].astype(F32)[:8, :]
    sub = lax.broadcasted_iota(jnp.int32, (8, 1), 0)
    first = jnp.zeros_like(prev8)
    last = jnp.zeros_like(next8)
    for j in range(w_ref.shape[0]):
        d = j - left
        if d == 0:
            continue
        wj = w_ref[j:j + 1, :]
        acc = acc + wj * jnp.dot(_shift_matrix(ROW_G, d), x, preferred_element_type=F32)
        if d < 0:
            mask = jnp.logical_and(sub < -d, has_prev)
            first = first + wj * jnp.where(mask, pltpu.roll(prev8, -d, 0), 0.0)
        else:
            mask = jnp.logical_and(sub >= 8 - d, has_next)
            last = last + wj * jnp.where(mask, pltpu.roll(next8, 8 - d, 0), 0.0)
    return jnp.concatenate([acc[0:8] + first, acc[8:ROW_G - 8], acc[ROW_G - 8:] + last], axis=0)


def _halo_specs(width, cb, gpb, tile_of, n_groups):
    per = ROW_G // HALO

    def main(b, s, *_):
        return (b * gpb + tile_of(s), cb)

    def prev(b, s, *_):
        return (jnp.maximum((b * gpb + tile_of(s)) * per - 1, 0), cb)

    def nxt(b, s, *_):
        return (jnp.minimum((b * gpb + tile_of(s) + 1) * per, n_groups * per - 1), cb)

    return [pl.BlockSpec((ROW_G, width), main), pl.BlockSpec((HALO, width), prev),
            pl.BlockSpec((HALO, width), nxt)]


def _seq_flags(j, gpb):
    return j >= 2, jnp.logical_and(j >= 1, j <= gpb - 2)


SCAN_G = 8


def _scan_groups(a, b, reverse):
    shape = a.shape
    a = a.reshape(shape[0] // SCAN_G, SCAN_G, shape[1])
    b = b.reshape(a.shape)
    pos = lax.broadcasted_iota(jnp.int32, a.shape, 1)
    s = 1
    while s < SCAN_G:
        keep = pos < SCAN_G - s if reverse else pos >= s
        shift = SCAN_G - s if reverse else s
        a_sh = jnp.where(keep, pltpu.roll(a, shift, 1), 1.0)
        b_sh = jnp.where(keep, pltpu.roll(b, shift, 1), 0.0)
        b = a * b_sh + b
        a = a * a_sh
        s *= 2
    return a.reshape(shape), b.reshape(shape)


def _gelu_tanh(x):
    return 0.5 * x * (1.0 + jnp.tanh(math.sqrt(2.0 / math.pi) * (x + 0.044715 * (x * x * x))))


def _lru_kernel(main_ref, prev_ref, next_ref, cw_ref, cb_ref, w_ref, ba_ref, bi_ref, lam_ref, *rest,
                gpb, reverse):
    if reverse:
        hf_ref, ly_ref, o_ref, carry_ref = rest
    else:
        o_ref, carry_ref = rest
    s = pl.program_id(1)
    j = jnp.where(s == 0, 0, gpb - s) if reverse else s
    has_prev, has_next = _seq_flags(j, gpb)
    xc = _dwconv(main_ref, prev_ref, next_ref, cw_ref, cb_ref, 2, has_prev, has_next)
    xcb = xc.astype(BF16)
    bw = w_ref.shape[1]
    r_parts, i_parts = [], []
    for n in range(w_ref.shape[0]):
        res = jnp.dot(xcb[:, n * bw:(n + 1) * bw], w_ref[n], preferred_element_type=F32)
        r_parts.append(res[:, :bw])
        i_parts.append(res[:, bw:])
    r = 0.5 * jnp.tanh(0.5 * (jnp.concatenate(r_parts, axis=1) + ba_ref[...])) + 0.5
    gate_i = 0.5 * jnp.tanh(0.5 * (jnp.concatenate(i_parts, axis=1) + bi_ref[...])) + 0.5
    nl = -lam_ref[...]
    softplus = jnp.maximum(nl, 0.0) + jnp.log(1.0 + jnp.exp(-jnp.abs(nl)))
    a = jnp.exp((-LRU_C) * r * softplus)
    bb = jnp.sqrt(1.0 - a * a) * (gate_i * xc)
    a_grp, h_grp = _scan_groups(a, bb, reverse)

    @pl.when(s == 0)
    def _():
        carry_ref[...] = jnp.zeros(carry_ref.shape, F32)

    h = carry_ref[...]
    n_grp = ROW_G // SCAN_G
    pieces = [None] * n_grp
    for g in (reversed(range(n_grp)) if reverse else range(n_grp)):
        rows = slice(g * SCAN_G, (g + 1) * SCAN_G)
        hg = a_grp[rows] * h + h_grp[rows]
        pieces[g] = hg
        h = hg[0:1, :] if reverse else hg[SCAN_G - 1:SCAN_G, :]
    carry_ref[...] = h
    h_all = jnp.concatenate(pieces, axis=0)
    if reverse:
        ly = ly_ref[...].astype(F32)
        o_ref[...] = (_gelu_tanh(ly) * (hf_ref[...].astype(F32) + h_all)).astype(o_ref.dtype)
    else:
        o_ref[...] = h_all.astype(o_ref.dtype)


def lru_pass(p, conv_w, conv_b, w_cat, ba, bi, lam, n_batch, gpb, cb_x, reverse, hf=None, cb_y=None):
    nt = p.shape[0]
    width = conv_w.shape[1]
    tile_of = (lambda s: jnp.where(s == 0, 0, gpb - s)) if reverse else (lambda s: s)
    kern = functools.partial(_lru_kernel, gpb=gpb, reverse=reverse)
    const2 = lambda b, s: (0, 0)
    in_specs = _halo_specs(width, cb_x, gpb, tile_of, nt // ROW_G) + [
        pl.BlockSpec(conv_w.shape, const2), pl.BlockSpec((1, width), const2),
        pl.BlockSpec(w_cat.shape, lambda b, s: (0, 0, 0)),
        pl.BlockSpec((1, width), const2), pl.BlockSpec((1, width), const2),
        pl.BlockSpec((1, width), const2)]
    args = [p, p, p, conv_w, conv_b.reshape(1, width), w_cat, ba.reshape(1, width),
            bi.reshape(1, width), lam.reshape(1, width)]
    row_spec = lambda cb: pl.BlockSpec((ROW_G, width), lambda b, s: (b * gpb + tile_of(s), cb))
    if reverse:
        in_specs += [row_spec(0), row_spec(cb_y)]
        args += [hf, p]
    return pl.pallas_call(
        kern, grid=(n_batch, gpb), in_specs=in_specs, out_specs=row_spec(0),
        out_shape=jax.ShapeDtypeStruct((nt, width), BF16),
        scratch_shapes=[pltpu.VMEM((1, width), F32)],
        compiler_params=_cparams(2), name="lru_bwd" if reverse else "lru_fwd")(*args)


def _short_conv_kernel(*refs, gpb, ncb):
    cw_ref, cb_ref, o_ref = refs[3 * ncb:]
    j = pl.program_id(1)
    has_prev, has_next = _seq_flags(j, gpb)
    width = o_ref.shape[1] // ncb
    for c in range(ncb):
        main_ref, prev_ref, next_ref = refs[3 * c:3 * c + 3]
        cols = pl.ds(c * width, width)
        o_ref[:, cols] = _dwconv(main_ref, prev_ref, next_ref, cw_ref.at[:, cols], cb_ref.at[:, cols], 1,
                                 has_prev, has_next).astype(o_ref.dtype)


def hyena_short_conv(p, conv_w, conv_b, n_batch, gpb, cb0, width):
    nt = p.shape[0]
    total = conv_w.shape[1]
    ncb = total // width
    halos = []
    for c in range(ncb):
        halos += _halo_specs(width, cb0 + c, gpb, lambda s: s, nt // ROW_G)
    kern = functools.partial(_short_conv_kernel, gpb=gpb, ncb=ncb)
    return pl.pallas_call(
        kern, grid=(n_batch, gpb),
        in_specs=halos + [pl.BlockSpec(conv_w.shape, lambda b, j: (0, 0)),
                          pl.BlockSpec((1, total), lambda b, j: (0, 0))],
        out_specs=pl.BlockSpec((ROW_G, total), lambda b, j: (b * gpb + j, 0)),
        out_shape=jax.ShapeDtypeStruct((nt, total), BF16),
        compiler_params=_cparams(2), name="hyena_short_conv")(*([p] * (3 * ncb)), conv_w,
                                                              conv_b.reshape(1, total))


def _merge_kernel(a_ref, r_ref, y_ref, g0_ref, g1_ref, g2_ref, bg_ref, wbr_ref, wout_ref, tab_ref, x_ref,
                  o_ref, *, k_gate, gpb, n_batch):
    width = a_ref.shape[1]
    m = None
    for k, (br_ref, gt_ref) in enumerate(((a_ref, g0_ref), (r_ref, g1_ref), (y_ref, g2_ref))):
        gate = jax.nn.sigmoid(gt_ref[...].astype(F32) + bg_ref[:, k * width:(k + 1) * width])
        term = gate * jnp.dot(br_ref[...], wbr_ref[k], preferred_element_type=F32)
        m = term if m is None else m + term
    out = jnp.dot(m.astype(BF16), wout_ref[...], preferred_element_type=F32)
    nsub = x_ref.shape[0] // ROW_G
    for s in range(nsub):
        rows = pl.ds(s * ROW_G, ROW_G)
        gmod = _mod_row(tab_ref, k_gate, pl.program_id(0) * nsub + s, gpb, n_batch)
        o_ref[rows, :] = x_ref[rows, :] + gmod * out[s * ROW_G:(s + 1) * ROW_G, :]


def merge_branches(x, a, r, y, p, b_gate, w_br, w_out, tab, k_gate, gpb, n_batch, cb_g):
    nt, d = x.shape
    tm = _pick(nt, (512, 256))
    kern = functools.partial(_merge_kernel, k_gate=k_gate, gpb=gpb, n_batch=n_batch)
    row = pl.BlockSpec((tm, d), lambda i: (i, 0))
    gspec = lambda k: pl.BlockSpec((tm, d), lambda i: (i, cb_g + k))
    return pl.pallas_call(
        kern, grid=(nt // tm,),
        in_specs=[row, row, row, gspec(0), gspec(1), gspec(2),
                  pl.BlockSpec((1, 3 * d), lambda i: (0, 0)),
                  pl.BlockSpec(w_br.shape, lambda i: (0, 0, 0)),
                  pl.BlockSpec(w_out.shape, lambda i: (0, 0)),
                  pl.BlockSpec(tab.shape, lambda i: (0, 0, 0)), row],
        out_specs=row, out_shape=jax.ShapeDtypeStruct((nt, d), F32),
        compiler_params=_cparams(1), name="merge")(a, r, y, p, p, p, b_gate.reshape(1, 3 * d), w_br, w_out, tab, x)


def _gate_up_kernel(*refs, expert):
    if expert:
        _, f_ref, wg_ref, wu_ref, o_ref = refs
    else:
        f_ref, wg_ref, wu_ref, o_ref = refs
    f = f_ref[...]
    g = jnp.dot(f, wg_ref[...], preferred_element_type=F32)
    u = jnp.dot(f, wu_ref[...], preferred_element_type=F32)
    o_ref[...] = (g * jax.nn.sigmoid(g) * u).astype(o_ref.dtype)


def gate_up(f, wg, wu, block_e=None):
    m, d = f.shape
    dff = wg.shape[-1]
    tn = _pick(dff, (1408, 1024, 512, 256, 128))
    if block_e is None:
        tm = _pick(m, (512, 256))
        grid_spec = pltpu.PrefetchScalarGridSpec(
            num_scalar_prefetch=0, grid=(dff // tn, m // tm),
            in_specs=[pl.BlockSpec((tm, d), lambda j, i: (i, 0)),
                      pl.BlockSpec((d, tn), lambda j, i: (0, j)),
                      pl.BlockSpec((d, tn), lambda j, i: (0, j))],
            out_specs=pl.BlockSpec((tm, tn), lambda j, i: (i, j)))
        args = (f, wg, wu)
    else:
        tm = MOE_BLOCK
        grid_spec = pltpu.PrefetchScalarGridSpec(
            num_scalar_prefetch=1, grid=(dff // tn, m // tm),
            in_specs=[pl.BlockSpec((tm, d), lambda j, i, be: (i, 0)),
                      pl.BlockSpec((None, d, tn), lambda j, i, be: (be[i], 0, j)),
                      pl.BlockSpec((None, d, tn), lambda j, i, be: (be[i], 0, j))],
            out_specs=pl.BlockSpec((tm, tn), lambda j, i, be: (i, j)))
        args = (block_e, f, wg, wu)
    return pl.pallas_call(
        functools.partial(_gate_up_kernel, expert=block_e is not None), grid_spec=grid_spec,
        out_shape=jax.ShapeDtypeStruct((m, dff), BF16),
        compiler_params=_cparams(2), name="gate_up")(*args)


def _down_res_kernel(h_ref, wd_ref, tab_ref, x_ref, o_ref, *, k_gate, gpb, n_batch):
    out = jnp.dot(h_ref[...], wd_ref[...], preferred_element_type=F32)
    nsub = x_ref.shape[0] // ROW_G
    for s in range(nsub):
        rows = pl.ds(s * ROW_G, ROW_G)
        gmod = _mod_row(tab_ref, k_gate, pl.program_id(0) * nsub + s, gpb, n_batch)
        o_ref[rows, :] = x_ref[rows, :] + gmod * out[s * ROW_G:(s + 1) * ROW_G, :]


def down_residual(h, wd, x, tab, k_gate, gpb, n_batch):
    nt, d = x.shape
    dff = h.shape[1]
    tm = _pick(nt, (512, 256))
    kern = functools.partial(_down_res_kernel, k_gate=k_gate, gpb=gpb, n_batch=n_batch)
    return pl.pallas_call(
        kern, grid=(nt // tm,),
        in_specs=[pl.BlockSpec((tm, dff), lambda i: (i, 0)),
                  pl.BlockSpec((dff, d), lambda i: (0, 0)),
                  pl.BlockSpec(tab.shape, lambda i: (0, 0, 0)),
                  pl.BlockSpec((tm, d), lambda i: (i, 0))],
        out_specs=pl.BlockSpec((tm, d), lambda i: (i, 0)),
        out_shape=jax.ShapeDtypeStruct((nt, d), F32),
        compiler_params=_cparams(1), name="down_residual")(h, wd, tab, x)


def _expert_down_kernel(be_ref, h_ref, wd_ref, o_ref):
    o_ref[...] = jnp.dot(h_ref[...], wd_ref[...], preferred_element_type=F32).astype(o_ref.dtype)


def expert_down(h, wd, block_e):
    m, dff = h.shape
    d = wd.shape[-1]
    grid_spec = pltpu.PrefetchScalarGridSpec(
        num_scalar_prefetch=1, grid=(m // MOE_BLOCK,),
        in_specs=[pl.BlockSpec((MOE_BLOCK, dff), lambda i, be: (i, 0)),
                  pl.BlockSpec((None, dff, d), lambda i, be: (be[i], 0, 0))],
        out_specs=pl.BlockSpec((MOE_BLOCK, d), lambda i, be: (i, 0)))
    return pl.pallas_call(_expert_down_kernel, grid_spec=grid_spec,
                          out_shape=jax.ShapeDtypeStruct((m, d), BF16),
                          compiler_params=_cparams(1), name="expert_down")(block_e, h, wd)


def _combine_kernel(y0_ref, y1_ref, w0_ref, w1_ref, tab_ref, x_ref, o_ref, *, k_gate, gpb, n_batch):
    nsub = x_ref.shape[0] // ROW_G
    for s in range(nsub):
        rows = pl.ds(s * ROW_G, ROW_G)
        gmod = _mod_row(tab_ref, k_gate, pl.program_id(0) * nsub + s, gpb, n_batch)
        y = (y0_ref[rows, :].astype(F32) * w0_ref[rows, :]
             + y1_ref[rows, :].astype(F32) * w1_ref[rows, :])
        o_ref[rows, :] = x_ref[rows, :] + gmod * y


def moe_combine(y0, y1, w0, w1, x, tab, k_gate, gpb, n_batch):
    nt, d = x.shape
    tm = _pick(nt, (512, 256))
    kern = functools.partial(_combine_kernel, k_gate=k_gate, gpb=gpb, n_batch=n_batch)
    row = pl.BlockSpec((tm, d), lambda i: (i, 0))
    wsp = pl.BlockSpec((tm, 1), lambda i: (i, 0))
    return pl.pallas_call(
        kern, grid=(nt // tm,),
        in_specs=[row, row, wsp, wsp, pl.BlockSpec(tab.shape, lambda i: (0, 0, 0)), row],
        out_specs=row, out_shape=jax.ShapeDtypeStruct((nt, d), F32),
        compiler_params=_cparams(1), name="moe_combine")(y0, y1, w0, w1, tab, x)


def moe_layer(f, logits, w_gate, w_up, w_down, x, tab, k_gate, gpb, n_batch):
    nt = f.shape[0]
    n_asg = nt * TOP_K
    top_v, top_i = lax.top_k(logits[:, :N_EXPERTS], TOP_K)
    top_w = jax.nn.softmax(top_v, axis=-1)
    e_flat = top_i.reshape(-1)
    onehot = (e_flat[:, None] == jnp.arange(N_EXPERTS, dtype=e_flat.dtype)[None, :]).astype(jnp.int32)
    csum = jnp.cumsum(onehot, axis=0)
    counts = csum[-1]
    rank = jnp.take_along_axis(csum, e_flat[:, None], axis=1)[:, 0] - 1
    padded = ((counts + MOE_BLOCK - 1) // MOE_BLOCK) * MOE_BLOCK
    pad_end = jnp.cumsum(padded)
    pad_start = pad_end - padded
    dest = (pad_start[e_flat] + rank).astype(jnp.int32)
    n_blocks = -(-n_asg // MOE_BLOCK) + N_EXPERTS
    n_slots = n_blocks * MOE_BLOCK
    block_e = jnp.minimum(jnp.searchsorted(pad_end, jnp.arange(n_blocks, dtype=jnp.int32) * MOE_BLOCK,
                                           side='right'), N_EXPERTS - 1).astype(jnp.int32)
    order = jnp.argsort(e_flat, stable=True).astype(jnp.int32)
    e_slot = jnp.repeat(block_e, MOE_BLOCK)
    within = jnp.arange(n_slots, dtype=jnp.int32) - pad_start[e_slot].astype(jnp.int32)
    valid = within < counts[e_slot]
    src = jnp.where(valid, (jnp.cumsum(counts) - counts)[e_slot].astype(jnp.int32) + within, 0)
    slot_tok = jnp.where(valid, order[src] // TOP_K, 0)
    xs = f.at[slot_tok].get(mode='promise_in_bounds')
    h = gate_up(xs, w_gate, w_up, block_e)
    ys = expert_down(h, w_down, block_e)
    dest2 = dest.reshape(nt, TOP_K)
    y0 = ys.at[dest2[:, 0]].get(mode='promise_in_bounds')
    y1 = ys.at[dest2[:, 1]].get(mode='promise_in_bounds')
    return moe_combine(y0, y1, top_w[:, 0:1], top_w[:, 1:2], x, tab, k_gate, gpb, n_batch)


HY_PAD = 128


def _hy_filter_kernel(z_ref, t_ref, w1_ref, b1_ref, w2_ref, b2_ref, fr_ref, w3_ref, dec_ref, hc_ref, norm_ref,
                      *, length):
    i = pl.program_id(0)
    rows = z_ref.shape[0]
    width = dec_ref.shape[1]
    hp = lax.Precision.HIGHEST
    fr = fr_ref[...]
    hdn = jnp.sin(fr * (jnp.dot(z_ref[...], w1_ref[...], precision=hp, preferred_element_type=F32) + b1_ref[...]))
    hdn = jnp.sin(fr * (jnp.dot(hdn, w2_ref[...], precision=hp, preferred_element_type=F32) + b2_ref[...]))
    w3 = w3_ref[...]
    h_hi, w_hi = hdn.astype(BF16), w3.astype(BF16)
    h_lo = (hdn - h_hi.astype(F32)).astype(BF16)
    w_lo = (w3 - w_hi.astype(F32)).astype(BF16)
    filt = (jnp.dot(h_hi, w_hi, preferred_element_type=F32) + jnp.dot(h_hi, w_lo, preferred_element_type=F32)
            + jnp.dot(h_lo, w_hi, preferred_element_type=F32))
    t = t_ref[...]
    r = i * rows + lax.broadcasted_iota(jnp.int32, (rows, 1), 0)
    first_half = r < length

    @pl.when(i == 0)
    def _():
        norm_ref[...] = jnp.full(norm_ref.shape, EPS, F32)

    for o in range(HY_ORDER):
        win = jnp.exp(-t * jnp.abs(dec_ref[o:o + 1, :]))
        fwd = filt[:, (2 * o) * width:(2 * o + 1) * width] * win
        bwd = filt[:, (2 * o + 1) * width:(2 * o + 2) * width] * win
        mass = jnp.where(first_half, jnp.abs(fwd) + jnp.abs(bwd), 0.0)
        norm_ref[o:o + 1, :] += jnp.sum(mass, axis=0, keepdims=True)
        hc = jnp.where(first_half, fwd, bwd) + jnp.where(r == 0, bwd, 0.0)
        hc_ref[o] = jnp.where(r == length, 0.0, hc)


def hyena_conv_filters(length, w1, b1, w2, b2, freq, w3, decay):
    n = 2 * length
    width = decay.shape[-1]
    r = jnp.arange(n, dtype=jnp.int32)
    tap = jnp.where(r < length, r, jnp.where(r == length, 0, n - r))
    t = jnp.linspace(0.0, 1.0, length, dtype=F32)[tap][:, None]
    w = 2.0 * math.pi * tap.astype(F32)[:, None] / length
    bands = jnp.linspace(1e-4, HY_BANDS - 1, HY_BANDS, dtype=F32)
    z = jnp.concatenate([t, jnp.cos(bands * w), -jnp.sin(bands * w)], axis=-1)
    assert max(w1.shape) <= HY_PAD

    def pad_to(x, shape):
        return jnp.pad(x, [(0, s - d) for s, d in zip(shape, x.shape)])

    args = (pad_to(z, (n, HY_PAD)), t, pad_to(w1, (HY_PAD, HY_PAD)), pad_to(b1[None, :], (1, HY_PAD)),
            pad_to(w2, (HY_PAD, HY_PAD)), pad_to(b2[None, :], (1, HY_PAD)), pad_to(freq[None, :], (1, HY_PAD)),
            pad_to(w3, (HY_PAD, w3.shape[1])), decay)
    const = lambda i: (0, 0)
    return pl.pallas_call(
        functools.partial(_hy_filter_kernel, length=length), grid=(n // DFT_ROWS,),
        in_specs=[pl.BlockSpec((DFT_ROWS, HY_PAD), lambda i: (i, 0)), pl.BlockSpec((DFT_ROWS, 1), lambda i: (i, 0)),
                  pl.BlockSpec((HY_PAD, HY_PAD), const), pl.BlockSpec((1, HY_PAD), const),
                  pl.BlockSpec((HY_PAD, HY_PAD), const), pl.BlockSpec((1, HY_PAD), const),
                  pl.BlockSpec((1, HY_PAD), const), pl.BlockSpec((HY_PAD, w3.shape[1]), const),
                  pl.BlockSpec(decay.shape, const)],
        out_specs=[pl.BlockSpec((HY_ORDER, DFT_ROWS, width), lambda i: (0, i, 0)),
                   pl.BlockSpec((HY_ORDER, width), const)],
        out_shape=[jax.ShapeDtypeStruct((HY_ORDER, n, width), F32),
                   jax.ShapeDtypeStruct((HY_ORDER, width), F32)],
        compiler_params=_cparams(1), name="hyena_filter")(*args)


DFT_ROWS = ROW_G
GRP = 8
HY_LANES = 256
HY_ROWS = 16
HY_STEP_ROWS = 64


def _group_chunks(rows, width):
    return [(slice(r, r + HY_ROWS), slice(c, c + HY_LANES))
            for r in range(0, rows, HY_ROWS) for c in range(0, width, HY_LANES)]


def _grouped_matmul(w, pieces):
    pieces = [p.astype(F32) for p in pieces]
    outs = []
    for r in range(0, HY_ROWS, GRP):
        rhs = jnp.concatenate([p[r:r + GRP] for p in pieces], axis=0).astype(BF16)
        outs.append(jnp.dot(w, rhs, preferred_element_type=F32))
    return outs


def _piece(outs, idx):
    return jnp.concatenate([o[idx * GRP:(idx + 1) * GRP] for o in outs], axis=0)


def _hy_stage_a_kernel(x_ref, w_ref, ar_ref, ai_ref, *, tile0, n_in):
    planes = x_ref.shape[0]
    n_a = ar_ref.shape[0]
    for g, c in _group_chunks(ar_ref.shape[1], ar_ref.shape[2]):
        outs = _grouped_matmul(w_ref[...], [x_ref[q, tile0 + a, g, c] for q in range(planes) for a in range(n_in)])
        for k in range(n_a):
            ar_ref[k, g, c] = _piece(outs, k).astype(ar_ref.dtype)
            ai_ref[k, g, c] = _piece(outs, n_a + k).astype(ai_ref.dtype)


def hy_stage_a(x5, wmat, col0, width, tile0, n_in):
    planes, npair, ntile, rows, _ = x5.shape
    n_a = wmat.shape[0] // (2 * GRP)
    assert wmat.shape[1] == planes * n_in * GRP
    rb = HY_STEP_ROWS
    kern = functools.partial(_hy_stage_a_kernel, tile0=tile0, n_in=n_in)
    ospec = pl.BlockSpec((None, n_a, rb, width), lambda p, r: (p, 0, r, 0))
    return pl.pallas_call(
        kern, grid=(npair, rows // rb),
        in_specs=[pl.BlockSpec((planes, None, ntile, rb, width), lambda p, r: (0, p, 0, r, col0 // width)),
                  pl.BlockSpec(wmat.shape, lambda p, r: (0, 0))],
        out_specs=[ospec, ospec],
        out_shape=[jax.ShapeDtypeStruct((npair, n_a, rows, width), BF16)] * 2,
        compiler_params=_cparams(2), name="hyena_stage_a")(x5, wmat)


def _hy_stage_b_kernel(ar_ref, ai_ref, f_ref, *rest, spectrum_only):
    half = ar_ref.shape[0]
    a = jnp.concatenate([ar_ref[...], ai_ref[...]], axis=0)
    x = jnp.dot(f_ref[...], a, preferred_element_type=F32)
    if spectrum_only:
        norm_ref, or_ref, oi_ref = rest
        inv = 1.0 / norm_ref[...]
        or_ref[...] = x[:half] * inv
        oi_ref[...] = x[half:] * inv
        return
    g_ref, hr_ref, hi_ref, or_ref, oi_ref = rest
    xr, xi = x[:half], x[half:]
    hr, hi = hr_ref[...], hi_ref[...]
    y = jnp.concatenate([xr * hr - xi * hi, xr * hi + xi * hr], axis=0).astype(BF16)
    z = jnp.dot(g_ref[...], y, preferred_element_type=F32)
    or_ref[...] = z[:half].astype(or_ref.dtype)
    oi_ref[...] = z[half:].astype(oi_ref.dtype)


def hy_stage_b(ar, ai, fmat, gmat=None, hr=None, hi=None, order=None, norm=None):
    npair, n_a, rows, width = ar.shape
    spectrum_only = gmat is None
    aspec = pl.BlockSpec((None, None, rows, width), lambda k, p: (p, k, 0, 0))
    mspec = pl.BlockSpec((None, 2 * rows, 2 * rows), lambda k, p: (k, 0, 0))
    in_specs = [aspec, aspec, mspec]
    args = [ar, ai, fmat]
    if spectrum_only:
        in_specs.append(pl.BlockSpec((None, 1, width), lambda k, p: (p, 0, 0)))
        args.append(norm.reshape(npair, 1, width))
    else:
        hspec = pl.BlockSpec((None, None, rows, width), lambda k, p: (order, k, 0, 0))
        in_specs += [mspec, hspec, hspec]
        args += [gmat, hr, hi]
    out_dtype = F32 if spectrum_only else BF16
    return pl.pallas_call(
        functools.partial(_hy_stage_b_kernel, spectrum_only=spectrum_only), grid=(n_a, npair),
        in_specs=in_specs, out_specs=[aspec, aspec],
        out_shape=[jax.ShapeDtypeStruct(ar.shape, out_dtype)] * 2,
        compiler_params=_cparams(2), name="hyena_stage_b")(*args)


def _hy_stage_a_inv_kernel(ar_ref, ai_ref, w_ref, u_ref, g_ref, skip_ref, o_ref, *, tile0_u, tile0_g):
    n_a = ar_ref.shape[0]
    n_out = o_ref.shape[1]
    skip = skip_ref[...]
    for g, c in _group_chunks(ar_ref.shape[1], ar_ref.shape[2]):
        outs = _grouped_matmul(w_ref[...], [ar_ref[k, g, c] for k in range(n_a)]
                               + [ai_ref[k, g, c] for k in range(n_a)])
        for part in range(2):
            for a in range(n_out):
                conv = (_piece(outs, part * n_out + a)
                        + skip[:, c] * u_ref[part, tile0_u + a, g, c].astype(F32))
                gate = g_ref[part, tile0_g + a, g, c].astype(F32)
                o_ref[part, a, g, c] = (gate * conv).astype(o_ref.dtype)


def hy_stage_a_inv(ar, ai, wmat, u5, u_col0, tile0_u, g5, g_col0, tile0_g, skip, out_dtype):
    npair, n_a, rows, width = ar.shape
    n_out = n_a // 2
    assert wmat.shape == (2 * n_out * GRP, 2 * n_a * GRP)
    rb = HY_STEP_ROWS
    kern = functools.partial(_hy_stage_a_inv_kernel, tile0_u=tile0_u, tile0_g=tile0_g)
    aspec = pl.BlockSpec((None, n_a, rb, width), lambda p, r: (p, 0, r, 0))
    return pl.pallas_call(
        kern, grid=(npair, rows // rb),
        in_specs=[aspec, aspec, pl.BlockSpec(wmat.shape, lambda p, r: (0, 0)),
                  pl.BlockSpec((2, None, u5.shape[2], rb, width), lambda p, r: (0, p, 0, r, u_col0 // width)),
                  pl.BlockSpec((2, None, g5.shape[2], rb, width), lambda p, r: (0, p, 0, r, g_col0 // width)),
                  pl.BlockSpec((1, width), lambda p, r: (0, 0))],
        out_specs=pl.BlockSpec((2, None, n_out, rb, width), lambda p, r: (0, p, 0, r, 0)),
        out_shape=jax.ShapeDtypeStruct((2, npair, n_out, rows, width), out_dtype),
        compiler_params=_cparams(2), name="hyena_stage_a_inv")(ar, ai, wmat, u5, g5, skip.reshape(1, width))


def stage_a_matrices(n_a):
    h = n_a // 2
    idx = jnp.arange(n_a, dtype=jnp.int32)
    ang = (-2.0 * math.pi / n_a) * ((idx[:, None] * idx[None, :]) % n_a).astype(F32)
    wr, wi = jnp.cos(ang), jnp.sin(ang)
    eye = jnp.eye(GRP, dtype=F32)

    def expand(blocks):
        return jnp.kron(jnp.block(blocks), eye).astype(BF16)

    fwd = expand([[wr[:, :h], -wi[:, :h]], [wi[:, :h], wr[:, :h]]])
    fwd_real = expand([[wr], [wi]])
    inv = expand([[wr[:h], wi[:h]], [-wi[:h], wr[:h]]])
    return fwd, fwd_real, inv


def dft_matrices(n_a):
    n = n_a * DFT_ROWS
    kb = jnp.arange(DFT_ROWS, dtype=jnp.int32)[None, :, None]
    b = jnp.arange(DFT_ROWS, dtype=jnp.int32)[None, None, :]
    ka = jnp.arange(n_a, dtype=jnp.int32)[:, None, None]
    ang = (-2.0 * math.pi / n) * ((b * (ka + n_a * kb)) % n).astype(F32)
    fr, fi = jnp.cos(ang), jnp.sin(ang)
    blk = jnp.concatenate([jnp.concatenate([fr, -fi], axis=2), jnp.concatenate([fi, fr], axis=2)], axis=1)
    return blk.astype(BF16), (jnp.swapaxes(blk, 1, 2) * (1.0 / n)).astype(BF16)


def hyena_branch(u5, width, tile0, n_tiles, fargs, skip, dft):
    length = n_tiles * DFT_ROWS
    n_a = 2 * n_tiles
    fmat, gmat, a_fwd, a_fwd_real, a_inv = dft
    hc, norm = hyena_conv_filters(length, *fargs)
    hc = hc.reshape(1, HY_ORDER, n_a, DFT_ROWS, width)
    hr, hi = hy_stage_b(*hy_stage_a(hc, a_fwd_real, 0, width, 0, n_a), fmat, norm=norm)
    x5, x_col0, x_tile0 = u5, 0, tile0
    out = None
    for order in range(HY_ORDER):
        ar, ai = hy_stage_a(x5, a_fwd, x_col0, width, x_tile0, n_tiles)
        ar, ai = hy_stage_b(ar, ai, fmat, gmat, hr, hi, order=order)
        out = hy_stage_a_inv(ar, ai, a_inv, x5, x_col0, x_tile0, u5, (order + 1) * width, tile0, skip[order],
                             BF16)
        x5, x_col0, x_tile0 = out, 0, 0
    return out


def kernel(x, c, ctx, c_ctx, w_mod, b_mod, g_mix, g_ffn, w_in, b_gate, w_br, w_out, da_lambda, da_subln_g,
           lru_conv_w, lru_conv_b, lru_wa, lru_ba, lru_wi, lru_bi, lru_lambda, hy_conv_w, hy_conv_b,
           hy_f_w1, hy_f_b1, hy_f_w2, hy_f_b2, hy_f_freq, hy_f_w3, hy_decay, hy_skip, ffn_w_gate, ffn_w_up,
           ffn_w_down, moe_router, moe_w_gate, moe_w_up, moe_w_down, g_final):
    n_batch, length, d = x.shape
    n_ctx = ctx.shape[1]
    depth = w_mod.shape[0]
    assert n_ctx == ROW_G and length % ROW_G == 0 and d % 128 == 0
    tb = n_ctx + length
    gpb = tb // ROW_G
    nt = n_batch * tb
    c_end = w_in.shape[2]
    cb_k, cb_v, cb_lx, cb_q, cb_ly, cb_hy, cb_g = 0, 1, 2, 3, 4, 5, 8
    assert c_end == 11 * d

    xs = jnp.concatenate([ctx, x], axis=1).reshape(nt, d)
    silu_rows = jnp.concatenate([jax.nn.silu(c), jax.nn.silu(c_ctx)[None, :],
                                 jnp.zeros((16 - n_batch - 1, d), F32)], axis=0)
    tables = rope_tables(length, n_ctx, ROPE_LANES)
    dft_ctx = dft_matrices(2) + stage_a_matrices(2)
    dft_lat = dft_matrices(2 * (gpb - 1)) + stage_a_matrices(2 * (gpb - 1))

    for li in range(depth):
        lam_init = 0.8 - 0.6 * math.exp(-0.3 * li)
        tab = mm_f32_bias(silu_rows, w_mod[li], b_mod[li], d)
        tab = tab.reshape(16, 6, d).transpose(1, 0, 2)

        h = norm_mod(xs, g_mix[li], tab, 0, 1, gpb, n_batch)
        p = in_proj(h, w_in[li].astype(BF16), tables, gpb, cb_k, cb_q)
        a_out = diff_attention(p, d, da_lambda[li], da_subln_g[li], lam_init, n_batch, gpb, cb_q, cb_k, cb_v)

        bw = d // LRU_BLOCKS
        r_out = None
        hf = None
        for direction in range(2):
            w_cat = jnp.concatenate([lru_wa[li, direction], lru_wi[li, direction]], axis=-1).astype(BF16)
            res = lru_pass(p, lru_conv_w[li], lru_conv_b[li], w_cat, lru_ba[li, direction],
                           lru_bi[li, direction], lru_lambda[li, direction], n_batch, gpb, cb_lx,
                           reverse=direction == 1, hf=hf, cb_y=cb_ly)
            if direction == 0:
                hf = res
            else:
                r_out = res
        del bw

        u = hyena_short_conv(p, hy_conv_w[li], hy_conv_b[li], n_batch, gpb, cb_hy, d)
        u5 = u.reshape(2, n_batch // 2, gpb, ROW_G, 3 * d)
        fargs = (hy_f_w1[li], hy_f_b1[li], hy_f_w2[li], hy_f_b2[li], hy_f_freq[li], hy_f_w3[li], hy_decay[li])
        y_c = hyena_branch(u5, d, 0, 1, fargs, hy_skip[li], dft_ctx)
        y_l = hyena_branch(u5, d, 1, gpb - 1, fargs, hy_skip[li], dft_lat)
        y_out = jnp.concatenate([y_c, y_l], axis=2).reshape(nt, d)

        xs = merge_branches(xs, a_out, r_out, y_out, p, b_gate[li], w_br[li].astype(BF16),
                            w_out[li].astype(BF16), tab, 2, gpb, n_batch, cb_g)

        jj = li // 2
        if li % 2 == 0:
            f = norm_mod(xs, g_ffn[li], tab, 3, 4, gpb, n_batch)
            hh = gate_up(f, ffn_w_gate[jj].astype(BF16), ffn_w_up[jj].astype(BF16))
            xs = down_residual(hh, ffn_w_down[jj].astype(BF16), xs, tab, 5, gpb, n_batch)
        else:
            wr = jnp.concatenate([moe_router[jj], jnp.zeros((d, 128 - N_EXPERTS), F32)], axis=1)
            f, logits = norm_mod(xs, g_ffn[li], tab, 3, 4, gpb, n_batch, w_router=wr)
            xs = moe_layer(f, logits, moe_w_gate[jj].astype(BF16), moe_w_up[jj].astype(BF16),
                           moe_w_down[jj].astype(BF16), xs, tab, 5, gpb, n_batch)

    return final_norm(xs.reshape(n_batch, tb, d), g_final, n_ctx)
```
